```python
import math
import jax, jax.numpy as jnp
from jax import lax
import numpy as np

D_MODEL = 1024
BATCH = 32
SEQ = 256
DEPTH = 2
DEC_BATCH = 8
DEC_SEQ = 1024
PAST_LEN = 512

GRID_W = 64
D_MIX = D_MODEL
N_DIFF_HEADS = 4
DIFF_HEAD_DIM = 64
DIFF_V_DIM = 2 * DIFF_HEAD_DIM
D_DIFF_QK = N_DIFF_HEADS * 2 * DIFF_HEAD_DIM
D_DIFF = N_DIFF_HEADS * DIFF_V_DIM
N_MLA_HEADS = 4
MLA_NOPE_DIM = 64
MLA_ROPE_DIM = 32
MLA_V_DIM = 128
Q_LORA = 256
KV_LORA = 128
D_MLA = N_MLA_HEADS * MLA_V_DIM
D_IN_PROJ = 2 * D_DIFF_QK + D_DIFF + Q_LORA + KV_LORA + MLA_ROPE_DIM
D_FF = 2816
N_EXPERTS = 8
TOP_K = 2
D_FF_EXPERT = 1408
N_DENSE = (DEPTH + 1) // 2
N_MOE = DEPTH // 2
ROPE_THETA = 10000.0
RMS_EPS = 1e-6
Q_BLOCK = 128

kernel_name = "hymba_diffmla_diffusion_step"


def _rmsnorm(x, g):
    x32 = x.astype(jnp.float32)
    y = x32 * lax.rsqrt(jnp.mean(x32 * x32, axis=-1, keepdims=True) + RMS_EPS)
    return (y * g.astype(jnp.float32)).astype(x.dtype)


def _rope_1d(x, pos):
    half = x.shape[-1] // 2
    freqs = ROPE_THETA ** (-jnp.arange(half, dtype=jnp.float32) / half)
    ang = pos.astype(jnp.float32)[:, None] * freqs[None, :]
    cos = jnp.cos(ang)[None, :, None, :]
    sin = jnp.sin(ang)[None, :, None, :]
    x1, x2 = x[..., :half], x[..., half:]
    return jnp.concatenate([x1 * cos - x2 * sin, x1 * sin + x2 * cos], axis=-1).astype(x.dtype)


def _axial_rope(x, rows, cols):
    r = x.shape[-1] // 2
    return jnp.concatenate([_rope_1d(x[..., :r], rows), _rope_1d(x[..., r:], cols)], axis=-1)


def _sweep_query_blocks(fn, qs):
    B, Tq = qs[0].shape[:2]
    blk = min(Q_BLOCK, Tq)
    nb = Tq // blk
    qb = tuple(jnp.moveaxis(q.reshape(B, nb, blk, *q.shape[2:]), 1, 0) for q in qs)
    out = lax.map(lambda a: fn(*a), qb)
    return jnp.moveaxis(out, 0, 1).reshape(B, Tq, *out.shape[3:])


def _swiglu(x, wg, wu, wd):
    return (jax.nn.silu(x @ wg) * (x @ wu)) @ wd


def _moe(h, m, P):
    B, T, D = h.shape
    xf = h.reshape(B * T, D)
    logits = jnp.einsum('nd,de->ne', xf, P['router'][m], preferred_element_type=jnp.float32)
    probs = jax.nn.softmax(logits, axis=-1)
    top_p, top_i = lax.top_k(probs, TOP_K)
    top_p = top_p / jnp.sum(top_p, axis=-1, keepdims=True)
    gates = jnp.sum(jax.nn.one_hot(top_i, N_EXPERTS, dtype=jnp.float32) * top_p[..., None], axis=1)
    y = jnp.zeros_like(xf)
    for e in range(N_EXPERTS):
        y = y + gates[:, e:e + 1].astype(xf.dtype) * _swiglu(
            xf, P['moe_w_gate'][m, e], P['moe_w_up'][m, e], P['moe_w_down'][m, e])
    return y.reshape(B, T, D)


def _mixer(h, l, P, pos, ctx):
    B, T, _ = h.shape
    z = h @ P['w_in'][l]
    o1 = D_DIFF_QK
    o2 = o1 + D_DIFF_QK
    o3 = o2 + D_DIFF
    o4 = o3 + Q_LORA
    o5 = o4 + KV_LORA
    qd, kd, vd, cq, ckv, kr = jnp.split(z, [o1, o2, o3, o4, o5], axis=-1)
    qd = qd.reshape(B, T, 2 * N_DIFF_HEADS, DIFF_HEAD_DIM)
    kd = kd.reshape(B, T, 2 * N_DIFF_HEADS, DIFF_HEAD_DIM)
    vd = vd.reshape(B, T, N_DIFF_HEADS, DIFF_V_DIM)
    q_mla = (_rmsnorm(cq, P['mla_q_norm_g'][l]) @ P['w_uq'][l]).reshape(
        B, T, N_MLA_HEADS, MLA_NOPE_DIM + MLA_ROPE_DIM)
    qn, qr = q_mla[..., :MLA_NOPE_DIM], q_mla[..., MLA_NOPE_DIM:]
    ckv = _rmsnorm(ckv, P['mla_kv_norm_g'][l])
    kr = kr[:, :, None, :]
    if pos is not None:
        rows, cols = pos
        qd = _axial_rope(qd, rows, cols)
        kd = _axial_rope(kd, rows, cols)
        qr = _axial_rope(qr, rows, cols)
        kr = _axial_rope(kr, rows, cols)
    kd = kd.reshape(B, T, N_DIFF_HEADS, 2 * DIFF_HEAD_DIM)
    kr = kr[:, :, 0, :]
    own = (kd, vd, ckv, kr)
    if ctx is None:
        k_all, v_all, ckv_all, kr_all = own
    else:
        k_all, v_all, ckv_all, kr_all = (jnp.concatenate([ctx[0], kd], axis=1),
                                         jnp.concatenate([ctx[1], vd], axis=1),
                                         jnp.concatenate([ctx[2], ckv], axis=1),
                                         jnp.concatenate([ctx[3], kr], axis=1))
    Tk = k_all.shape[1]

    k5 = k_all.reshape(B, Tk, N_DIFF_HEADS, 2, DIFF_HEAD_DIM)
    k1, k2 = k5[..., 0, :], k5[..., 1, :]
    q5 = qd.reshape(B, T, N_DIFF_HEADS, 2, DIFF_HEAD_DIM)
    q1, q2 = q5[..., 0, :], q5[..., 1, :]
    lam_init = 0.8 - 0.6 * math.exp(-0.3 * l)
    f32 = jnp.float32
    lam = (jnp.exp(jnp.sum(P['diff_lq1'][l].astype(f32) * P['diff_lk1'][l].astype(f32)))
           - jnp.exp(jnp.sum(P['diff_lq2'][l].astype(f32) * P['diff_lk2'][l].astype(f32)))
           + lam_init)
    d_scale = DIFF_HEAD_DIM ** -0.5

    def diff_block(q1b, q2b):
        s1 = jnp.einsum('bqhd,bkhd->bhqk', q1b, k1, preferred_element_type=f32) * d_scale
        s2 = jnp.einsum('bqhd,bkhd->bhqk', q2b, k2, preferred_element_type=f32) * d_scale
        p = jax.nn.softmax(s1, axis=-1) - lam * jax.nn.softmax(s2, axis=-1)
        return jnp.einsum('bhqk,bkhe->bqhe', p.astype(v_all.dtype), v_all)

    diff_out = _sweep_query_blocks(diff_block, (q1, q2))
    diff_out = _rmsnorm(diff_out, P['diff_subln_g'][l]) * (1.0 - lam_init)

    k_nope = (ckv_all @ P['w_uk'][l]).reshape(B, Tk, N_MLA_HEADS, MLA_NOPE_DIM)
    v_mla = (ckv_all @ P['w_uv'][l]).reshape(B, Tk, N_MLA_HEADS, MLA_V_DIM)
    m_scale = (MLA_NOPE_DIM + MLA_ROPE_DIM) ** -0.5

    def mla_block(qnb, qrb):
        s = (jnp.einsum('bqhd,bkhd->bhqk', qnb, k_nope, preferred_element_type=f32)
             + jnp.einsum('bqhr,bkr->bhqk', qrb, kr_all, preferred_element_type=f32)) * m_scale
        p = jax.nn.softmax(s, axis=-1)
        return jnp.einsum('bhqk,bkhe->bqhe', p.astype(v_mla.dtype), v_mla)

    mla_out = _sweep_query_blocks(mla_block, (qn, qr))

    mixed = jnp.concatenate([diff_out.reshape(B, T, D_DIFF), mla_out.reshape(B, T, D_MLA)], axis=-1)
    return mixed @ P['w_o'][l], own


def _ada(cond, l, P):
    return (jax.nn.silu(cond) @ P['w_ada'][l] + P['b_ada'][l])[:, None, :]


def _layer(x, mod, l, P, pos, ctx):
    sm, scm, gm, sf, scf, gf = jnp.split(mod, 6, axis=-1)
    h = _rmsnorm(x, P['norm_mix_g'][l]) * (1 + scm) + sm
    a, own = _mixer(h, l, P, pos, ctx)
    x = x + gm * a
    h = _rmsnorm(x, P['norm_ffn_g'][l]) * (1 + scf) + sf
    if l % 2 == 0:
        i = l // 2
        f = _swiglu(h, P['w_gate'][i], P['w_up'][i], P['w_down'][i])
    else:
        f = _moe(h, l // 2, P)
    x = x + gf * f
    return x, own


def setup_inputs(seed: int = 0) -> dict:
    key = jax.random.key(seed)
    ks = jax.random.split(key, 40)
    f32 = jnp.float32

    def nrm(k, shape, scale=1.0):
        return jax.random.normal(k, shape, f32) * scale

    def gain(k, shape):
        return 1.0 + 0.05 * jax.random.normal(k, shape, f32)

    D = D_MODEL
    return {
        'x_prompt': nrm(ks[0], (BATCH, SEQ, D)),
        'x_sample': nrm(ks[1], (DEC_BATCH, DEC_SEQ, D)),
        'cache_diff_k': nrm(ks[2], (DEC_BATCH, DEPTH, PAST_LEN, N_DIFF_HEADS, 2 * DIFF_HEAD_DIM)),
        'cache_diff_v': nrm(ks[3], (DEC_BATCH, DEPTH, PAST_LEN, N_DIFF_HEADS, DIFF_V_DIM)),
        'cache_mla_ckv': nrm(ks[4], (DEC_BATCH, DEPTH, PAST_LEN, KV_LORA)),
        'cache_mla_krope': nrm(ks[5], (DEC_BATCH, DEPTH, PAST_LEN, MLA_ROPE_DIM)),
        'c': nrm(ks[6], (DEC_BATCH, D)),
        'c_ctx': nrm(ks[7], (D,)),
        'w_ada': nrm(ks[8], (DEPTH, D, 6 * D), 0.5 * D ** -0.5),
        'b_ada': nrm(ks[9], (DEPTH, 6 * D), 0.01),
        'norm_mix_g': gain(ks[10], (DEPTH, D)),
        'norm_ffn_g': gain(ks[11], (DEPTH, D)),
        'w_in': nrm(ks[12], (DEPTH, D, D_IN_PROJ), D ** -0.5),
        'mla_q_norm_g': gain(ks[13], (DEPTH, Q_LORA)),
        'mla_kv_norm_g': gain(ks[14], (DEPTH, KV_LORA)),
        'w_uq': nrm(ks[15], (DEPTH, Q_LORA, N_MLA_HEADS * (MLA_NOPE_DIM + MLA_ROPE_DIM)), Q_LORA ** -0.5),
        'w_uk': nrm(ks[16], (DEPTH, KV_LORA, N_MLA_HEADS * MLA_NOPE_DIM), KV_LORA ** -0.5),
        'w_uv': nrm(ks[17], (DEPTH, KV_LORA, N_MLA_HEADS * MLA_V_DIM), KV_LORA ** -0.5),
        'diff_lq1': nrm(ks[18], (DEPTH, DIFF_HEAD_DIM), 0.1),
        'diff_lk1': nrm(ks[19], (DEPTH, DIFF_HEAD_DIM), 0.1),
        'diff_lq2': nrm(ks[20], (DEPTH, DIFF_HEAD_DIM), 0.1),
        'diff_lk2': nrm(ks[21], (DEPTH, DIFF_HEAD_DIM), 0.1),
        'diff_subln_g': gain(ks[22], (DEPTH, DIFF_V_DIM)),
        'w_o': nrm(ks[23], (DEPTH, D_MIX, D), D_MIX ** -0.5),
        'w_gate': nrm(ks[24], (N_DENSE, D, D_FF), D ** -0.5),
        'w_up': nrm(ks[25], (N_DENSE, D, D_FF), D ** -0.5),
        'w_down': nrm(ks[26], (N_DENSE, D_FF, D), D_FF ** -0.5),
        'router': nrm(ks[27], (N_MOE, D, N_EXPERTS), D ** -0.5),
        'moe_w_gate': nrm(ks[28], (N_MOE, N_EXPERTS, D, D_FF_EXPERT), D ** -0.5),
        'moe_w_up': nrm(ks[29], (N_MOE, N_EXPERTS, D, D_FF_EXPERT), D ** -0.5),
        'moe_w_down': nrm(ks[30], (N_MOE, N_EXPERTS, D_FF_EXPERT, D), D_FF_EXPERT ** -0.5),
        'final_norm_g': gain(ks[31], (D,)),
    }


def reference(x_prompt, x_sample, cache_diff_k, cache_diff_v, cache_mla_ckv, cache_mla_krope,
              c, c_ctx, w_ada, b_ada, norm_mix_g, norm_ffn_g, w_in, mla_q_norm_g, mla_kv_norm_g,
              w_uq, w_uk, w_uv, diff_lq1, diff_lk1, diff_lq2, diff_lk2, diff_subln_g, w_o,
              w_gate, w_up, w_down, router, moe_w_gate, moe_w_up, moe_w_down, final_norm_g):
    P = {'w_ada': w_ada, 'b_ada': b_ada, 'norm_mix_g': norm_mix_g, 'norm_ffn_g': norm_ffn_g,
         'w_in': w_in, 'mla_q_norm_g': mla_q_norm_g, 'mla_kv_norm_g': mla_kv_norm_g,
         'w_uq': w_uq, 'w_uk': w_uk, 'w_uv': w_uv, 'diff_lq1': diff_lq1, 'diff_lk1': diff_lk1,
         'diff_lq2': diff_lq2, 'diff_lk2': diff_lk2, 'diff_subln_g': diff_subln_g, 'w_o': w_o,
         'w_gate': w_gate, 'w_up': w_up, 'w_down': w_down, 'router': router,
         'moe_w_gate': moe_w_gate, 'moe_w_up': moe_w_up, 'moe_w_down': moe_w_down}

    xp = x_prompt
    ks_, vs_, ckvs_, krs_ = [], [], [], []
    for l in range(DEPTH):
        xp, own = _layer(xp, _ada(c_ctx[None, :], l, P), l, P, None, None)
        ks_.append(own[0]); vs_.append(own[1]); ckvs_.append(own[2]); krs_.append(own[3])
    y_prompt = _rmsnorm(xp, final_norm_g)
    new_diff_k = jnp.stack(ks_, axis=1)
    new_diff_v = jnp.stack(vs_, axis=1)
    new_mla_ckv = jnp.stack(ckvs_, axis=1)
    new_mla_krope = jnp.stack(krs_, axis=1)

    T = x_sample.shape[1]
    n_rows = T // GRID_W
    rows = jnp.repeat(jnp.arange(n_rows, dtype=jnp.int32), GRID_W)
    cols = jnp.tile(jnp.arange(GRID_W, dtype=jnp.int32), n_rows)
    xs = x_sample
    for l in range(DEPTH):
        ctx = (cache_diff_k[:, l], cache_diff_v[:, l], cache_mla_ckv[:, l], cache_mla_krope[:, l])
        xs, _ = _layer(xs, _ada(c, l, P), l, P, (rows, cols), ctx)
    y_sample = _rmsnorm(xs, final_norm_g)

    return (y_prompt, y_sample, new_diff_k, new_diff_v, new_mla_ckv, new_mla_krope)
```

```python
import functools
import math

import jax
import jax.numpy as jnp
from jax import lax
from jax.experimental import pallas as pl
from jax.experimental.pallas import tpu as pltpu

F32 = jnp.float32
BF16 = jnp.bfloat16

D_MODEL = 1024
DEPTH = 2
GRID_W = 64
N_HEADS = 4
HEAD_W = 128
DIFF_HEAD_DIM = 64
MLA_NOPE_DIM = 64
MLA_ROPE_DIM = 32
Q_LORA = 256
KV_LORA = 128
D_FF = 2816
FF_CHUNK = 1408
N_EXPERTS = 8
D_FF_EXPERT = 1408
EXPERT_FF_CHUNKS = ((0, 512), (512, 512), (1024, 384))
ROPE_THETA = 10000.0
RMS_EPS = 1e-6
N_MOD = 6
COND_ROWS = 16

TOKEN_TILE = 256
MOE_TOKEN_TILE = 1024
VMEM_LIMIT_BYTES = 56 * 1024 * 1024

O_QD, O_KD, O_VD, O_CQ, O_CKV, O_KR, O_END = 0, 512, 1024, 1536, 1792, 1920, 1952
W_IN_PAD = 2048


def _params(sem):
    return pltpu.CompilerParams(dimension_semantics=sem, vmem_limit_bytes=VMEM_LIMIT_BYTES)


def _const_spec(shape):
    nd = len(shape)
    return pl.BlockSpec(shape, lambda *_: (0,) * nd)


def _rms(x, g):
    return x * lax.rsqrt(jnp.mean(x * x, axis=-1, keepdims=True) + RMS_EPS) * g


def _split_bf16(x):
    hi = x.astype(BF16)
    lo = (x - hi.astype(F32)).astype(BF16)
    return hi, lo


def _dot(a, b):
    return jnp.dot(a, b, preferred_element_type=F32)


def _dot3(a, b):
    a_hi, a_lo = _split_bf16(a)
    b_hi, b_lo = _split_bf16(b)
    return _dot(a_hi, b_hi) + _dot(a_hi, b_lo) + _dot(a_lo, b_hi)


def _ada_kernel(cond_ref, w_ref, b_ref, out_ref):
    cond = cond_ref[...]
    s = cond * jax.nn.sigmoid(cond)
    out_ref[0] = _dot3(s, w_ref[0]) + b_ref[0]


def _ada_table(cond, w_ada, b_ada):
    d = D_MODEL
    return pl.pallas_call(
        _ada_kernel,
        grid=(DEPTH, N_MOD),
        in_specs=[
            pl.BlockSpec((COND_ROWS, d), lambda l, j: (0, 0)),
            pl.BlockSpec((1, d, d), lambda l, j: (l, 0, j)),
            pl.BlockSpec((1, 1, d), lambda l, j: (l, 0, j)),
        ],
        out_specs=pl.BlockSpec((1, COND_ROWS, d), lambda l, j: (l, 0, j)),
        out_shape=jax.ShapeDtypeStruct((DEPTH, COND_ROWS, N_MOD * d), F32),
        compiler_params=_params(("arbitrary", "arbitrary")),
        name="ada_table",
    )(cond, w_ada, b_ada.reshape(DEPTH, 1, N_MOD * d))


def _axial_tables(seq, dim):
    half = dim // 4
    freqs = ROPE_THETA ** (-jnp.arange(half, dtype=F32) / half)
    pos = jnp.arange(seq, dtype=jnp.int32)
    rows = (pos // GRID_W).astype(F32)[:, None] * freqs[None, :]
    cols = (pos % GRID_W).astype(F32)[:, None] * freqs[None, :]
    cos = jnp.concatenate([jnp.cos(rows), jnp.cos(rows), jnp.cos(cols), jnp.cos(cols)], axis=-1)
    sin = jnp.concatenate([-jnp.sin(rows), jnp.sin(rows), -jnp.sin(cols), jnp.sin(cols)], axis=-1)
    return cos, sin


def _rope_tables(seq):
    cos64, sin64 = _axial_tables(seq, DIFF_HEAD_DIM)
    cos32, sin32 = _axial_tables(seq, MLA_ROPE_DIM)
    ones = lambda n: jnp.ones((seq, n), F32)
    zeros = lambda n: jnp.zeros((seq, n), F32)
    cos_d = jnp.tile(cos64, (1, 2 * N_HEADS))
    sin_d = jnp.tile(sin64, (1, 2 * N_HEADS))
    pad = HEAD_W - MLA_NOPE_DIM - MLA_ROPE_DIM
    cos_m = jnp.tile(jnp.concatenate([ones(MLA_NOPE_DIM), cos32, ones(pad)], axis=-1), (1, N_HEADS))
    sin_m = jnp.tile(jnp.concatenate([zeros(MLA_NOPE_DIM), sin32, zeros(pad)], axis=-1), (1, N_HEADS))
    cos_r = jnp.concatenate([cos32, ones(HEAD_W - MLA_ROPE_DIM)], axis=-1)
    sin_r = jnp.concatenate([sin32, zeros(HEAD_W - MLA_ROPE_DIM)], axis=-1)
    return cos_d, sin_d, cos_m, sin_m, cos_r, sin_r


def _rope(x, cos, sin, block):
    width = x.shape[-1]
    lane = lax.broadcasted_iota(jnp.int32, x.shape, 1)
    first = (lane % (2 * block)) < block
    partner = jnp.where(first, pltpu.roll(x, width - block, 1), pltpu.roll(x, block, 1))
    return x * cos + partner * sin


def _pre_mixer_kernel(*refs, rope, cache_out):
    it = iter(refs)
    x_ref, mod_ref, g_ref, win_ref, qg_ref, kvg_ref, wuq_ref, wkc_ref, wkr_ref, wuv_ref = (
        next(it) for _ in range(10))
    if rope:
        cos_d, sin_d, cos_m, sin_m, cos_r, sin_r = (next(it) for _ in range(6))
    qd_ref, kd_ref, vd_ref, qm_ref, km_ref, vm_ref = (next(it) for _ in range(6))
    if cache_out:
        kd32_ref, vd32_ref, ckv32_ref, kr32_ref = (next(it) for _ in range(4))

    x = x_ref[...]
    h = _rms(x, g_ref[...]) * (1.0 + mod_ref[0, 1:2, :]) + mod_ref[0, 0:1, :]
    z = _dot(h.astype(BF16), win_ref[...])

    qd = z[:, O_QD:O_KD]
    kd = z[:, O_KD:O_VD]
    vd = z[:, O_VD:O_CQ]
    cq = z[:, O_CQ:O_CKV]
    ckv = z[:, O_CKV:O_KR]
    kr = z[:, O_KR:W_IN_PAD]

    qm = _dot(_rms(cq, qg_ref[...]).astype(BF16), wuq_ref[...])
    ckv = _rms(ckv, kvg_ref[...])
    if rope:
        qd = _rope(qd, cos_d[...], sin_d[...], DIFF_HEAD_DIM // 4)
        kd = _rope(kd, cos_d[...], sin_d[...], DIFF_HEAD_DIM // 4)
        qm = _rope(qm, cos_m[...], sin_m[...], MLA_ROPE_DIM // 4)
        kr = _rope(kr, cos_r[...], sin_r[...], MLA_ROPE_DIM // 4)

    ckv_b = ckv.astype(BF16)
    qd_ref[...] = (qd * (DIFF_HEAD_DIM ** -0.5)).astype(BF16)
    kd_ref[...] = kd.astype(BF16)
    vd_ref[...] = vd.astype(BF16)
    qm_ref[...] = (qm * ((MLA_NOPE_DIM + MLA_ROPE_DIM) ** -0.5)).astype(BF16)
    km_ref[...] = (_dot(ckv_b, wkc_ref[...]) + _dot(kr.astype(BF16), wkr_ref[...])).astype(BF16)
    vm_ref[...] = _dot(ckv_b, wuv_ref[...]).astype(BF16)
    if cache_out:
        kd32_ref[...] = kd
        vd32_ref[...] = vd
        ckv32_ref[...] = ckv
        kr32_ref[...] = kr[:, :MLA_ROPE_DIM]


def _pre_mixer(x, mod, lw, tables, *, mod_row, cache_out):
    n = x.shape[0]
    tm = TOKEN_TILE
    rope = tables is not None
    row = lambda i: (i, 0)
    args = [x, mod, lw["norm_mix_g"], lw["w_in"], lw["q_norm_g"], lw["kv_norm_g"],
            lw["w_uq"], lw["w_kc"], lw["w_kr"], lw["w_uv"]]
    specs = [pl.BlockSpec((tm, D_MODEL), row),
             pl.BlockSpec((1, N_MOD, D_MODEL), lambda i: (mod_row(i), 0, 0))]
    specs += [_const_spec(a.shape) for a in args[2:]]
    if rope:
        tiles_per_seq = tables[0].shape[0] // tm
        for t in tables:
            args.append(t)
            specs.append(pl.BlockSpec((tm, t.shape[1]), lambda i: (i % tiles_per_seq, 0)))
    w4 = N_HEADS * HEAD_W
    out_shape = [jax.ShapeDtypeStruct((n, w4), BF16)] * 6
    out_specs = [pl.BlockSpec((tm, w4), row)] * 6
    if cache_out:
        for w in (w4, w4, KV_LORA, MLA_ROPE_DIM):
            out_shape.append(jax.ShapeDtypeStruct((n, w), F32))
            out_specs.append(pl.BlockSpec((tm, w), row))
    return pl.pallas_call(
        functools.partial(_pre_mixer_kernel, rope=rope, cache_out=cache_out),
        grid=(n // tm,),
        in_specs=specs,
        out_specs=out_specs,
        out_shape=out_shape,
        compiler_params=_params(("parallel",)),
        name="pre_mixer",
    )(*args)


def _softmax_terms(s):
    m = jnp.max(s, axis=-1, keepdims=True)
    e = jnp.exp(s - m)
    return e, 1.0 / jnp.sum(e, axis=-1, keepdims=True)


def _nt_dot(a, b):
    return lax.dot_general(a, b, (((1,), (1,)), ((), ())), preferred_element_type=F32)


def _attention_kernel(*refs, with_cache, lam_init):
    it = iter(refs)
    qd_ref, qm_ref, kd_ref, vd_ref, km_ref, vm_ref = (next(it) for _ in range(6))
    lq1, lk1, lq2, lk2, subg_ref = (next(it) for _ in range(5))
    if with_cache:
        ckd_ref, cvd_ref, cckv_ref, ckr_ref, wkc_ref, wkr_ref, wuv_ref = (next(it) for _ in range(7))
    out_ref = next(it)
    if with_cache:
        kd_all, vd_all, km_all, vm_all = (next(it) for _ in range(4))
        n_cache = ckd_ref.shape[2]

        @pl.when(pl.program_id(1) == 0)
        def _fill():
            kd_all[:n_cache, :] = ckd_ref[0, 0].astype(BF16)
            vd_all[:n_cache, :] = cvd_ref[0, 0].astype(BF16)
            cckv = cckv_ref[0, 0].astype(BF16)
            km_all[:n_cache, :] = (_dot(cckv, wkc_ref[...])
                                   + _dot(ckr_ref[0, 0].astype(BF16), wkr_ref[...])).astype(BF16)
            vm_all[:n_cache, :] = _dot(cckv, wuv_ref[...]).astype(BF16)
            kd_all[n_cache:, :] = kd_ref[...]
            vd_all[n_cache:, :] = vd_ref[...]
            km_all[n_cache:, :] = km_ref[...]
            vm_all[n_cache:, :] = vm_ref[...]
    else:
        kd_all, vd_all, km_all, vm_all = kd_ref, vd_ref, km_ref, vm_ref

    lam = (jnp.exp(jnp.sum(lq1[...] * lk1[...], axis=-1, keepdims=True))
           - jnp.exp(jnp.sum(lq2[...] * lk2[...], axis=-1, keepdims=True)) + lam_init)
    subg = subg_ref[...]
    tq = qd_ref.shape[0]
    lane = lax.broadcasted_iota(jnp.int32, (tq, HEAD_W), 1)
    first = lane < DIFF_HEAD_DIM

    for hd in range(N_HEADS):
        sl = slice(hd * HEAD_W, (hd + 1) * HEAD_W)
        q = qd_ref[:, sl]
        k = kd_all[:, sl]
        zero = jnp.zeros_like(q)
        e1, r1 = _softmax_terms(_nt_dot(jnp.where(first, q, zero), k))
        e2, r2 = _softmax_terms(_nt_dot(jnp.where(first, zero, q), k))
        p = (e1 * r1 - e2 * (lam * r2)).astype(BF16)
        o = _dot(p, vd_all[:, sl])
        out_ref[:, sl] = (_rms(o, subg) * (1.0 - lam_init)).astype(BF16)

    for hd in range(N_HEADS):
        sl = slice(hd * HEAD_W, (hd + 1) * HEAD_W)
        e, r = _softmax_terms(_nt_dot(qm_ref[:, sl], km_all[:, sl]))
        o = _dot(e.astype(BF16), vm_all[:, sl]) * r
        out_ref[:, N_HEADS * HEAD_W + hd * HEAD_W:N_HEADS * HEAD_W + (hd + 1) * HEAD_W] = o.astype(BF16)


def _attention(qkv, lw, layer, *, batch, seq, cache):
    qd, kd, vd, qm, km, vm = qkv
    w4 = N_HEADS * HEAD_W
    tq = TOKEN_TILE
    nq = seq // tq
    lam_init = 0.8 - 0.6 * math.exp(-0.3 * layer)
    q_spec = pl.BlockSpec((tq, w4), lambda b, j: (b * nq + j, 0))
    kv_spec = pl.BlockSpec((seq, w4), lambda b, j: (b, 0))
    args = [qd, qm, kd, vd, km, vm, lw["lq1"], lw["lk1"], lw["lq2"], lw["lk2"], lw["subln_g"]]
    specs = [q_spec, q_spec, kv_spec, kv_spec, kv_spec, kv_spec] + [_const_spec(a.shape) for a in args[6:]]
    scratch = []
    if cache is not None:
        ckd, cvd, cckv, ckr = cache
        n_cache = ckd.shape[2]
        for a in (ckd, cvd, cckv, ckr):
            args.append(a)
            specs.append(pl.BlockSpec((1, 1) + a.shape[2:], lambda b, j: (b, layer, 0, 0)))
        for name in ("w_kc", "w_kr_rows", "w_uv"):
            args.append(lw[name])
            specs.append(_const_spec(lw[name].shape))
        scratch = [pltpu.VMEM((n_cache + seq, w4), BF16)] * 4
    return pl.pallas_call(
        functools.partial(_attention_kernel, with_cache=cache is not None, lam_init=lam_init),
        grid=(batch, nq),
        in_specs=specs,
        out_specs=pl.BlockSpec((tq, 2 * w4), lambda b, j: (b * nq + j, 0)),
        out_shape=jax.ShapeDtypeStruct((batch * seq, 2 * w4), BF16),
        scratch_shapes=scratch,
        compiler_params=_params(("parallel", "arbitrary")),
        name="attention",
    )(*args)


def _post_mixer(x_ref, mixed_ref, mod_ref, wo_ref, g_ref):
    x1 = x_ref[...] + mod_ref[0, 2:3, :] * _dot(mixed_ref[...], wo_ref[...])
    h = _rms(x1, g_ref[...]) * (1.0 + mod_ref[0, 4:5, :]) + mod_ref[0, 3:4, :]
    return x1, h


def _silu(g):
    return g * jax.nn.sigmoid(g)


def _dense_ffn_kernel(x_ref, mixed_ref, mod_ref, wo_ref, g_ref, wg_ref, wu_ref, wd_ref, out_ref):
    x1, h = _post_mixer(x_ref, mixed_ref, mod_ref, wo_ref, g_ref)
    hb = h.astype(BF16)
    acc = jnp.zeros_like(x1)
    for c in range(D_FF // FF_CHUNK):
        sl = slice(c * FF_CHUNK, (c + 1) * FF_CHUNK)
        act = _silu(_dot(hb, wg_ref[:, sl])) * _dot(hb, wu_ref[:, sl])
        acc = acc + _dot(act.astype(BF16), wd_ref[sl, :])
    out_ref[...] = x1 + mod_ref[0, 5:6, :] * acc


def _dense_ffn(x, mixed, mod, lw, *, mod_row):
    n = x.shape[0]
    tm = TOKEN_TILE
    row = lambda i: (i, 0)
    weights = [lw["w_o"], lw["norm_ffn_g"], lw["w_gate"], lw["w_up"], lw["w_down"]]
    wspecs = [pl.BlockSpec(w.shape, lambda i: (0, 0), pipeline_mode=pl.Buffered(1)) for w in weights]
    return pl.pallas_call(
        _dense_ffn_kernel,
        grid=(n // tm,),
        in_specs=[pl.BlockSpec((tm, D_MODEL), row), pl.BlockSpec((tm, D_MODEL), row),
                  pl.BlockSpec((1, N_MOD, D_MODEL), lambda i: (mod_row(i), 0, 0))] + wspecs,
        out_specs=pl.BlockSpec((tm, D_MODEL), row),
        out_shape=jax.ShapeDtypeStruct((n, D_MODEL), F32),
        compiler_params=_params(("parallel",)),
        name="dense_ffn",
    )(x, mixed, mod, *weights)


def _router_kernel(x_ref, mixed_ref, mod_ref, wo_ref, g_ref, router_ref, x1_ref, hb_ref, gates_ref):
    x1, h = _post_mixer(x_ref, mixed_ref, mod_ref, wo_ref, g_ref)
    x1_ref[...] = x1
    hb_ref[...] = h.astype(BF16)
    logits = _dot3(h, router_ref[...])
    ex = jnp.exp(logits - jnp.max(logits, axis=-1, keepdims=True))
    probs = ex / jnp.sum(ex, axis=-1, keepdims=True)
    idx = lax.broadcasted_iota(jnp.int32, probs.shape, 1)
    p1 = jnp.max(probs, axis=-1, keepdims=True)
    sel1 = idx == jnp.min(jnp.where(probs == p1, idx, N_EXPERTS), axis=-1, keepdims=True)
    rest = jnp.where(sel1, -1.0, probs)
    p2 = jnp.max(rest, axis=-1, keepdims=True)
    sel2 = idx == jnp.min(jnp.where(rest == p2, idx, N_EXPERTS), axis=-1, keepdims=True)
    den = p1 + p2
    gates_ref[...] = jnp.where(sel1, p1 / den, 0.0) + jnp.where(sel2, p2 / den, 0.0)


def _router(x, mixed, mod, lw, *, mod_row):
    n = x.shape[0]
    tm = TOKEN_TILE
    row = lambda i: (i, 0)
    weights = [lw["w_o"], lw["norm_ffn_g"], lw["router"]]
    return pl.pallas_call(
        _router_kernel,
        grid=(n // tm,),
        in_specs=[pl.BlockSpec((tm, D_MODEL), row), pl.BlockSpec((tm, D_MODEL), row),
                  pl.BlockSpec((1, N_MOD, D_MODEL), lambda i: (mod_row(i), 0, 0))]
                 + [_const_spec(w.shape) for w in weights],
        out_specs=[pl.BlockSpec((tm, D_MODEL), row), pl.BlockSpec((tm, D_MODEL), row),
                   pl.BlockSpec((tm, N_EXPERTS), row)],
        out_shape=[jax.ShapeDtypeStruct((n, D_MODEL), F32), jax.ShapeDtypeStruct((n, D_MODEL), BF16),
                   jax.ShapeDtypeStruct((n, N_EXPERTS), F32)],
        compiler_params=_params(("parallel",)),
        name="router",
    )(x, mixed, mod, *weights)


def _moe_kernel(x1_ref, hb_ref, gates_ref, mod_ref, wg_ref, wu_ref, wd_ref, out_ref, acc_ref):
    e = pl.program_id(1)

    @pl.when(e == 0)
    def _init():
        acc_ref[...] = jnp.zeros_like(acc_ref)

    hb = hb_ref[...]
    lane = lax.broadcasted_iota(jnp.int32, gates_ref.shape, 1)
    gate = jnp.sum(jnp.where(lane == e, gates_ref[...], 0.0), axis=-1, keepdims=True)
    y = jnp.zeros(acc_ref.shape, F32)
    for start, size in EXPERT_FF_CHUNKS:
        sl = slice(start, start + size)
        act = _silu(_dot(hb, wg_ref[0, :, sl])) * _dot(hb, wu_ref[0, :, sl])
        y = y + _dot(act.astype(BF16), wd_ref[0, sl, :])
    acc_ref[...] += gate * y

    @pl.when(e == N_EXPERTS - 1)
    def _finish():
        out_ref[...] = x1_ref[...] + mod_ref[0, 5:6, :] * acc_ref[...]


def _moe(x1, hb, gates, mod, lw, *, mod_row):
    n = x1.shape[0]
    tm = MOE_TOKEN_TILE
    row = lambda i, e: (i, 0)
    wspec = lambda w: pl.BlockSpec((1,) + w.shape[1:], lambda i, e: (e, 0, 0))
    return pl.pallas_call(
        _moe_kernel,
        grid=(n // tm, N_EXPERTS),
        in_specs=[pl.BlockSpec((tm, D_MODEL), row), pl.BlockSpec((tm, D_MODEL), row),
                  pl.BlockSpec((tm, N_EXPERTS), row),
                  pl.BlockSpec((1, N_MOD, D_MODEL), lambda i, e: (mod_row(i), 0, 0)),
                  wspec(lw["moe_w_gate"]), wspec(lw["moe_w_up"]), wspec(lw["moe_w_down"])],
        out_specs=pl.BlockSpec((tm, D_MODEL), row),
        out_shape=jax.ShapeDtypeStruct((n, D_MODEL), F32),
        scratch_shapes=[pltpu.VMEM((tm, D_MODEL), F32)],
        compiler_params=_params(("parallel", "arbitrary")),
        name="moe",
    )(x1, hb, gates, mod, lw["moe_w_gate"], lw["moe_w_up"], lw["moe_w_down"])


def _final_norm_kernel(x_ref, g_ref, out_ref):
    out_ref[...] = _rms(x_ref[...], g_ref[...])


def _final_norm(x, g):
    n = x.shape[0]
    tm = MOE_TOKEN_TILE
    return pl.pallas_call(
        _final_norm_kernel,
        grid=(n // tm,),
        in_specs=[pl.BlockSpec((tm, D_MODEL), lambda i: (i, 0)), _const_spec(g.shape)],
        out_specs=pl.BlockSpec((tm, D_MODEL), lambda i: (i, 0)),
        out_shape=jax.ShapeDtypeStruct((n, D_MODEL), F32),
        compiler_params=_params(("parallel",)),
        name="final_norm",
    )(x, g)


def _head_slots(w, used):
    k = w.shape[0]
    w = w.reshape(k, N_HEADS, used)
    return jnp.pad(w, ((0, 0), (0, 0), (0, HEAD_W - used))).reshape(k, N_HEADS * HEAD_W)


def _layer_weights(l, P):
    row = lambda v: v.reshape(1, -1)
    place = jnp.zeros((HEAD_W, N_HEADS, HEAD_W), F32)
    r = jnp.arange(MLA_ROPE_DIM)
    place = place.at[r, :, MLA_NOPE_DIM + r].set(1.0).reshape(HEAD_W, N_HEADS * HEAD_W)
    lw = {
        "norm_mix_g": row(P["norm_mix_g"][l]),
        "norm_ffn_g": row(P["norm_ffn_g"][l]),
        "w_in": jnp.pad(P["w_in"][l], ((0, 0), (0, W_IN_PAD - O_END))).astype(BF16),
        "q_norm_g": row(P["mla_q_norm_g"][l]),
        "kv_norm_g": row(P["mla_kv_norm_g"][l]),
        "w_uq": _head_slots(P["w_uq"][l], MLA_NOPE_DIM + MLA_ROPE_DIM).astype(BF16),
        "w_kc": _head_slots(P["w_uk"][l], MLA_NOPE_DIM).astype(BF16),
        "w_kr": place.astype(BF16),
        "w_kr_rows": place[:MLA_ROPE_DIM].astype(BF16),
        "w_uv": P["w_uv"][l].astype(BF16),
        "lq1": row(P["diff_lq1"][l]), "lk1": row(P["diff_lk1"][l]),
        "lq2": row(P["diff_lq2"][l]), "lk2": row(P["diff_lk2"][l]),
        "subln_g": row(P["diff_subln_g"][l]),
        "w_o": P["w_o"][l].astype(BF16),
    }
    if l % 2 == 0:
        i = l // 2
        lw.update(w_gate=P["w_gate"][i].astype(BF16), w_up=P["w_up"][i].astype(BF16),
                  w_down=P["w_down"][i].astype(BF16))
    else:
        m = l // 2
        lw.update(router=P["router"][m], moe_w_gate=P["moe_w_gate"][m].astype(BF16),
                  moe_w_up=P["moe_w_up"][m].astype(BF16), moe_w_down=P["moe_w_down"][m].astype(BF16))
    return lw


def _run_group(x, mods, weights, *, batch, seq, mod_row, moe_mod_row, tables, caches, cache_out):
    own = []
    for l in range(DEPTH):
        lw = weights[l]
        outs = _pre_mixer(x, mods[l], lw, tables, mod_row=mod_row, cache_out=cache_out)
        own.append(outs[6:])
        mixed = _attention(outs[:6], lw, l, batch=batch, seq=seq, cache=caches)
        if l % 2 == 0:
            x = _dense_ffn(x, mixed, mods[l], lw, mod_row=mod_row)
        else:
            x1, hb, gates = _router(x, mixed, mods[l], lw, mod_row=mod_row)
            x = _moe(x1, hb, gates, mods[l], lw, mod_row=moe_mod_row)
    return x, own


def kernel(x_prompt, x_sample, cache_diff_k, cache_diff_v, cache_mla_ckv, cache_mla_krope, c, c_ctx, w_ada, b_ada, norm_mix_g, norm_ffn_g, w_in, mla_q_norm_g, mla_kv_norm_g, w_uq, w_uk, w_uv, diff_lq1, diff_lk1, diff_lq2, diff_lk2, diff_subln_g, w_o, w_gate, w_up, w_down, router, moe_w_gate, moe_w_up, moe_w_down, final_norm_g):
    P = dict(norm_mix_g=norm_mix_g, norm_ffn_g=norm_ffn_g, w_in=w_in, mla_q_norm_g=mla_q_norm_g,
             mla_kv_norm_g=mla_kv_norm_g, w_uq=w_uq, w_uk=w_uk, w_uv=w_uv, diff_lq1=diff_lq1,
             diff_lk1=diff_lk1, diff_lq2=diff_lq2, diff_lk2=diff_lk2, diff_subln_g=diff_subln_g,
             w_o=w_o, w_gate=w_gate, w_up=w_up, w_down=w_down, router=router,
             moe_w_gate=moe_w_gate, moe_w_up=moe_w_up, moe_w_down=moe_w_down)
    bp, sp, d = x_prompt.shape
    bs, ss, _ = x_sample.shape
    n_past = cache_diff_k.shape[2]
    w4 = N_HEADS * HEAD_W

    cond = jnp.zeros((COND_ROWS, d), F32).at[0].set(c_ctx).at[1:1 + bs].set(c)
    mod_all = _ada_table(cond, w_ada, b_ada).reshape(DEPTH, COND_ROWS, N_MOD, d)
    mods = [mod_all[l] for l in range(DEPTH)]
    weights = [_layer_weights(l, P) for l in range(DEPTH)]
    final_g = final_norm_g.reshape(1, d)

    xp, own = _run_group(
        x_prompt.reshape(bp * sp, d), mods, weights, batch=bp, seq=sp,
        mod_row=lambda i: 0, moe_mod_row=lambda i: 0, tables=None, caches=None, cache_out=True)
    y_prompt = _final_norm(xp, final_g).reshape(bp, sp, d)
    new_diff_k = jnp.stack([o[0].reshape(bp, sp, N_HEADS, HEAD_W) for o in own], axis=1)
    new_diff_v = jnp.stack([o[1].reshape(bp, sp, N_HEADS, HEAD_W) for o in own], axis=1)
    new_mla_ckv = jnp.stack([o[2].reshape(bp, sp, KV_LORA) for o in own], axis=1)
    new_mla_krope = jnp.stack([o[3].reshape(bp, sp, MLA_ROPE_DIM) for o in own], axis=1)

    tiles_per_seq = ss // TOKEN_TILE
    moe_tiles_per_seq = ss // MOE_TOKEN_TILE
    caches = (cache_diff_k.reshape(bs, DEPTH, n_past, w4), cache_diff_v.reshape(bs, DEPTH, n_past, w4),
              cache_mla_ckv, cache_mla_krope)
    xs, _ = _run_group(
        x_sample.reshape(bs * ss, d), mods, weights, batch=bs, seq=ss,
        mod_row=lambda i: 1 + i // tiles_per_seq, moe_mod_row=lambda i: 1 + i // moe_tiles_per_seq,
        tables=_rope_tables(ss), caches=caches, cache_out=False)
    y_sample = _final_norm(xs, final_g).reshape(bs, ss, d)

    return (y_prompt, y_sample, new_diff_k, new_diff_v, new_mla_ckv, new_mla_krope)
```

```python
import functools
import math

import jax
import jax.numpy as jnp
from jax import lax
from jax.experimental import pallas as pl
from jax.experimental.pallas import tpu as pltpu

F32 = jnp.float32
BF16 = jnp.bfloat16

D_MODEL = 1024
DEPTH = 2
GRID_W = 64
N_HEADS = 4
HEAD_W = 128
DIFF_HEAD_DIM = 64
MLA_NOPE_DIM = 64
MLA_ROPE_DIM = 32
Q_LORA = 256
KV_LORA = 128
D_FF = 2816
FF_CHUNK = 1408
N_EXPERTS = 8
D_FF_EXPERT = 1408
EXPERT_FF_CHUNKS = ((0, 512), (512, 512), (1024, 384))
ROPE_THETA = 10000.0
RMS_EPS = 1e-6
N_MOD = 6
COND_ROWS = 16

TOKEN_TILE = 256
MOE_ROW_TILE = 256
TOP_K = 2
VMEM_LIMIT_BYTES = 56 * 1024 * 1024

O_QD, O_KD, O_VD, O_CQ, O_CKV, O_KR, O_END = 0, 512, 1024, 1536, 1792, 1920, 1952
W_IN_PAD = 2048


def _params(sem):
    return pltpu.CompilerParams(dimension_semantics=sem, vmem_limit_bytes=VMEM_LIMIT_BYTES)


def _const_spec(shape):
    nd = len(shape)
    return pl.BlockSpec(shape, lambda *_: (0,) * nd)


def _rms(x, g):
    return x * lax.rsqrt(jnp.mean(x * x, axis=-1, keepdims=True) + RMS_EPS) * g


def _split_bf16(x):
    hi = x.astype(BF16)
    lo = (x - hi.astype(F32)).astype(BF16)
    return hi, lo


def _dot(a, b):
    return jnp.dot(a, b, preferred_element_type=F32)


def _dot3(a, b):
    a_hi, a_lo = _split_bf16(a)
    b_hi, b_lo = _split_bf16(b)
    return _dot(a_hi, b_hi) + _dot(a_hi, b_lo) + _dot(a_lo, b_hi)


def _ada_kernel(cond_ref, w_ref, b_ref, out_ref):
    cond = cond_ref[...]
    s = cond * jax.nn.sigmoid(cond)
    out_ref[0] = _dot3(s, w_ref[0]) + b_ref[0]


def _ada_table(cond, w_ada, b_ada):
    d = D_MODEL
    return pl.pallas_call(
        _ada_kernel,
        grid=(DEPTH, N_MOD),
        in_specs=[
            pl.BlockSpec((COND_ROWS, d), lambda l, j: (0, 0)),
            pl.BlockSpec((1, d, d), lambda l, j: (l, 0, j)),
            pl.BlockSpec((1, 1, d), lambda l, j: (l, 0, j)),
        ],
        out_specs=pl.BlockSpec((1, COND_ROWS, d), lambda l, j: (l, 0, j)),
        out_shape=jax.ShapeDtypeStruct((DEPTH, COND_ROWS, N_MOD * d), F32),
        compiler_params=_params(("arbitrary", "arbitrary")),
        name="ada_table",
    )(cond, w_ada, b_ada.reshape(DEPTH, 1, N_MOD * d))


def _axial_tables(seq, dim):
    half = dim // 4
    freqs = ROPE_THETA ** (-jnp.arange(half, dtype=F32) / half)
    pos = jnp.arange(seq, dtype=jnp.int32)
    rows = (pos // GRID_W).astype(F32)[:, None] * freqs[None, :]
    cols = (pos % GRID_W).astype(F32)[:, None] * freqs[None, :]
    cos = jnp.concatenate([jnp.cos(rows), jnp.cos(rows), jnp.cos(cols), jnp.cos(cols)], axis=-1)
    sin = jnp.concatenate([-jnp.sin(rows), jnp.sin(rows), -jnp.sin(cols), jnp.sin(cols)], axis=-1)
    return cos, sin


def _rope_tables(seq):
    cos64, sin64 = _axial_tables(seq, DIFF_HEAD_DIM)
    cos32, sin32 = _axial_tables(seq, MLA_ROPE_DIM)
    ones = lambda n: jnp.ones((seq, n), F32)
    zeros = lambda n: jnp.zeros((seq, n), F32)
    cos_d = jnp.tile(cos64, (1, 2 * N_HEADS))
    sin_d = jnp.tile(sin64, (1, 2 * N_HEADS))
    pad = HEAD_W - MLA_NOPE_DIM - MLA_ROPE_DIM
    cos_m = jnp.tile(jnp.concatenate([ones(MLA_NOPE_DIM), cos32, ones(pad)], axis=-1), (1, N_HEADS))
    sin_m = jnp.tile(jnp.concatenate([zeros(MLA_NOPE_DIM), sin32, zeros(pad)], axis=-1), (1, N_HEADS))
    cos_r = jnp.concatenate([cos32, ones(HEAD_W - MLA_ROPE_DIM)], axis=-1)
    sin_r = jnp.concatenate([sin32, zeros(HEAD_W - MLA_ROPE_DIM)], axis=-1)
    return cos_d, sin_d, cos_m, sin_m, cos_r, sin_r


def _rope(x, cos, sin, block):
    width = x.shape[-1]
    lane = lax.broadcasted_iota(jnp.int32, x.shape, 1)
    first = (lane % (2 * block)) < block
    partner = jnp.where(first, pltpu.roll(x, width - block, 1), pltpu.roll(x, block, 1))
    return x * cos + partner * sin


def _pre_mixer_kernel(*refs, rope, cache_out):
    it = iter(refs)
    x_ref, mod_ref, g_ref, win_ref, qg_ref, kvg_ref, wuq_ref, wkc_ref, wkr_ref, wuv_ref = (
        next(it) for _ in range(10))
    if rope:
        cos_d, sin_d, cos_m, sin_m, cos_r, sin_r = (next(it) for _ in range(6))
    qd_ref, kd_ref, vd_ref, qm_ref, km_ref, vm_ref = (next(it) for _ in range(6))
    if cache_out:
        kd32_ref, vd32_ref, ckv32_ref, kr32_ref = (next(it) for _ in range(4))

    x = x_ref[...]
    h = _rms(x, g_ref[...]) * (1.0 + mod_ref[0, 1:2, :]) + mod_ref[0, 0:1, :]
    z = _dot(h.astype(BF16), win_ref[...])

    qd = z[:, O_QD:O_KD]
    kd = z[:, O_KD:O_VD]
    vd = z[:, O_VD:O_CQ]
    cq = z[:, O_CQ:O_CKV]
    ckv = z[:, O_CKV:O_KR]
    kr = z[:, O_KR:W_IN_PAD]

    qm = _dot(_rms(cq, qg_ref[...]).astype(BF16), wuq_ref[...])
    ckv = _rms(ckv, kvg_ref[...])
    if rope:
        qd = _rope(qd, cos_d[...], sin_d[...], DIFF_HEAD_DIM // 4)
        kd = _rope(kd, cos_d[...], sin_d[...], DIFF_HEAD_DIM // 4)
        qm = _rope(qm, cos_m[...], sin_m[...], MLA_ROPE_DIM // 4)
        kr = _rope(kr, cos_r[...], sin_r[...], MLA_ROPE_DIM // 4)

    ckv_b = ckv.astype(BF16)
    qd_ref[...] = (qd * (DIFF_HEAD_DIM ** -0.5)).astype(BF16)
    kd_ref[...] = kd.astype(BF16)
    vd_ref[...] = vd.astype(BF16)
    qm_ref[...] = (qm * ((MLA_NOPE_DIM + MLA_ROPE_DIM) ** -0.5)).astype(BF16)
    km_ref[...] = (_dot(ckv_b, wkc_ref[...]) + _dot(kr.astype(BF16), wkr_ref[...])).astype(BF16)
    vm_ref[...] = _dot(ckv_b, wuv_ref[...]).astype(BF16)
    if cache_out:
        kd32_ref[...] = kd
        vd32_ref[...] = vd
        ckv32_ref[...] = ckv
        kr32_ref[...] = kr[:, :MLA_ROPE_DIM]


def _pre_mixer(x, mod, lw, tables, *, mod_row, cache_out):
    n = x.shape[0]
    tm = TOKEN_TILE
    rope = tables is not None
    row = lambda i: (i, 0)
    args = [x, mod, lw["norm_mix_g"], lw["w_in"], lw["q_norm_g"], lw["kv_norm_g"],
            lw["w_uq"], lw["w_kc"], lw["w_kr"], lw["w_uv"]]
    specs = [pl.BlockSpec((tm, D_MODEL), row),
             pl.BlockSpec((1, N_MOD, D_MODEL), lambda i: (mod_row(i), 0, 0))]
    specs += [_const_spec(a.shape) for a in args[2:]]
    if rope:
        tiles_per_seq = tables[0].shape[0] // tm
        for t in tables:
            args.append(t)
            specs.append(pl.BlockSpec((tm, t.shape[1]), lambda i: (i % tiles_per_seq, 0)))
    w4 = N_HEADS * HEAD_W
    out_shape = [jax.ShapeDtypeStruct((n, w4), BF16)] * 6
    out_specs = [pl.BlockSpec((tm, w4), row)] * 6
    if cache_out:
        for w in (w4, w4, KV_LORA, MLA_ROPE_DIM):
            out_shape.append(jax.ShapeDtypeStruct((n, w), F32))
            out_specs.append(pl.BlockSpec((tm, w), row))
    return pl.pallas_call(
        functools.partial(_pre_mixer_kernel, rope=rope, cache_out=cache_out),
        grid=(n // tm,),
        in_specs=specs,
        out_specs=out_specs,
        out_shape=out_shape,
        compiler_params=_params(("parallel",)),
        name="pre_mixer",
    )(*args)


def _softmax_terms(s):
    m = jnp.max(s, axis=-1, keepdims=True)
    e = jnp.exp(s - m)
    return e, 1.0 / jnp.sum(e, axis=-1, keepdims=True)


def _nt_dot(a, b):
    return lax.dot_general(a, b, (((1,), (1,)), ((), ())), preferred_element_type=F32)


def _attention_kernel(*refs, with_cache, lam_init):
    it = iter(refs)
    qd_ref, qm_ref, kd_ref, vd_ref, km_ref, vm_ref = (next(it) for _ in range(6))
    lq1, lk1, lq2, lk2, subg_ref = (next(it) for _ in range(5))
    if with_cache:
        ckd_ref, cvd_ref, cckv_ref, ckr_ref, wkc_ref, wkr_ref, wuv_ref = (next(it) for _ in range(7))
    out_ref = next(it)
    if with_cache:
        kd_all, vd_all, km_all, vm_all = (next(it) for _ in range(4))
        n_cache = ckd_ref.shape[2]

        @pl.when(pl.program_id(1) == 0)
        def _fill():
            kd_all[:n_cache, :] = ckd_ref[0, 0].astype(BF16)
            vd_all[:n_cache, :] = cvd_ref[0, 0].astype(BF16)
            cckv = cckv_ref[0, 0].astype(BF16)
            km_all[:n_cache, :] = (_dot(cckv, wkc_ref[...])
                                   + _dot(ckr_ref[0, 0].astype(BF16), wkr_ref[...])).astype(BF16)
            vm_all[:n_cache, :] = _dot(cckv, wuv_ref[...]).astype(BF16)
            kd_all[n_cache:, :] = kd_ref[...]
            vd_all[n_cache:, :] = vd_ref[...]
            km_all[n_cache:, :] = km_ref[...]
            vm_all[n_cache:, :] = vm_ref[...]
    else:
        kd_all, vd_all, km_all, vm_all = kd_ref, vd_ref, km_ref, vm_ref

    lam = (jnp.exp(jnp.sum(lq1[...] * lk1[...], axis=-1, keepdims=True))
           - jnp.exp(jnp.sum(lq2[...] * lk2[...], axis=-1, keepdims=True)) + lam_init)
    subg = subg_ref[...]
    tq = qd_ref.shape[0]
    lane = lax.broadcasted_iota(jnp.int32, (tq, HEAD_W), 1)
    first = lane < DIFF_HEAD_DIM

    for hd in range(N_HEADS):
        sl = slice(hd * HEAD_W, (hd + 1) * HEAD_W)
        q = qd_ref[:, sl]
        k = kd_all[:, sl]
        zero = jnp.zeros_like(q)
        e1, r1 = _softmax_terms(_nt_dot(jnp.where(first, q, zero), k))
        e2, r2 = _softmax_terms(_nt_dot(jnp.where(first, zero, q), k))
        p = (e1 * r1 - e2 * (lam * r2)).astype(BF16)
        o = _dot(p, vd_all[:, sl])
        out_ref[:, sl] = (_rms(o, subg) * (1.0 - lam_init)).astype(BF16)

    for hd in range(N_HEADS):
        sl = slice(hd * HEAD_W, (hd + 1) * HEAD_W)
        e, r = _softmax_terms(_nt_dot(qm_ref[:, sl], km_all[:, sl]))
        o = _dot(e.astype(BF16), vm_all[:, sl]) * r
        out_ref[:, N_HEADS * HEAD_W + hd * HEAD_W:N_HEADS * HEAD_W + (hd + 1) * HEAD_W] = o.astype(BF16)


def _attention(qkv, lw, layer, *, batch, seq, cache):
    qd, kd, vd, qm, km, vm = qkv
    w4 = N_HEADS * HEAD_W
    tq = TOKEN_TILE
    nq = seq // tq
    lam_init = 0.8 - 0.6 * math.exp(-0.3 * layer)
    q_spec = pl.BlockSpec((tq, w4), lambda b, j: (b * nq + j, 0))
    kv_spec = pl.BlockSpec((seq, w4), lambda b, j: (b, 0))
    args = [qd, qm, kd, vd, km, vm, lw["lq1"], lw["lk1"], lw["lq2"], lw["lk2"], lw["subln_g"]]
    specs = [q_spec, q_spec, kv_spec, kv_spec, kv_spec, kv_spec] + [_const_spec(a.shape) for a in args[6:]]
    scratch = []
    if cache is not None:
        ckd, cvd, cckv, ckr = cache
        n_cache = ckd.shape[2]
        for a in (ckd, cvd, cckv, ckr):
            args.append(a)
            specs.append(pl.BlockSpec((1, 1) + a.shape[2:], lambda b, j: (b, layer, 0, 0)))
        for name in ("w_kc", "w_kr_rows", "w_uv"):
            args.append(lw[name])
            specs.append(_const_spec(lw[name].shape))
        scratch = [pltpu.VMEM((n_cache + seq, w4), BF16)] * 4
    return pl.pallas_call(
        functools.partial(_attention_kernel, with_cache=cache is not None, lam_init=lam_init),
        grid=(batch, nq),
        in_specs=specs,
        out_specs=pl.BlockSpec((tq, 2 * w4), lambda b, j: (b * nq + j, 0)),
        out_shape=jax.ShapeDtypeStruct((batch * seq, 2 * w4), BF16),
        scratch_shapes=scratch,
        compiler_params=_params(("parallel", "arbitrary")),
        name="attention",
    )(*args)


def _post_mixer(x_ref, mixed_ref, mod_ref, wo_ref, g_ref):
    x1 = x_ref[...] + mod_ref[0, 2:3, :] * _dot(mixed_ref[...], wo_ref[...])
    h = _rms(x1, g_ref[...]) * (1.0 + mod_ref[0, 4:5, :]) + mod_ref[0, 3:4, :]
    return x1, h


def _silu(g):
    return g * jax.nn.sigmoid(g)


def _dense_ffn_kernel(x_ref, mixed_ref, mod_ref, wo_ref, g_ref, wg_ref, wu_ref, wd_ref, out_ref):
    x1, h = _post_mixer(x_ref, mixed_ref, mod_ref, wo_ref, g_ref)
    hb = h.astype(BF16)
    acc = jnp.zeros_like(x1)
    for c in range(D_FF // FF_CHUNK):
        sl = slice(c * FF_CHUNK, (c + 1) * FF_CHUNK)
        act = _silu(_dot(hb, wg_ref[:, sl])) * _dot(hb, wu_ref[:, sl])
        acc = acc + _dot(act.astype(BF16), wd_ref[sl, :])
    out_ref[...] = x1 + mod_ref[0, 5:6, :] * acc


def _dense_ffn(x, mixed, mod, lw, *, mod_row):
    n = x.shape[0]
    tm = TOKEN_TILE
    row = lambda i: (i, 0)
    weights = [lw["w_o"], lw["norm_ffn_g"], lw["w_gate"], lw["w_up"], lw["w_down"]]
    wspecs = [pl.BlockSpec(w.shape, lambda i: (0, 0), pipeline_mode=pl.Buffered(1)) for w in weights]
    return pl.pallas_call(
        _dense_ffn_kernel,
        grid=(n // tm,),
        in_specs=[pl.BlockSpec((tm, D_MODEL), row), pl.BlockSpec((tm, D_MODEL), row),
                  pl.BlockSpec((1, N_MOD, D_MODEL), lambda i: (mod_row(i), 0, 0))] + wspecs,
        out_specs=pl.BlockSpec((tm, D_MODEL), row),
        out_shape=jax.ShapeDtypeStruct((n, D_MODEL), F32),
        compiler_params=_params(("parallel",)),
        name="dense_ffn",
    )(x, mixed, mod, *weights)


def _router_kernel(x_ref, mixed_ref, mod_ref, wo_ref, g_ref, router_ref, x1_ref, h_ref, topi_ref, topg_ref):
    x1, h = _post_mixer(x_ref, mixed_ref, mod_ref, wo_ref, g_ref)
    x1_ref[...] = x1
    h_ref[...] = h
    logits = _dot3(h, router_ref[...])
    ex = jnp.exp(logits - jnp.max(logits, axis=-1, keepdims=True))
    probs = ex / jnp.sum(ex, axis=-1, keepdims=True)
    idx = lax.broadcasted_iota(jnp.int32, probs.shape, 1)
    p1 = jnp.max(probs, axis=-1, keepdims=True)
    i1 = jnp.min(jnp.where(probs == p1, idx, N_EXPERTS), axis=-1, keepdims=True)
    rest = jnp.where(idx == i1, -1.0, probs)
    p2 = jnp.max(rest, axis=-1, keepdims=True)
    i2 = jnp.min(jnp.where(rest == p2, idx, N_EXPERTS), axis=-1, keepdims=True)
    den = p1 + p2
    first = lax.broadcasted_iota(jnp.int32, topi_ref.shape, 1) == 0
    topi_ref[...] = jnp.where(first, i1, i2)
    topg_ref[...] = jnp.where(first, p1 / den, p2 / den)


def _router(x, mixed, mod, lw, *, mod_row):
    n = x.shape[0]
    tm = TOKEN_TILE
    row = lambda i: (i, 0)
    weights = [lw["w_o"], lw["norm_ffn_g"], lw["router"]]
    return pl.pallas_call(
        _router_kernel,
        grid=(n // tm,),
        in_specs=[pl.BlockSpec((tm, D_MODEL), row), pl.BlockSpec((tm, D_MODEL), row),
                  pl.BlockSpec((1, N_MOD, D_MODEL), lambda i: (mod_row(i), 0, 0))]
                 + [_const_spec(w.shape) for w in weights],
        out_specs=[pl.BlockSpec((tm, D_MODEL), row), pl.BlockSpec((tm, D_MODEL), row),
                   pl.BlockSpec((tm, TOP_K), row), pl.BlockSpec((tm, TOP_K), row)],
        out_shape=[jax.ShapeDtypeStruct((n, D_MODEL), F32), jax.ShapeDtypeStruct((n, D_MODEL), F32),
                   jax.ShapeDtypeStruct((n, TOP_K), jnp.int32), jax.ShapeDtypeStruct((n, TOP_K), F32)],
        compiler_params=_params(("parallel",)),
        name="router",
    )(x, mixed, mod, *weights)


def _route_plan(top_i, tm):
    e_flat = top_i.reshape(-1)
    onehot = (e_flat[:, None] == jnp.arange(N_EXPERTS, dtype=jnp.int32)[None, :]).astype(jnp.int32)
    csum = jnp.cumsum(onehot, axis=0)
    rank = jnp.sum(onehot * (csum - 1), axis=1)
    counts = csum[-1]
    gsize = ((counts + tm - 1) // tm) * tm
    gend = jnp.cumsum(gsize)
    dst = ((gend - gsize)[e_flat] + rank).astype(jnp.int32)
    n_tiles = _max_row_tiles(e_flat.shape[0], tm)
    n_used = (gend[-1] // tm).astype(jnp.int32)
    tile_start = jnp.minimum(jnp.arange(n_tiles, dtype=jnp.int32), n_used - 1) * tm
    tile_expert = jnp.sum((gend[None, :] <= tile_start[:, None]).astype(jnp.int32), axis=1)
    return dst, tile_expert.astype(jnp.int32), n_used.reshape(1), gsize.astype(jnp.int32), gend.astype(jnp.int32)


def _max_row_tiles(n_pairs, tm):
    return (n_pairs + N_EXPERTS * (tm - 1)) // tm


def _dispatch_kernel(dst_ref, gsize_ref, gend_ref, nu_ref, h_ref, xs_ref, zero_ref, sem):
    i = pl.program_id(0)
    tm = h_ref.shape[0]

    @pl.when(i == 0)
    def _zero_unused():
        zero_ref[...] = jnp.zeros_like(zero_ref)

        def zero_tile(start):
            cp = pltpu.make_async_copy(zero_ref, xs_ref.at[pl.ds(pl.multiple_of(start, tm), tm)], sem)
            cp.start()
            cp.wait()

        for e in range(N_EXPERTS):
            @pl.when(gsize_ref[e] > 0)
            def _():
                zero_tile(gend_ref[e] - tm)

        def unused(t, carry):
            zero_tile(t * tm)
            return carry

        lax.fori_loop(nu_ref[0], xs_ref.shape[0] // tm, unused, 0)

    base = i * (tm * TOP_K)

    def issue(r, carry):
        for k in range(TOP_K):
            d = dst_ref[base + TOP_K * r + k]
            pltpu.make_async_copy(h_ref.at[pl.ds(r, 1)], xs_ref.at[pl.ds(d, 1)], sem).start()
        return carry

    lax.fori_loop(0, tm, issue, 0, unroll=8)
    for _ in range(TOP_K):
        pltpu.make_async_copy(h_ref, xs_ref.at[pl.ds(0, tm)], sem).wait()


def _dispatch(h, plan, tm):
    dst, _, n_used, gsize, gend = plan
    n = h.shape[0]
    n_rows = _max_row_tiles(dst.shape[0], tm) * tm
    return pl.pallas_call(
        _dispatch_kernel,
        grid_spec=pltpu.PrefetchScalarGridSpec(
            num_scalar_prefetch=4,
            grid=(n // tm,),
            in_specs=[pl.BlockSpec((tm, D_MODEL), lambda i, *_: (i, 0))],
            out_specs=pl.BlockSpec(memory_space=pl.ANY),
            scratch_shapes=[pltpu.VMEM((tm, D_MODEL), F32), pltpu.SemaphoreType.DMA(())],
        ),
        out_shape=jax.ShapeDtypeStruct((n_rows, D_MODEL), F32),
        compiler_params=_params(("arbitrary",)),
        name="moe_dispatch",
    )(dst, gsize, gend, n_used, h)


def _expert_ffn_kernel(te_ref, nu_ref, xs_ref, wg_ref, wu_ref, wd_ref, ys_ref, wg_b, wu_b, wd_b):
    i = pl.program_id(0)

    @pl.when((i == 0) | (te_ref[i] != te_ref[jnp.maximum(i - 1, 0)]))
    def _new_expert():
        wg_b[...] = wg_ref[0].astype(BF16)
        wu_b[...] = wu_ref[0].astype(BF16)
        wd_b[...] = wd_ref[0].astype(BF16)

    @pl.when(i < nu_ref[0])
    def _compute():
        xb = xs_ref[...].astype(BF16)
        y = jnp.zeros(ys_ref.shape, F32)
        for start, size in EXPERT_FF_CHUNKS:
            sl = slice(start, start + size)
            act = _silu(_dot(xb, wg_b[:, sl])) * _dot(xb, wu_b[:, sl])
            y = y + _dot(act.astype(BF16), wd_b[sl, :])
        ys_ref[...] = y

    @pl.when(i >= nu_ref[0])
    def _unused():
        ys_ref[...] = jnp.zeros_like(ys_ref)


def _expert_ffn(xs, plan, lw, tm):
    _, tile_expert, n_used, _, _ = plan
    n_tiles = xs.shape[0] // tm
    wspec = lambda w: pl.BlockSpec((1,) + w.shape[1:], lambda i, te, nu: (te[i], 0, 0))
    return pl.pallas_call(
        _expert_ffn_kernel,
        grid_spec=pltpu.PrefetchScalarGridSpec(
            num_scalar_prefetch=2,
            grid=(n_tiles,),
            in_specs=[pl.BlockSpec((tm, D_MODEL), lambda i, te, nu: (jnp.minimum(i, nu[0] - 1), 0)),
                      wspec(lw["moe_w_gate"]), wspec(lw["moe_w_up"]), wspec(lw["moe_w_down"])],
            out_specs=pl.BlockSpec((tm, D_MODEL), lambda i, te, nu: (i, 0)),
            scratch_shapes=[pltpu.VMEM((D_MODEL, D_FF_EXPERT), BF16), pltpu.VMEM((D_MODEL, D_FF_EXPERT), BF16),
                            pltpu.VMEM((D_FF_EXPERT, D_MODEL), BF16)],
        ),
        out_shape=jax.ShapeDtypeStruct(xs.shape, F32),
        compiler_params=_params(("arbitrary",)),
        name="expert_ffn",
    )(tile_expert, n_used, xs, lw["moe_w_gate"], lw["moe_w_up"], lw["moe_w_down"])


def _combine_kernel(dst_ref, x1_ref, topg_ref, mod_ref, fg_ref, ys_ref, out_ref, buf, sem):
    i = pl.program_id(0)
    n = pl.num_programs(0)
    tm = x1_ref.shape[0]

    def issue(tile, slot):
        base = tile * (tm * TOP_K)

        def body(r, carry):
            for k in range(TOP_K):
                d = dst_ref[base + TOP_K * r + k]
                pltpu.make_async_copy(ys_ref.at[pl.ds(d, 1)], buf.at[slot, k, pl.ds(r, 1)],
                                      sem.at[slot]).start()
            return carry

        lax.fori_loop(0, tm, body, 0, unroll=8)

    @pl.when(i == 0)
    def _first():
        issue(0, 0)

    @pl.when(i + 1 < n)
    def _ahead():
        issue(i + 1, (i + 1) % 2)

    slot = i % 2
    for k in range(TOP_K):
        pltpu.make_async_copy(ys_ref.at[pl.ds(0, tm)], buf.at[slot, k], sem.at[slot]).wait()
    g = topg_ref[...]
    f = g[:, 0:1] * buf[slot, 0] + g[:, 1:2] * buf[slot, 1]
    out_ref[...] = _rms(x1_ref[...] + mod_ref[0, 5:6, :] * f, fg_ref[...])


def _combine(x1, top_g, mod, final_g, ys, plan, *, mod_row):
    dst = plan[0]
    n = x1.shape[0]
    tm = TOKEN_TILE
    row = lambda i, *_: (i, 0)
    return pl.pallas_call(
        _combine_kernel,
        grid_spec=pltpu.PrefetchScalarGridSpec(
            num_scalar_prefetch=1,
            grid=(n // tm,),
            in_specs=[pl.BlockSpec((tm, D_MODEL), row), pl.BlockSpec((tm, TOP_K), row),
                      pl.BlockSpec((1, N_MOD, D_MODEL), lambda i, *_: (mod_row(i), 0, 0)),
                      pl.BlockSpec(final_g.shape, lambda i, *_: (0, 0)),
                      pl.BlockSpec(memory_space=pl.ANY)],
            out_specs=pl.BlockSpec((tm, D_MODEL), row),
            scratch_shapes=[pltpu.VMEM((2, TOP_K, tm, D_MODEL), F32), pltpu.SemaphoreType.DMA((2,))],
        ),
        out_shape=jax.ShapeDtypeStruct((n, D_MODEL), F32),
        compiler_params=_params(("arbitrary",)),
        name="moe_combine",
    )(dst, x1, top_g, mod, final_g, ys)


def _sparse_moe(x, mixed, mod, lw, final_g, *, mod_row):
    x1, h, top_i, top_g = _router(x, mixed, mod, lw, mod_row=mod_row)
    plan = _route_plan(top_i, MOE_ROW_TILE)
    xs = _dispatch(h, plan, MOE_ROW_TILE)
    ys = _expert_ffn(xs, plan, lw, MOE_ROW_TILE)
    return _combine(x1, top_g, mod, final_g, ys, plan, mod_row=mod_row)


def _head_slots(w, used):
    k = w.shape[0]
    w = w.reshape(k, N_HEADS, used)
    return jnp.pad(w, ((0, 0), (0, 0), (0, HEAD_W - used))).reshape(k, N_HEADS * HEAD_W)


def _layer_weights(l, P):
    row = lambda v: v.reshape(1, -1)
    place = jnp.zeros((HEAD_W, N_HEADS, HEAD_W), F32)
    r = jnp.arange(MLA_ROPE_DIM)
    place = place.at[r, :, MLA_NOPE_DIM + r].set(1.0).reshape(HEAD_W, N_HEADS * HEAD_W)
    lw = {
        "norm_mix_g": row(P["norm_mix_g"][l]),
        "norm_ffn_g": row(P["norm_ffn_g"][l]),
        "w_in": jnp.pad(P["w_in"][l], ((0, 0), (0, W_IN_PAD - O_END))).astype(BF16),
        "q_norm_g": row(P["mla_q_norm_g"][l]),
        "kv_norm_g": row(P["mla_kv_norm_g"][l]),
        "w_uq": _head_slots(P["w_uq"][l], MLA_NOPE_DIM + MLA_ROPE_DIM).astype(BF16),
        "w_kc": _head_slots(P["w_uk"][l], MLA_NOPE_DIM).astype(BF16),
        "w_kr": place.astype(BF16),
        "w_kr_rows": place[:MLA_ROPE_DIM].astype(BF16),
        "w_uv": P["w_uv"][l].astype(BF16),
        "lq1": row(P["diff_lq1"][l]), "lk1": row(P["diff_lk1"][l]),
        "lq2": row(P["diff_lq2"][l]), "lk2": row(P["diff_lk2"][l]),
        "subln_g": row(P["diff_subln_g"][l]),
        "w_o": P["w_o"][l].astype(BF16),
    }
    if l % 2 == 0:
        i = l // 2
        lw.update(w_gate=P["w_gate"][i].astype(BF16), w_up=P["w_up"][i].astype(BF16),
                  w_down=P["w_down"][i].astype(BF16))
    else:
        m = l // 2
        lw.update(router=P["router"][m], moe_w_gate=P["moe_w_gate"][m],
                  moe_w_up=P["moe_w_up"][m], moe_w_down=P["moe_w_down"][m])
    return lw


def _run_group(x, mods, weights, final_g, *, batch, seq, mod_row, tables, caches, cache_out):
    assert DEPTH % 2 == 0
    own = []
    for l in range(DEPTH):
        lw = weights[l]
        outs = _pre_mixer(x, mods[l], lw, tables, mod_row=mod_row, cache_out=cache_out)
        own.append(outs[6:])
        mixed = _attention(outs[:6], lw, l, batch=batch, seq=seq, cache=caches)
        if l % 2 == 0:
            x = _dense_ffn(x, mixed, mods[l], lw, mod_row=mod_row)
        else:
            assert l == DEPTH - 1
            x = _sparse_moe(x, mixed, mods[l], lw, final_g, mod_row=mod_row)
    return x, own


def kernel(x_prompt, x_sample, cache_diff_k, cache_diff_v, cache_mla_ckv, cache_mla_krope, c, c_ctx, w_ada, b_ada, norm_mix_g, norm_ffn_g, w_in, mla_q_norm_g, mla_kv_norm_g, w_uq, w_uk, w_uv, diff_lq1, diff_lk1, diff_lq2, diff_lk2, diff_subln_g, w_o, w_gate, w_up, w_down, router, moe_w_gate, moe_w_up, moe_w_down, final_norm_g):
    P = dict(norm_mix_g=norm_mix_g, norm_ffn_g=norm_ffn_g, w_in=w_in, mla_q_norm_g=mla_q_norm_g,
             mla_kv_norm_g=mla_kv_norm_g, w_uq=w_uq, w_uk=w_uk, w_uv=w_uv, diff_lq1=diff_lq1,
             diff_lk1=diff_lk1, diff_lq2=diff_lq2, diff_lk2=diff_lk2, diff_subln_g=diff_subln_g,
             w_o=w_o, w_gate=w_gate, w_up=w_up, w_down=w_down, router=router,
             moe_w_gate=moe_w_gate, moe_w_up=moe_w_up, moe_w_down=moe_w_down)
    bp, sp, d = x_prompt.shape
    bs, ss, _ = x_sample.shape
    n_past = cache_diff_k.shape[2]
    w4 = N_HEADS * HEAD_W

    cond = jnp.zeros((COND_ROWS, d), F32).at[0].set(c_ctx).at[1:1 + bs].set(c)
    mod_all = _ada_table(cond, w_ada, b_ada).reshape(DEPTH, COND_ROWS, N_MOD, d)
    mods = [mod_all[l] for l in range(DEPTH)]
    weights = [_layer_weights(l, P) for l in range(DEPTH)]
    final_g = final_norm_g.reshape(1, d)

    yp, own = _run_group(
        x_prompt.reshape(bp * sp, d), mods, weights, final_g, batch=bp, seq=sp,
        mod_row=lambda i: 0, tables=None, caches=None, cache_out=True)
    y_prompt = yp.reshape(bp, sp, d)
    new_diff_k = jnp.stack([o[0].reshape(bp, sp, N_HEADS, HEAD_W) for o in own], axis=1)
    new_diff_v = jnp.stack([o[1].reshape(bp, sp, N_HEADS, HEAD_W) for o in own], axis=1)
    new_mla_ckv = jnp.stack([o[2].reshape(bp, sp, KV_LORA) for o in own], axis=1)
    new_mla_krope = jnp.stack([o[3].reshape(bp, sp, MLA_ROPE_DIM) for o in own], axis=1)

    tiles_per_seq = ss // TOKEN_TILE
    caches = (cache_diff_k.reshape(bs, DEPTH, n_past, w4), cache_diff_v.reshape(bs, DEPTH, n_past, w4),
              cache_mla_ckv, cache_mla_krope)
    ys, _ = _run_group(
        x_sample.reshape(bs * ss, d), mods, weights, final_g, batch=bs, seq=ss,
        mod_row=lambda i: 1 + i // tiles_per_seq,
        tables=_rope_tables(ss), caches=caches, cache_out=False)
    y_sample = ys.reshape(bs, ss, d)

    return (y_prompt, y_sample, new_diff_k, new_diff_v, new_mla_ckv, new_mla_krope)
```

```python
import functools
import math

import jax
import jax.numpy as jnp
from jax import lax
from jax.experimental import pallas as pl
from jax.experimental.pallas import tpu as pltpu

F32 = jnp.float32
BF16 = jnp.bfloat16

D_MODEL = 1024
DEPTH = 2
GRID_W = 64
N_HEADS = 4
HEAD_W = 128
DIFF_HEAD_DIM = 64
MLA_NOPE_DIM = 64
MLA_ROPE_DIM = 32
Q_LORA = 256
KV_LORA = 128
D_FF = 2816
FF_CHUNK = 1408
N_EXPERTS = 8
D_FF_EXPERT = 1408
EXPERT_FF_CHUNKS = ((0, 512), (512, 512), (1024, 384))
ROPE_THETA = 10000.0
RMS_EPS = 1e-6
N_MOD = 6
COND_ROWS = 16

TOKEN_TILE = 256
FFN_TOKEN_TILE = 512
MOE_ROW_TILE = 512
TOP_K = 2
VMEM_LIMIT_BYTES = 60 * 1024 * 1024

O_QD, O_KD, O_VD, O_CQ, O_CKV, O_KR, O_END = 0, 512, 1024, 1536, 1792, 1920, 1952
W_IN_PAD = 2048


def _params(sem):
    return pltpu.CompilerParams(dimension_semantics=sem, vmem_limit_bytes=VMEM_LIMIT_BYTES)


def _const_spec(shape):
    nd = len(shape)
    return pl.BlockSpec(shape, lambda *_: (0,) * nd)


def _rms(x, g):
    return x * lax.rsqrt(jnp.mean(x * x, axis=-1, keepdims=True) + RMS_EPS) * g


def _split_bf16(x):
    hi = x.astype(BF16)
    lo = (x - hi.astype(F32)).astype(BF16)
    return hi, lo


def _dot(a, b):
    return jnp.dot(a, b, preferred_element_type=F32)


def _dot3(a, b):
    a_hi, a_lo = _split_bf16(a)
    b_hi, b_lo = _split_bf16(b)
    return _dot(a_hi, b_hi) + _dot(a_hi, b_lo) + _dot(a_lo, b_hi)


def _ada_kernel(cond_ref, w_ref, b_ref, out_ref):
    cond = cond_ref[...]
    s = cond * jax.nn.sigmoid(cond)
    out_ref[0] = _dot3(s, w_ref[0]) + b_ref[0]


def _ada_table(cond, w_ada, b_ada):
    d = D_MODEL
    return pl.pallas_call(
        _ada_kernel,
        grid=(DEPTH, N_MOD),
        in_specs=[
            pl.BlockSpec((COND_ROWS, d), lambda l, j: (0, 0)),
            pl.BlockSpec((1, d, d), lambda l, j: (l, 0, j)),
            pl.BlockSpec((1, 1, d), lambda l, j: (l, 0, j)),
        ],
        out_specs=pl.BlockSpec((1, COND_ROWS, d), lambda l, j: (l, 0, j)),
        out_shape=jax.ShapeDtypeStruct((DEPTH, COND_ROWS, N_MOD * d), F32),
        compiler_params=_params(("arbitrary", "arbitrary")),
        name="ada_table",
    )(cond, w_ada, b_ada.reshape(DEPTH, 1, N_MOD * d))


def _axial_tables(seq, dim):
    half = dim // 4
    freqs = ROPE_THETA ** (-jnp.arange(half, dtype=F32) / half)
    pos = jnp.arange(seq, dtype=jnp.int32)
    rows = (pos // GRID_W).astype(F32)[:, None] * freqs[None, :]
    cols = (pos % GRID_W).astype(F32)[:, None] * freqs[None, :]
    cos = jnp.concatenate([jnp.cos(rows), jnp.cos(rows), jnp.cos(cols), jnp.cos(cols)], axis=-1)
    sin = jnp.concatenate([-jnp.sin(rows), jnp.sin(rows), -jnp.sin(cols), jnp.sin(cols)], axis=-1)
    return cos, sin


def _rope_tables(seq):
    cos64, sin64 = _axial_tables(seq, DIFF_HEAD_DIM)
    cos32, sin32 = _axial_tables(seq, MLA_ROPE_DIM)
    ones = lambda n: jnp.ones((seq, n), F32)
    zeros = lambda n: jnp.zeros((seq, n), F32)
    cos_d = jnp.tile(cos64, (1, 2 * N_HEADS))
    sin_d = jnp.tile(sin64, (1, 2 * N_HEADS))
    pad = HEAD_W - MLA_NOPE_DIM - MLA_ROPE_DIM
    cos_m = jnp.tile(jnp.concatenate([ones(MLA_NOPE_DIM), cos32, ones(pad)], axis=-1), (1, N_HEADS))
    sin_m = jnp.tile(jnp.concatenate([zeros(MLA_NOPE_DIM), sin32, zeros(pad)], axis=-1), (1, N_HEADS))
    cos_r = jnp.concatenate([cos32, ones(HEAD_W - MLA_ROPE_DIM)], axis=-1)
    sin_r = jnp.concatenate([sin32, zeros(HEAD_W - MLA_ROPE_DIM)], axis=-1)
    return cos_d, sin_d, cos_m, sin_m, cos_r, sin_r


def _rope(x, cos, sin, block):
    width = x.shape[-1]
    lane = lax.broadcasted_iota(jnp.int32, x.shape, 1)
    first = (lane % (2 * block)) < block
    partner = jnp.where(first, pltpu.roll(x, width - block, 1), pltpu.roll(x, block, 1))
    return x * cos + partner * sin


def _store_heads(ref, layer, x):
    for hd in range(N_HEADS):
        ref[0, layer, :, hd, :] = x[:, hd * HEAD_W:(hd + 1) * HEAD_W]


def _pre_mixer_kernel(*refs, rope, cache_out, n_prev):
    it = iter(refs)
    x_ref, mod_ref, g_ref, win_ref, qg_ref, kvg_ref, wuq_ref, wkc_ref, wkr_ref, wuv_ref = (
        next(it) for _ in range(10))
    if rope:
        cos_d, sin_d, cos_m, sin_m, cos_r, sin_r = (next(it) for _ in range(6))
    prev = [[next(it) for _ in range(4)] for _ in range(n_prev)]
    qd_ref, kd_ref, vd_ref, qm_ref, km_ref, vm_ref = (next(it) for _ in range(6))
    if cache_out:
        kd32_ref, vd32_ref, ckv32_ref, kr32_ref = (next(it) for _ in range(4))

    x = x_ref[...]
    h = _rms(x, g_ref[...]) * (1.0 + mod_ref[0, 1:2, :]) + mod_ref[0, 0:1, :]
    z = _dot(h.astype(BF16), win_ref[...])

    qd = z[:, O_QD:O_KD]
    kd = z[:, O_KD:O_VD]
    vd = z[:, O_VD:O_CQ]
    cq = z[:, O_CQ:O_CKV]
    ckv = z[:, O_CKV:O_KR]
    kr = z[:, O_KR:W_IN_PAD]

    qm = _dot(_rms(cq, qg_ref[...]).astype(BF16), wuq_ref[...])
    ckv = _rms(ckv, kvg_ref[...])
    if rope:
        qd = _rope(qd, cos_d[...], sin_d[...], DIFF_HEAD_DIM // 4)
        kd = _rope(kd, cos_d[...], sin_d[...], DIFF_HEAD_DIM // 4)
        qm = _rope(qm, cos_m[...], sin_m[...], MLA_ROPE_DIM // 4)
        kr = _rope(kr, cos_r[...], sin_r[...], MLA_ROPE_DIM // 4)

    ckv_b = ckv.astype(BF16)
    qd_ref[...] = (qd * (DIFF_HEAD_DIM ** -0.5)).astype(BF16)
    kd_ref[...] = kd.astype(BF16)
    vd_ref[...] = vd.astype(BF16)
    qm_ref[...] = (qm * ((MLA_NOPE_DIM + MLA_ROPE_DIM) ** -0.5)).astype(BF16)
    km_ref[...] = (_dot(ckv_b, wkc_ref[...]) + _dot(kr.astype(BF16), wkr_ref[...])).astype(BF16)
    vm_ref[...] = _dot(ckv_b, wuv_ref[...]).astype(BF16)
    if cache_out == "flat":
        kd32_ref[...] = kd
        vd32_ref[...] = vd
        ckv32_ref[...] = ckv
        kr32_ref[...] = kr[:, :MLA_ROPE_DIM]
    elif cache_out == "stacked":
        for l, (pk, pv, pc, pr) in enumerate(prev):
            _store_heads(kd32_ref, l, pk[...])
            _store_heads(vd32_ref, l, pv[...])
            ckv32_ref[0, l] = pc[...]
            kr32_ref[0, l] = pr[...]
        _store_heads(kd32_ref, n_prev, kd)
        _store_heads(vd32_ref, n_prev, vd)
        ckv32_ref[0, n_prev] = ckv
        kr32_ref[0, n_prev] = kr[:, :MLA_ROPE_DIM]


def _pre_mixer(x, mod, lw, tables, *, seq, mod_row, cache_out=None, prev=()):
    n = x.shape[0]
    tm = TOKEN_TILE
    tiles_per_seq = seq // tm
    rope = tables is not None
    row = lambda i: (i, 0)
    args = [x, mod, lw["norm_mix_g"], lw["w_in"], lw["q_norm_g"], lw["kv_norm_g"],
            lw["w_uq"], lw["w_kc"], lw["w_kr"], lw["w_uv"]]
    specs = [pl.BlockSpec((tm, D_MODEL), row),
             pl.BlockSpec((1, N_MOD, D_MODEL), lambda i: (mod_row(i, tm), 0, 0))]
    specs += [_const_spec(a.shape) for a in args[2:]]
    if rope:
        for t in tables:
            args.append(t)
            specs.append(pl.BlockSpec((tm, t.shape[1]), lambda i: (i % tiles_per_seq, 0)))
    for layer_rows in prev:
        for a in layer_rows:
            args.append(a)
            specs.append(pl.BlockSpec((tm, a.shape[1]), row))
    w4 = N_HEADS * HEAD_W
    out_shape = [jax.ShapeDtypeStruct((n, w4), BF16)] * 6
    out_specs = [pl.BlockSpec((tm, w4), row)] * 6
    if cache_out == "flat":
        for w in (w4, w4, KV_LORA, MLA_ROPE_DIM):
            out_shape.append(jax.ShapeDtypeStruct((n, w), F32))
            out_specs.append(pl.BlockSpec((tm, w), row))
    elif cache_out == "stacked":
        n_layers = len(prev) + 1
        for tail in ((N_HEADS, HEAD_W), (N_HEADS, HEAD_W), (KV_LORA,), (MLA_ROPE_DIM,)):
            zeros = (0,) * len(tail)
            out_shape.append(jax.ShapeDtypeStruct((n // seq, n_layers, seq) + tail, F32))
            out_specs.append(pl.BlockSpec(
                (1, n_layers, tm) + tail,
                lambda i, zeros=zeros: (i // tiles_per_seq, 0, i % tiles_per_seq) + zeros))
    return pl.pallas_call(
        functools.partial(_pre_mixer_kernel, rope=rope, cache_out=cache_out, n_prev=len(prev)),
        grid=(n // tm,),
        in_specs=specs,
        out_specs=out_specs,
        out_shape=out_shape,
        compiler_params=_params(("parallel",)),
        name="pre_mixer",
    )(*args)


def _softmax_terms(s):
    m = jnp.max(s, axis=-1, keepdims=True)
    e = jnp.exp(s - m)
    return e, 1.0 / jnp.sum(e, axis=-1, keepdims=True)


def _nt_dot(a, b):
    return lax.dot_general(a, b, (((1,), (1,)), ((), ())), preferred_element_type=F32)


def _attention_kernel(*refs, with_cache, lam_init):
    it = iter(refs)
    qd_ref, qm_ref, kd_ref, vd_ref, km_ref, vm_ref = (next(it) for _ in range(6))
    lq1, lk1, lq2, lk2, subg_ref = (next(it) for _ in range(5))
    if with_cache:
        ckd_ref, cvd_ref, cckv_ref, ckr_ref, wkc_ref, wkr_ref, wuv_ref = (next(it) for _ in range(7))
    out_ref = next(it)
    if with_cache:
        kd_all, vd_all, km_all, vm_all = (next(it) for _ in range(4))
        n_cache = cckv_ref.shape[2]

        @pl.when(pl.program_id(1) == 0)
        def _fill():
            kd_all[:n_cache, :] = ckd_ref[0, 0].astype(BF16)
            vd_all[:n_cache, :] = cvd_ref[0, 0].astype(BF16)
            cckv = cckv_ref[0, 0].astype(BF16)
            km_all[:n_cache, :] = (_dot(cckv, wkc_ref[...])
                                   + _dot(ckr_ref[0, 0].astype(BF16), wkr_ref[...])).astype(BF16)
            vm_all[:n_cache, :] = _dot(cckv, wuv_ref[...]).astype(BF16)
            kd_all[n_cache:, :] = kd_ref[...]
            vd_all[n_cache:, :] = vd_ref[...]
            km_all[n_cache:, :] = km_ref[...]
            vm_all[n_cache:, :] = vm_ref[...]
    else:
        kd_all, vd_all, km_all, vm_all = kd_ref, vd_ref, km_ref, vm_ref

    lam = (jnp.exp(jnp.sum(lq1[...] * lk1[...], axis=-1, keepdims=True))
           - jnp.exp(jnp.sum(lq2[...] * lk2[...], axis=-1, keepdims=True)) + lam_init)
    subg = subg_ref[...]
    tq = qd_ref.shape[0]
    lane = lax.broadcasted_iota(jnp.int32, (tq, HEAD_W), 1)
    first = lane < DIFF_HEAD_DIM

    for hd in range(N_HEADS):
        sl = slice(hd * HEAD_W, (hd + 1) * HEAD_W)
        q = qd_ref[:, sl]
        k = kd_all[:, sl]
        zero = jnp.zeros_like(q)
        e1, r1 = _softmax_terms(_nt_dot(jnp.where(first, q, zero), k))
        e2, r2 = _softmax_terms(_nt_dot(jnp.where(first, zero, q), k))
        p = (e1 * r1 - e2 * (lam * r2)).astype(BF16)
        o = _dot(p, vd_all[:, sl])
        out_ref[:, sl] = (_rms(o, subg) * (1.0 - lam_init)).astype(BF16)

    for hd in range(N_HEADS):
        sl = slice(hd * HEAD_W, (hd + 1) * HEAD_W)
        e, r = _softmax_terms(_nt_dot(qm_ref[:, sl], km_all[:, sl]))
        o = _dot(e.astype(BF16), vm_all[:, sl]) * r
        out_ref[:, N_HEADS * HEAD_W + hd * HEAD_W:N_HEADS * HEAD_W + (hd + 1) * HEAD_W] = o.astype(BF16)


def _attention(qkv, lw, layer, *, batch, seq, cache):
    qd, kd, vd, qm, km, vm = qkv
    w4 = N_HEADS * HEAD_W
    tq = TOKEN_TILE
    nq = seq // tq
    lam_init = 0.8 - 0.6 * math.exp(-0.3 * layer)
    q_spec = pl.BlockSpec((tq, w4), lambda b, j: (b * nq + j, 0))
    kv_spec = pl.BlockSpec((seq, w4), lambda b, j: (b, 0))
    args = [qd, qm, kd, vd, km, vm, lw["lq1"], lw["lk1"], lw["lq2"], lw["lk2"], lw["subln_g"]]
    specs = [q_spec, q_spec, kv_spec, kv_spec, kv_spec, kv_spec] + [_const_spec(a.shape) for a in args[6:]]
    scratch = []
    if cache is not None:
        ckd, cvd, cckv, ckr = cache
        n_cache = ckd.shape[2]
        for a in (ckd, cvd, cckv, ckr):
            args.append(a)
            specs.append(pl.BlockSpec((1, 1) + a.shape[2:], lambda b, j: (b, layer, 0, 0)))
        for name in ("w_kc", "w_kr_rows", "w_uv"):
            args.append(lw[name])
            specs.append(_const_spec(lw[name].shape))
        scratch = [pltpu.VMEM((n_cache + seq, w4), BF16)] * 4
    return pl.pallas_call(
        functools.partial(_attention_kernel, with_cache=cache is not None, lam_init=lam_init),
        grid=(batch, nq),
        in_specs=specs,
        out_specs=pl.BlockSpec((tq, 2 * w4), lambda b, j: (b * nq + j, 0)),
        out_shape=jax.ShapeDtypeStruct((batch * seq, 2 * w4), BF16),
        scratch_shapes=scratch,
        compiler_params=_params(("parallel", "arbitrary")),
        name="attention",
    )(*args)


def _post_mixer(x_ref, mixed_ref, mod_ref, wo_ref, g_ref):
    x1 = x_ref[...] + mod_ref[0, 2:3, :] * _dot(mixed_ref[...], wo_ref[...])
    h = _rms(x1, g_ref[...]) * (1.0 + mod_ref[0, 4:5, :]) + mod_ref[0, 3:4, :]
    return x1, h


def _silu(g):
    return g * jax.nn.sigmoid(g)


def _dense_ffn_kernel(x_ref, mixed_ref, mod_ref, wo_ref, g_ref, wg_ref, wu_ref, wd_ref, out_ref):
    x1, h = _post_mixer(x_ref, mixed_ref, mod_ref, wo_ref, g_ref)
    hb = h.astype(BF16)
    acc = jnp.zeros_like(x1)
    for c in range(D_FF // FF_CHUNK):
        sl = slice(c * FF_CHUNK, (c + 1) * FF_CHUNK)
        act = _silu(_dot(hb, wg_ref[:, sl])) * _dot(hb, wu_ref[:, sl])
        acc = acc + _dot(act.astype(BF16), wd_ref[sl, :])
    out_ref[...] = x1 + mod_ref[0, 5:6, :] * acc


def _dense_ffn(x, mixed, mod, lw, *, mod_row):
    n = x.shape[0]
    tm = FFN_TOKEN_TILE
    row = lambda i: (i, 0)
    weights = [lw["w_o"], lw["norm_ffn_g"], lw["w_gate"], lw["w_up"], lw["w_down"]]
    wspecs = [pl.BlockSpec(w.shape, lambda i: (0, 0), pipeline_mode=pl.Buffered(1)) for w in weights]
    return pl.pallas_call(
        _dense_ffn_kernel,
        grid=(n // tm,),
        in_specs=[pl.BlockSpec((tm, D_MODEL), row), pl.BlockSpec((tm, D_MODEL), row),
                  pl.BlockSpec((1, N_MOD, D_MODEL), lambda i: (mod_row(i, tm), 0, 0))] + wspecs,
        out_specs=pl.BlockSpec((tm, D_MODEL), row),
        out_shape=jax.ShapeDtypeStruct((n, D_MODEL), F32),
        compiler_params=_params(("parallel",)),
        name="dense_ffn",
    )(x, mixed, mod, *weights)


def _router_kernel(x_ref, mixed_ref, mod_ref, wo_ref, g_ref, router_ref, x1_ref, h_ref, topi_ref, topg_ref):
    x1, h = _post_mixer(x_ref, mixed_ref, mod_ref, wo_ref, g_ref)
    x1_ref[...] = x1
    h_ref[...] = h
    logits = _dot3(h, router_ref[...])
    ex = jnp.exp(logits - jnp.max(logits, axis=-1, keepdims=True))
    probs = ex / jnp.sum(ex, axis=-1, keepdims=True)
    idx = lax.broadcasted_iota(jnp.int32, probs.shape, 1)
    p1 = jnp.max(probs, axis=-1, keepdims=True)
    i1 = jnp.min(jnp.where(probs == p1, idx, N_EXPERTS), axis=-1, keepdims=True)
    rest = jnp.where(idx == i1, -1.0, probs)
    p2 = jnp.max(rest, axis=-1, keepdims=True)
    i2 = jnp.min(jnp.where(rest == p2, idx, N_EXPERTS), axis=-1, keepdims=True)
    den = p1 + p2
    first = lax.broadcasted_iota(jnp.int32, topi_ref.shape, 1) == 0
    topi_ref[...] = jnp.where(first, i1, i2)
    topg_ref[...] = jnp.where(first, p1 / den, p2 / den)


def _router(x, mixed, mod, lw, *, mod_row):
    n = x.shape[0]
    tm = TOKEN_TILE
    row = lambda i: (i, 0)
    weights = [lw["w_o"], lw["norm_ffn_g"], lw["router"]]
    return pl.pallas_call(
        _router_kernel,
        grid=(n // tm,),
        in_specs=[pl.BlockSpec((tm, D_MODEL), row), pl.BlockSpec((tm, D_MODEL), row),
                  pl.BlockSpec((1, N_MOD, D_MODEL), lambda i: (mod_row(i, tm), 0, 0))]
                 + [_const_spec(w.shape) for w in weights],
        out_specs=[pl.BlockSpec((tm, D_MODEL), row), pl.BlockSpec((tm, D_MODEL), row),
                   pl.BlockSpec((tm, TOP_K), row), pl.BlockSpec((tm, TOP_K), row)],
        out_shape=[jax.ShapeDtypeStruct((n, D_MODEL), F32), jax.ShapeDtypeStruct((n, D_MODEL), F32),
                   jax.ShapeDtypeStruct((n, TOP_K), jnp.int32), jax.ShapeDtypeStruct((n, TOP_K), F32)],
        compiler_params=_params(("parallel",)),
        name="router",
    )(x, mixed, mod, *weights)


def _route_plan(top_i, tm):
    e_flat = top_i.reshape(-1)
    onehot = (e_flat[:, None] == jnp.arange(N_EXPERTS, dtype=jnp.int32)[None, :]).astype(jnp.int32)
    csum = jnp.cumsum(onehot, axis=0)
    rank = jnp.sum(onehot * (csum - 1), axis=1)
    counts = csum[-1]
    gsize = ((counts + tm - 1) // tm) * tm
    gend = jnp.cumsum(gsize)
    dst = ((gend - gsize)[e_flat] + rank).astype(jnp.int32)
    n_tiles = _max_row_tiles(e_flat.shape[0], tm)
    n_used = (gend[-1] // tm).astype(jnp.int32)
    tile_start = jnp.minimum(jnp.arange(n_tiles, dtype=jnp.int32), n_used - 1) * tm
    tile_expert = jnp.sum((gend[None, :] <= tile_start[:, None]).astype(jnp.int32), axis=1)
    return dst, tile_expert.astype(jnp.int32), n_used.reshape(1), gsize.astype(jnp.int32), gend.astype(jnp.int32)


def _max_row_tiles(n_pairs, tm):
    return (n_pairs + N_EXPERTS * (tm - 1)) // tm


def _dispatch_kernel(dst_ref, gsize_ref, gend_ref, nu_ref, h_ref, xs_ref, zero_ref, sem):
    i = pl.program_id(0)
    tm = h_ref.shape[0]

    @pl.when(i == 0)
    def _zero_unused():
        zero_ref[...] = jnp.zeros_like(zero_ref)

        def zero_tile(start):
            cp = pltpu.make_async_copy(zero_ref, xs_ref.at[pl.ds(pl.multiple_of(start, tm), tm)], sem)
            cp.start()
            cp.wait()

        for e in range(N_EXPERTS):
            @pl.when(gsize_ref[e] > 0)
            def _():
                zero_tile(gend_ref[e] - tm)

        def unused(t, carry):
            zero_tile(t * tm)
            return carry

        lax.fori_loop(nu_ref[0], xs_ref.shape[0] // tm, unused, 0)

    base = i * (tm * TOP_K)

    def issue(r, carry):
        for k in range(TOP_K):
            d = dst_ref[base + TOP_K * r + k]
            pltpu.make_async_copy(h_ref.at[pl.ds(r, 1)], xs_ref.at[pl.ds(d, 1)], sem).start()
        return carry

    lax.fori_loop(0, tm, issue, 0, unroll=8)
    for _ in range(TOP_K):
        pltpu.make_async_copy(h_ref, xs_ref.at[pl.ds(0, tm)], sem).wait()


def _dispatch(h, plan, tm):
    dst, _, n_used, gsize, gend = plan
    n = h.shape[0]
    n_rows = _max_row_tiles(dst.shape[0], tm) * tm
    return pl.pallas_call(
        _dispatch_kernel,
        grid_spec=pltpu.PrefetchScalarGridSpec(
            num_scalar_prefetch=4,
            grid=(n // tm,),
            in_specs=[pl.BlockSpec((tm, D_MODEL), lambda i, *_: (i, 0))],
            out_specs=pl.BlockSpec(memory_space=pl.ANY),
            scratch_shapes=[pltpu.VMEM((tm, D_MODEL), F32), pltpu.SemaphoreType.DMA(())],
        ),
        out_shape=jax.ShapeDtypeStruct((n_rows, D_MODEL), F32),
        compiler_params=_params(("arbitrary",)),
        name="moe_dispatch",
    )(dst, gsize, gend, n_used, h)


def _expert_ffn_kernel(te_ref, nu_ref, xs_ref, wg_ref, wu_ref, wd_ref, ys_ref, wg_b, wu_b, wd_b):
    i = pl.program_id(0)

    @pl.when((i == 0) | (te_ref[i] != te_ref[jnp.maximum(i - 1, 0)]))
    def _new_expert():
        wg_b[...] = wg_ref[0].astype(BF16)
        wu_b[...] = wu_ref[0].astype(BF16)
        wd_b[...] = wd_ref[0].astype(BF16)

    @pl.when(i < nu_ref[0])
    def _compute():
        xb = xs_ref[...].astype(BF16)
        y = jnp.zeros(ys_ref.shape, F32)
        for start, size in EXPERT_FF_CHUNKS:
            sl = slice(start, start + size)
            act = _silu(_dot(xb, wg_b[:, sl])) * _dot(xb, wu_b[:, sl])
            y = y + _dot(act.astype(BF16), wd_b[sl, :])
        ys_ref[...] = y

    @pl.when(i >= nu_ref[0])
    def _unused():
        ys_ref[...] = jnp.zeros_like(ys_ref)


def _expert_ffn(xs, plan, lw, tm):
    _, tile_expert, n_used, _, _ = plan
    n_tiles = xs.shape[0] // tm
    wspec = lambda w: pl.BlockSpec((1,) + w.shape[1:], lambda i, te, nu: (te[i], 0, 0))
    return pl.pallas_call(
        _expert_ffn_kernel,
        grid_spec=pltpu.PrefetchScalarGridSpec(
            num_scalar_prefetch=2,
            grid=(n_tiles,),
            in_specs=[pl.BlockSpec((tm, D_MODEL), lambda i, te, nu: (jnp.minimum(i, nu[0] - 1), 0)),
                      wspec(lw["moe_w_gate"]), wspec(lw["moe_w_up"]), wspec(lw["moe_w_down"])],
            out_specs=pl.BlockSpec((tm, D_MODEL), lambda i, te, nu: (i, 0)),
            scratch_shapes=[pltpu.VMEM((D_MODEL, D_FF_EXPERT), BF16), pltpu.VMEM((D_MODEL, D_FF_EXPERT), BF16),
                            pltpu.VMEM((D_FF_EXPERT, D_MODEL), BF16)],
        ),
        out_shape=jax.ShapeDtypeStruct(xs.shape, F32),
        compiler_params=_params(("arbitrary",)),
        name="expert_ffn",
    )(tile_expert, n_used, xs, lw["moe_w_gate"], lw["moe_w_up"], lw["moe_w_down"])


def _combine_kernel(dst_ref, x1_ref, topg_ref, mod_ref, fg_ref, ys_ref, out_ref, buf, sem):
    i = pl.program_id(0)
    n = pl.num_programs(0)
    tm = x1_ref.shape[0]

    def issue(tile, slot):
        base = tile * (tm * TOP_K)

        def body(r, carry):
            for k in range(TOP_K):
                d = dst_ref[base + TOP_K * r + k]
                pltpu.make_async_copy(ys_ref.at[pl.ds(d, 1)], buf.at[slot, k, pl.ds(r, 1)],
                                      sem.at[slot]).start()
            return carry

        lax.fori_loop(0, tm, body, 0, unroll=8)

    @pl.when(i == 0)
    def _first():
        issue(0, 0)

    @pl.when(i + 1 < n)
    def _ahead():
        issue(i + 1, (i + 1) % 2)

    slot = i % 2
    for k in range(TOP_K):
        pltpu.make_async_copy(ys_ref.at[pl.ds(0, tm)], buf.at[slot, k], sem.at[slot]).wait()
    g = topg_ref[...]
    f = g[:, 0:1] * buf[slot, 0] + g[:, 1:2] * buf[slot, 1]
    out_ref[...] = _rms(x1_ref[...] + mod_ref[0, 5:6, :] * f, fg_ref[...])


def _combine(x1, top_g, mod, final_g, ys, plan, *, mod_row):
    dst = plan[0]
    n = x1.shape[0]
    tm = TOKEN_TILE
    row = lambda i, *_: (i, 0)
    return pl.pallas_call(
        _combine_kernel,
        grid_spec=pltpu.PrefetchScalarGridSpec(
            num_scalar_prefetch=1,
            grid=(n // tm,),
            in_specs=[pl.BlockSpec((tm, D_MODEL), row), pl.BlockSpec((tm, TOP_K), row),
                      pl.BlockSpec((1, N_MOD, D_MODEL), lambda i, *_: (mod_row(i, tm), 0, 0)),
                      pl.BlockSpec(final_g.shape, lambda i, *_: (0, 0)),
                      pl.BlockSpec(memory_space=pl.ANY)],
            out_specs=pl.BlockSpec((tm, D_MODEL), row),
            scratch_shapes=[pltpu.VMEM((2, TOP_K, tm, D_MODEL), F32), pltpu.SemaphoreType.DMA((2,))],
        ),
        out_shape=jax.ShapeDtypeStruct((n, D_MODEL), F32),
        compiler_params=_params(("arbitrary",)),
        name="moe_combine",
    )(dst, x1, top_g, mod, final_g, ys)


def _sparse_moe(x, mixed, mod, lw, final_g, *, mod_row):
    x1, h, top_i, top_g = _router(x, mixed, mod, lw, mod_row=mod_row)
    plan = _route_plan(top_i, MOE_ROW_TILE)
    xs = _dispatch(h, plan, MOE_ROW_TILE)
    ys = _expert_ffn(xs, plan, lw, MOE_ROW_TILE)
    return _combine(x1, top_g, mod, final_g, ys, plan, mod_row=mod_row)


def _head_slots(w, used):
    k = w.shape[0]
    w = w.reshape(k, N_HEADS, used)
    return jnp.pad(w, ((0, 0), (0, 0), (0, HEAD_W - used))).reshape(k, N_HEADS * HEAD_W)


def _layer_weights(l, P):
    row = lambda v: v.reshape(1, -1)
    place = jnp.zeros((HEAD_W, N_HEADS, HEAD_W), F32)
    r = jnp.arange(MLA_ROPE_DIM)
    place = place.at[r, :, MLA_NOPE_DIM + r].set(1.0).reshape(HEAD_W, N_HEADS * HEAD_W)
    lw = {
        "norm_mix_g": row(P["norm_mix_g"][l]),
        "norm_ffn_g": row(P["norm_ffn_g"][l]),
        "w_in": jnp.pad(P["w_in"][l], ((0, 0), (0, W_IN_PAD - O_END))).astype(BF16),
        "q_norm_g": row(P["mla_q_norm_g"][l]),
        "kv_norm_g": row(P["mla_kv_norm_g"][l]),
        "w_uq": _head_slots(P["w_uq"][l], MLA_NOPE_DIM + MLA_ROPE_DIM).astype(BF16),
        "w_kc": _head_slots(P["w_uk"][l], MLA_NOPE_DIM).astype(BF16),
        "w_kr": place.astype(BF16),
        "w_kr_rows": place[:MLA_ROPE_DIM].astype(BF16),
        "w_uv": P["w_uv"][l].astype(BF16),
        "lq1": row(P["diff_lq1"][l]), "lk1": row(P["diff_lk1"][l]),
        "lq2": row(P["diff_lq2"][l]), "lk2": row(P["diff_lk2"][l]),
        "subln_g": row(P["diff_subln_g"][l]),
        "w_o": P["w_o"][l].astype(BF16),
    }
    if l % 2 == 0:
        i = l // 2
        lw.update(w_gate=P["w_gate"][i].astype(BF16), w_up=P["w_up"][i].astype(BF16),
                  w_down=P["w_down"][i].astype(BF16))
    else:
        m = l // 2
        lw.update(router=P["router"][m], moe_w_gate=P["moe_w_gate"][m],
                  moe_w_up=P["moe_w_up"][m], moe_w_down=P["moe_w_down"][m])
    return lw


def _run_group(x, mods, weights, final_g, *, batch, seq, mod_row, tables, caches, cache_out):
    assert DEPTH % 2 == 0
    own = []
    for l in range(DEPTH):
        lw = weights[l]
        if not cache_out:
            outs = _pre_mixer(x, mods[l], lw, tables, seq=seq, mod_row=mod_row)
        elif l < DEPTH - 1:
            outs = _pre_mixer(x, mods[l], lw, tables, seq=seq, mod_row=mod_row, cache_out="flat")
            own.append(outs[6:])
        else:
            outs = _pre_mixer(x, mods[l], lw, tables, seq=seq, mod_row=mod_row, cache_out="stacked",
                              prev=own)
            own = outs[6:]
        mixed = _attention(outs[:6], lw, l, batch=batch, seq=seq, cache=caches)
        if l % 2 == 0:
            x = _dense_ffn(x, mixed, mods[l], lw, mod_row=mod_row)
        else:
            assert l == DEPTH - 1
            x = _sparse_moe(x, mixed, mods[l], lw, final_g, mod_row=mod_row)
    return x, own


def kernel(x_prompt, x_sample, cache_diff_k, cache_diff_v, cache_mla_ckv, cache_mla_krope, c, c_ctx, w_ada, b_ada, norm_mix_g, norm_ffn_g, w_in, mla_q_norm_g, mla_kv_norm_g, w_uq, w_uk, w_uv, diff_lq1, diff_lk1, diff_lq2, diff_lk2, diff_subln_g, w_o, w_gate, w_up, w_down, router, moe_w_gate, moe_w_up, moe_w_down, final_norm_g):
    P = dict(norm_mix_g=norm_mix_g, norm_ffn_g=norm_ffn_g, w_in=w_in, mla_q_norm_g=mla_q_norm_g,
             mla_kv_norm_g=mla_kv_norm_g, w_uq=w_uq, w_uk=w_uk, w_uv=w_uv, diff_lq1=diff_lq1,
             diff_lk1=diff_lk1, diff_lq2=diff_lq2, diff_lk2=diff_lk2, diff_subln_g=diff_subln_g,
             w_o=w_o, w_gate=w_gate, w_up=w_up, w_down=w_down, router=router,
             moe_w_gate=moe_w_gate, moe_w_up=moe_w_up, moe_w_down=moe_w_down)
    bp, sp, d = x_prompt.shape
    bs, ss, _ = x_sample.shape
    n_past = cache_diff_k.shape[2]
    w4 = N_HEADS * HEAD_W

    cond = jnp.zeros((COND_ROWS, d), F32).at[0].set(c_ctx).at[1:1 + bs].set(c)
    mod_all = _ada_table(cond, w_ada, b_ada).reshape(DEPTH, COND_ROWS, N_MOD, d)
    mods = [mod_all[l] for l in range(DEPTH)]
    weights = [_layer_weights(l, P) for l in range(DEPTH)]
    final_g = final_norm_g.reshape(1, d)

    yp, own = _run_group(
        x_prompt.reshape(bp * sp, d), mods, weights, final_g, batch=bp, seq=sp,
        mod_row=lambda i, tm: 0, tables=None, caches=None, cache_out=True)
    y_prompt = yp.reshape(bp, sp, d)
    new_diff_k, new_diff_v, new_mla_ckv, new_mla_krope = own

    caches = (cache_diff_k.reshape(bs, DEPTH, n_past, w4), cache_diff_v.reshape(bs, DEPTH, n_past, w4),
              cache_mla_ckv, cache_mla_krope)
    ys, _ = _run_group(
        x_sample.reshape(bs * ss, d), mods, weights, final_g, batch=bs, seq=ss,
        mod_row=lambda i, tm: 1 + (i * tm) // ss,
        tables=_rope_tables(ss), caches=caches, cache_out=False)
    y_sample = ys.reshape(bs, ss, d)

    return (y_prompt, y_sample, new_diff_k, new_diff_v, new_mla_ckv, new_mla_krope)
```

```python
import functools
import math

import jax
import jax.numpy as jnp
from jax import lax
from jax.experimental import pallas as pl
from jax.experimental.pallas import tpu as pltpu
from jax.experimental.pallas import tpu_sc as plsc

F32 = jnp.float32
BF16 = jnp.bfloat16

D_MODEL = 1024
DEPTH = 2
GRID_W = 64
N_HEADS = 4
HEAD_W = 128
DIFF_HEAD_DIM = 64
MLA_NOPE_DIM = 64
MLA_ROPE_DIM = 32
Q_LORA = 256
KV_LORA = 128
D_FF = 2816
FF_CHUNK = 1408
N_EXPERTS = 8
D_FF_EXPERT = 1408
EXPERT_FF_CHUNKS = ((0, 512), (512, 512), (1024, 384))
ROPE_THETA = 10000.0
RMS_EPS = 1e-6
LOG2_E = math.log2(math.e)
N_MOD = 6
COND_ROWS = 16

TOKEN_TILE = 256
FFN_TOKEN_TILE = 512
MOE_ROW_TILE = 512
TOP_K = 2
SC_CORES = 2
SC_WORKERS = SC_CORES * 16
SC_STAGING_BYTES = 448 * 1024
SC_MAX_INDEX_ROWS = 128
VMEM_LIMIT_BYTES = 60 * 1024 * 1024

O_QD, O_KD, O_VD, O_CQ, O_CKV, O_KR, O_END = 0, 512, 1024, 1536, 1792, 1920, 1952
W_IN_PAD = 2048


def _params(sem):
    return pltpu.CompilerParams(dimension_semantics=sem, vmem_limit_bytes=VMEM_LIMIT_BYTES)


def _const_spec(shape):
    nd = len(shape)
    return pl.BlockSpec(shape, lambda *_: (0,) * nd)


def _rms(x, g):
    return x * lax.rsqrt(jnp.mean(x * x, axis=-1, keepdims=True) + RMS_EPS) * g


def _split_bf16(x):
    hi = x.astype(BF16)
    lo = (x - hi.astype(F32)).astype(BF16)
    return hi, lo


def _dot(a, b):
    return jnp.dot(a, b, preferred_element_type=F32)


def _dot3(a, b):
    a_hi, a_lo = _split_bf16(a)
    b_hi, b_lo = _split_bf16(b)
    return _dot(a_hi, b_hi) + _dot(a_hi, b_lo) + _dot(a_lo, b_hi)


def _ada_kernel(cond_ref, w_ref, b_ref, out_ref):
    cond = cond_ref[...]
    s = cond * jax.nn.sigmoid(cond)
    out_ref[0] = _dot3(s, w_ref[0]) + b_ref[0]


def _ada_table(cond, w_ada, b_ada):
    d = D_MODEL
    return pl.pallas_call(
        _ada_kernel,
        grid=(DEPTH, N_MOD),
        in_specs=[
            pl.BlockSpec((COND_ROWS, d), lambda l, j: (0, 0)),
            pl.BlockSpec((1, d, d), lambda l, j: (l, 0, j)),
            pl.BlockSpec((1, 1, d), lambda l, j: (l, 0, j)),
        ],
        out_specs=pl.BlockSpec((1, COND_ROWS, d), lambda l, j: (l, 0, j)),
        out_shape=jax.ShapeDtypeStruct((DEPTH, COND_ROWS, N_MOD * d), F32),
        compiler_params=_params(("arbitrary", "arbitrary")),
        name="ada_table",
    )(cond, w_ada, b_ada.reshape(DEPTH, 1, N_MOD * d))


def _axial_tables(seq, dim):
    half = dim // 4
    freqs = ROPE_THETA ** (-jnp.arange(half, dtype=F32) / half)
    pos = jnp.arange(seq, dtype=jnp.int32)
    rows = (pos // GRID_W).astype(F32)[:, None] * freqs[None, :]
    cols = (pos % GRID_W).astype(F32)[:, None] * freqs[None, :]
    cos = jnp.concatenate([jnp.cos(rows), jnp.cos(rows), jnp.cos(cols), jnp.cos(cols)], axis=-1)
    sin = jnp.concatenate([-jnp.sin(rows), jnp.sin(rows), -jnp.sin(cols), jnp.sin(cols)], axis=-1)
    return cos, sin


def _rope_tables(seq):
    cos64, sin64 = _axial_tables(seq, DIFF_HEAD_DIM)
    cos32, sin32 = _axial_tables(seq, MLA_ROPE_DIM)
    ones = lambda n: jnp.ones((seq, n), F32)
    zeros = lambda n: jnp.zeros((seq, n), F32)
    cos_d = jnp.tile(cos64, (1, 2 * N_HEADS))
    sin_d = jnp.tile(sin64, (1, 2 * N_HEADS))
    pad = HEAD_W - MLA_NOPE_DIM - MLA_ROPE_DIM
    cos_m = jnp.tile(jnp.concatenate([ones(MLA_NOPE_DIM), cos32, ones(pad)], axis=-1), (1, N_HEADS))
    sin_m = jnp.tile(jnp.concatenate([zeros(MLA_NOPE_DIM), sin32, zeros(pad)], axis=-1), (1, N_HEADS))
    cos_r = jnp.concatenate([cos32, ones(HEAD_W - MLA_ROPE_DIM)], axis=-1)
    sin_r = jnp.concatenate([sin32, zeros(HEAD_W - MLA_ROPE_DIM)], axis=-1)
    return cos_d, sin_d, cos_m, sin_m, cos_r, sin_r


def _rope(x, cos, sin, block):
    width = x.shape[-1]
    lane = lax.broadcasted_iota(jnp.int32, x.shape, 1)
    first = (lane % (2 * block)) < block
    partner = jnp.where(first, pltpu.roll(x, width - block, 1), pltpu.roll(x, block, 1))
    return x * cos + partner * sin


def _store_heads(ref, layer, x):
    for hd in range(N_HEADS):
        ref[0, layer, :, hd, :] = x[:, hd * HEAD_W:(hd + 1) * HEAD_W]


def _pre_mixer_kernel(*refs, rope, cache_out, n_prev):
    it = iter(refs)
    x_ref, mod_ref, g_ref, win_ref, qg_ref, kvg_ref, wuq_ref, wkc_ref, wkr_ref, wuv_ref = (
        next(it) for _ in range(10))
    if rope:
        cos_d, sin_d, cos_m, sin_m, cos_r, sin_r = (next(it) for _ in range(6))
    prev = [[next(it) for _ in range(4)] for _ in range(n_prev)]
    qd_ref, kd_ref, vd_ref, qm_ref, km_ref, vm_ref = (next(it) for _ in range(6))
    if cache_out:
        kd32_ref, vd32_ref, ckv32_ref, kr32_ref = (next(it) for _ in range(4))

    x = x_ref[...]
    h = _rms(x, g_ref[...]) * (1.0 + mod_ref[0, 1:2, :]) + mod_ref[0, 0:1, :]
    z = _dot(h.astype(BF16), win_ref[...])

    qd = z[:, O_QD:O_KD]
    kd = z[:, O_KD:O_VD]
    vd = z[:, O_VD:O_CQ]
    cq = z[:, O_CQ:O_CKV]
    ckv = z[:, O_CKV:O_KR]
    kr = z[:, O_KR:W_IN_PAD]

    qm = _dot(_rms(cq, qg_ref[...]).astype(BF16), wuq_ref[...])
    ckv = _rms(ckv, kvg_ref[...])
    if rope:
        qd = _rope(qd, cos_d[...], sin_d[...], DIFF_HEAD_DIM // 4)
        kd = _rope(kd, cos_d[...], sin_d[...], DIFF_HEAD_DIM // 4)
        qm = _rope(qm, cos_m[...], sin_m[...], MLA_ROPE_DIM // 4)
        kr = _rope(kr, cos_r[...], sin_r[...], MLA_ROPE_DIM // 4)

    ckv_b = ckv.astype(BF16)
    qd_ref[...] = (qd * (LOG2_E * DIFF_HEAD_DIM ** -0.5)).astype(BF16)
    kd_ref[...] = kd.astype(BF16)
    vd_ref[...] = vd.astype(BF16)
    qm_ref[...] = (qm * (LOG2_E * (MLA_NOPE_DIM + MLA_ROPE_DIM) ** -0.5)).astype(BF16)
    km_ref[...] = (_dot(ckv_b, wkc_ref[...]) + _dot(kr.astype(BF16), wkr_ref[...])).astype(BF16)
    vm_ref[...] = _dot(ckv_b, wuv_ref[...]).astype(BF16)
    if cache_out == "flat":
        kd32_ref[...] = kd
        vd32_ref[...] = vd
        ckv32_ref[...] = ckv
        kr32_ref[...] = kr[:, :MLA_ROPE_DIM]
    elif cache_out == "stacked":
        for l, (pk, pv, pc, pr) in enumerate(prev):
            _store_heads(kd32_ref, l, pk[...])
            _store_heads(vd32_ref, l, pv[...])
            ckv32_ref[0, l] = pc[...]
            kr32_ref[0, l] = pr[...]
        _store_heads(kd32_ref, n_prev, kd)
        _store_heads(vd32_ref, n_prev, vd)
        ckv32_ref[0, n_prev] = ckv
        kr32_ref[0, n_prev] = kr[:, :MLA_ROPE_DIM]


def _pre_mixer(x, mod, lw, tables, *, seq, mod_row, cache_out=None, prev=()):
    n = x.shape[0]
    tm = TOKEN_TILE
    tiles_per_seq = seq // tm
    rope = tables is not None
    row = lambda i: (i, 0)
    args = [x, mod, lw["norm_mix_g"], lw["w_in"], lw["q_norm_g"], lw["kv_norm_g"],
            lw["w_uq"], lw["w_kc"], lw["w_kr"], lw["w_uv"]]
    specs = [pl.BlockSpec((tm, D_MODEL), row),
             pl.BlockSpec((1, N_MOD, D_MODEL), lambda i: (mod_row(i, tm), 0, 0))]
    specs += [_const_spec(a.shape) for a in args[2:]]
    if rope:
        for t in tables:
            args.append(t)
            specs.append(pl.BlockSpec((tm, t.shape[1]), lambda i: (i % tiles_per_seq, 0)))
    for layer_rows in prev:
        for a in layer_rows:
            args.append(a)
            specs.append(pl.BlockSpec((tm, a.shape[1]), row))
    w4 = N_HEADS * HEAD_W
    out_shape = [jax.ShapeDtypeStruct((n, w4), BF16)] * 6
    out_specs = [pl.BlockSpec((tm, w4), row)] * 6
    if cache_out == "flat":
        for w in (w4, w4, KV_LORA, MLA_ROPE_DIM):
            out_shape.append(jax.ShapeDtypeStruct((n, w), F32))
            out_specs.append(pl.BlockSpec((tm, w), row))
    elif cache_out == "stacked":
        n_layers = len(prev) + 1
        for tail in ((N_HEADS, HEAD_W), (N_HEADS, HEAD_W), (KV_LORA,), (MLA_ROPE_DIM,)):
            zeros = (0,) * len(tail)
            out_shape.append(jax.ShapeDtypeStruct((n // seq, n_layers, seq) + tail, F32))
            out_specs.append(pl.BlockSpec(
                (1, n_layers, tm) + tail,
                lambda i, zeros=zeros: (i // tiles_per_seq, 0, i % tiles_per_seq) + zeros))
    return pl.pallas_call(
        functools.partial(_pre_mixer_kernel, rope=rope, cache_out=cache_out, n_prev=len(prev)),
        grid=(n // tm,),
        in_specs=specs,
        out_specs=out_specs,
        out_shape=out_shape,
        compiler_params=_params(("parallel",)),
        name="pre_mixer",
    )(*args)


def _softmax_terms(s):
    m = jnp.max(s, axis=-1, keepdims=True)
    e = jnp.exp2(s - m)
    return e, 1.0 / jnp.sum(e, axis=-1, keepdims=True)


def _nt_dot(a, b):
    return lax.dot_general(a, b, (((1,), (1,)), ((), ())), preferred_element_type=F32)


def _attention_kernel(*refs, with_cache, lam_init):
    it = iter(refs)
    qd_ref, qm_ref, kd_ref, vd_ref, km_ref, vm_ref = (next(it) for _ in range(6))
    lq1, lk1, lq2, lk2, subg_ref = (next(it) for _ in range(5))
    if with_cache:
        ckd_ref, cvd_ref, cckv_ref, ckr_ref, wkc_ref, wkr_ref, wuv_ref = (next(it) for _ in range(7))
    out_ref = next(it)
    if with_cache:
        kd_all, vd_all, km_all, vm_all = (next(it) for _ in range(4))
        n_cache = cckv_ref.shape[2]

        @pl.when(pl.program_id(1) == 0)
        def _fill():
            kd_all[:n_cache, :] = ckd_ref[0, 0].astype(BF16)
            vd_all[:n_cache, :] = cvd_ref[0, 0].astype(BF16)
            cckv = cckv_ref[0, 0].astype(BF16)
            km_all[:n_cache, :] = (_dot(cckv, wkc_ref[...])
                                   + _dot(ckr_ref[0, 0].astype(BF16), wkr_ref[...])).astype(BF16)
            vm_all[:n_cache, :] = _dot(cckv, wuv_ref[...]).astype(BF16)
            kd_all[n_cache:, :] = kd_ref[...]
            vd_all[n_cache:, :] = vd_ref[...]
            km_all[n_cache:, :] = km_ref[...]
            vm_all[n_cache:, :] = vm_ref[...]
    else:
        kd_all, vd_all, km_all, vm_all = kd_ref, vd_ref, km_ref, vm_ref

    lam = (jnp.exp(jnp.sum(lq1[...] * lk1[...], axis=-1, keepdims=True))
           - jnp.exp(jnp.sum(lq2[...] * lk2[...], axis=-1, keepdims=True)) + lam_init)
    subg = subg_ref[...]
    tq = qd_ref.shape[0]
    lane = lax.broadcasted_iota(jnp.int32, (tq, HEAD_W), 1)
    first = lane < DIFF_HEAD_DIM

    for hd in range(N_HEADS):
        sl = slice(hd * HEAD_W, (hd + 1) * HEAD_W)
        q = qd_ref[:, sl]
        k = kd_all[:, sl]
        zero = jnp.zeros_like(q)
        e1, r1 = _softmax_terms(_nt_dot(jnp.where(first, q, zero), k))
        e2, r2 = _softmax_terms(_nt_dot(jnp.where(first, zero, q), k))
        p = (e1 * r1 - e2 * (lam * r2)).astype(BF16)
        o = _dot(p, vd_all[:, sl])
        out_ref[:, sl] = (_rms(o, subg) * (1.0 - lam_init)).astype(BF16)

    for hd in range(N_HEADS):
        sl = slice(hd * HEAD_W, (hd + 1) * HEAD_W)
        e, r = _softmax_terms(_nt_dot(qm_ref[:, sl], km_all[:, sl]))
        o = _dot(e.astype(BF16), vm_all[:, sl]) * r
        out_ref[:, N_HEADS * HEAD_W + hd * HEAD_W:N_HEADS * HEAD_W + (hd + 1) * HEAD_W] = o.astype(BF16)


def _attention(qkv, lw, layer, *, batch, seq, cache):
    qd, kd, vd, qm, km, vm = qkv
    w4 = N_HEADS * HEAD_W
    tq = TOKEN_TILE
    nq = seq // tq
    lam_init = 0.8 - 0.6 * math.exp(-0.3 * layer)
    q_spec = pl.BlockSpec((tq, w4), lambda b, j: (b * nq + j, 0))
    kv_spec = pl.BlockSpec((seq, w4), lambda b, j: (b, 0))
    args = [qd, qm, kd, vd, km, vm, lw["lq1"], lw["lk1"], lw["lq2"], lw["lk2"], lw["subln_g"]]
    specs = [q_spec, q_spec, kv_spec, kv_spec, kv_spec, kv_spec] + [_const_spec(a.shape) for a in args[6:]]
    scratch = []
    if cache is not None:
        ckd, cvd, cckv, ckr = cache
        n_cache = ckd.shape[2]
        for a in (ckd, cvd, cckv, ckr):
            args.append(a)
            specs.append(pl.BlockSpec((1, 1) + a.shape[2:], lambda b, j: (b, layer, 0, 0)))
        for name in ("w_kc", "w_kr_rows", "w_uv"):
            args.append(lw[name])
            specs.append(_const_spec(lw[name].shape))
        scratch = [pltpu.VMEM((n_cache + seq, w4), BF16)] * 4
    return pl.pallas_call(
        functools.partial(_attention_kernel, with_cache=cache is not None, lam_init=lam_init),
        grid=(batch, nq),
        in_specs=specs,
        out_specs=pl.BlockSpec((tq, 2 * w4), lambda b, j: (b * nq + j, 0)),
        out_shape=jax.ShapeDtypeStruct((batch * seq, 2 * w4), BF16),
        scratch_shapes=scratch,
        compiler_params=_params(("parallel", "arbitrary")),
        name="attention",
    )(*args)


def _post_mixer(x_ref, mixed_ref, mod_ref, wo_ref, g_ref):
    x1 = x_ref[...] + mod_ref[0, 2:3, :] * _dot(mixed_ref[...], wo_ref[...])
    h = _rms(x1, g_ref[...]) * (1.0 + mod_ref[0, 4:5, :]) + mod_ref[0, 3:4, :]
    return x1, h


def _silu(g):
    return g * jax.nn.sigmoid(g)


def _dense_ffn_kernel(x_ref, mixed_ref, mod_ref, wo_ref, g_ref, wg_ref, wu_ref, wd_ref, out_ref):
    x1, h = _post_mixer(x_ref, mixed_ref, mod_ref, wo_ref, g_ref)
    hb = h.astype(BF16)
    acc = jnp.zeros_like(x1)
    for c in range(D_FF // FF_CHUNK):
        sl = slice(c * FF_CHUNK, (c + 1) * FF_CHUNK)
        act = _silu(_dot(hb, wg_ref[:, sl])) * _dot(hb, wu_ref[:, sl])
        acc = acc + _dot(act.astype(BF16), wd_ref[sl, :])
    out_ref[...] = x1 + mod_ref[0, 5:6, :] * acc


def _dense_ffn(x, mixed, mod, lw, *, mod_row):
    n = x.shape[0]
    tm = FFN_TOKEN_TILE
    row = lambda i: (i, 0)
    weights = [lw["w_o"], lw["norm_ffn_g"], lw["w_gate"], lw["w_up"], lw["w_down"]]
    wspecs = [pl.BlockSpec(w.shape, lambda i: (0, 0), pipeline_mode=pl.Buffered(1)) for w in weights]
    return pl.pallas_call(
        _dense_ffn_kernel,
        grid=(n // tm,),
        in_specs=[pl.BlockSpec((tm, D_MODEL), row), pl.BlockSpec((tm, D_MODEL), row),
                  pl.BlockSpec((1, N_MOD, D_MODEL), lambda i: (mod_row(i, tm), 0, 0))] + wspecs,
        out_specs=pl.BlockSpec((tm, D_MODEL), row),
        out_shape=jax.ShapeDtypeStruct((n, D_MODEL), F32),
        compiler_params=_params(("parallel",)),
        name="dense_ffn",
    )(x, mixed, mod, *weights)


def _router_kernel(x_ref, mixed_ref, mod_ref, wo_ref, g_ref, router_ref, x1_ref, h_ref, topi_ref, topg_ref):
    x1, h = _post_mixer(x_ref, mixed_ref, mod_ref, wo_ref, g_ref)
    x1_ref[...] = x1
    h_ref[...] = h
    logits = _dot3(h, router_ref[...])
    ex = jnp.exp(logits - jnp.max(logits, axis=-1, keepdims=True))
    probs = ex / jnp.sum(ex, axis=-1, keepdims=True)
    idx = lax.broadcasted_iota(jnp.int32, probs.shape, 1)
    p1 = jnp.max(probs, axis=-1, keepdims=True)
    i1 = jnp.min(jnp.where(probs == p1, idx, N_EXPERTS), axis=-1, keepdims=True)
    rest = jnp.where(idx == i1, -1.0, probs)
    p2 = jnp.max(rest, axis=-1, keepdims=True)
    i2 = jnp.min(jnp.where(rest == p2, idx, N_EXPERTS), axis=-1, keepdims=True)
    den = p1 + p2
    first = lax.broadcasted_iota(jnp.int32, topi_ref.shape, 1) == 0
    topi_ref[...] = jnp.where(first, i1, i2)
    topg_ref[...] = jnp.where(first, p1 / den, p2 / den)


def _router(x, mixed, mod, lw, *, mod_row):
    n = x.shape[0]
    tm = TOKEN_TILE
    row = lambda i: (i, 0)
    weights = [lw["w_o"], lw["norm_ffn_g"], lw["router"]]
    return pl.pallas_call(
        _router_kernel,
        grid=(n // tm,),
        in_specs=[pl.BlockSpec((tm, D_MODEL), row), pl.BlockSpec((tm, D_MODEL), row),
                  pl.BlockSpec((1, N_MOD, D_MODEL), lambda i: (mod_row(i, tm), 0, 0))]
                 + [_const_spec(w.shape) for w in weights],
        out_specs=[pl.BlockSpec((tm, D_MODEL), row), pl.BlockSpec((tm, D_MODEL), row),
                   pl.BlockSpec((tm, TOP_K), row), pl.BlockSpec((tm, TOP_K), row)],
        out_shape=[jax.ShapeDtypeStruct((n, D_MODEL), F32), jax.ShapeDtypeStruct((n, D_MODEL), F32),
                   jax.ShapeDtypeStruct((n, TOP_K), jnp.int32), jax.ShapeDtypeStruct((n, TOP_K), F32)],
        compiler_params=_params(("parallel",)),
        name="router",
    )(x, mixed, mod, *weights)


def _route_plan(top_i, tm):
    e_flat = top_i.reshape(-1)
    onehot = (e_flat[:, None] == jnp.arange(N_EXPERTS, dtype=jnp.int32)[None, :]).astype(jnp.int32)
    csum = jnp.cumsum(onehot, axis=0)
    rank = jnp.sum(onehot * (csum - 1), axis=1)
    counts = csum[-1]
    gsize = ((counts + tm - 1) // tm) * tm
    gend = jnp.cumsum(gsize)
    dst = ((gend - gsize)[e_flat] + rank).astype(jnp.int32)
    n_tiles = _max_row_tiles(e_flat.shape[0], tm)
    n_used = (gend[-1] // tm).astype(jnp.int32)
    tile_start = jnp.minimum(jnp.arange(n_tiles, dtype=jnp.int32), n_used - 1) * tm
    tile_expert = jnp.sum((gend[None, :] <= tile_start[:, None]).astype(jnp.int32), axis=1)
    return dst, tile_expert.astype(jnp.int32), n_used.reshape(1), gsize.astype(jnp.int32), gend.astype(jnp.int32)


def _max_row_tiles(n_pairs, tm):
    return (n_pairs + N_EXPERTS * (tm - 1)) // tm


def _dispatch_kernel(dst_ref, gsize_ref, gend_ref, nu_ref, h_ref, xs_ref, zero_ref, sem):
    i = pl.program_id(0)
    tm = h_ref.shape[0]

    @pl.when(i == 0)
    def _zero_unused():
        zero_ref[...] = jnp.zeros_like(zero_ref)

        def zero_tile(start):
            cp = pltpu.make_async_copy(zero_ref, xs_ref.at[pl.ds(pl.multiple_of(start, tm), tm)], sem)
            cp.start()
            cp.wait()

        for e in range(N_EXPERTS):
            @pl.when(gsize_ref[e] > 0)
            def _():
                zero_tile(gend_ref[e] - tm)

        def unused(t, carry):
            zero_tile(t * tm)
            return carry

        lax.fori_loop(nu_ref[0], xs_ref.shape[0] // tm, unused, 0)

    base = i * (tm * TOP_K)

    def issue(r, carry):
        for k in range(TOP_K):
            d = dst_ref[base + TOP_K * r + k]
            pltpu.make_async_copy(h_ref.at[pl.ds(r, 1)], xs_ref.at[pl.ds(d, 1)], sem).start()
        return carry

    lax.fori_loop(0, tm, issue, 0, unroll=8)
    for _ in range(TOP_K):
        pltpu.make_async_copy(h_ref, xs_ref.at[pl.ds(0, tm)], sem).wait()


def _dispatch(h, plan, tm):
    dst, _, n_used, gsize, gend = plan
    n = h.shape[0]
    n_rows = _max_row_tiles(dst.shape[0], tm) * tm
    return pl.pallas_call(
        _dispatch_kernel,
        grid_spec=pltpu.PrefetchScalarGridSpec(
            num_scalar_prefetch=4,
            grid=(n // tm,),
            in_specs=[pl.BlockSpec((tm, D_MODEL), lambda i, *_: (i, 0))],
            out_specs=pl.BlockSpec(memory_space=pl.ANY),
            scratch_shapes=[pltpu.VMEM((tm, D_MODEL), F32), pltpu.SemaphoreType.DMA(())],
        ),
        out_shape=jax.ShapeDtypeStruct((n_rows, D_MODEL), F32),
        compiler_params=_params(("arbitrary",)),
        name="moe_dispatch",
    )(dst, gsize, gend, n_used, h)


def _expert_ffn_kernel(te_ref, nu_ref, xs_ref, wg_ref, wu_ref, wd_ref, ys_ref, wg_b, wu_b, wd_b):
    i = pl.program_id(0)

    @pl.when((i == 0) | (te_ref[i] != te_ref[jnp.maximum(i - 1, 0)]))
    def _new_expert():
        wg_b[...] = wg_ref[0].astype(BF16)
        wu_b[...] = wu_ref[0].astype(BF16)
        wd_b[...] = wd_ref[0].astype(BF16)

    @pl.when(i < nu_ref[0])
    def _compute():
        xb = xs_ref[...].astype(BF16)
        y = jnp.zeros(ys_ref.shape, F32)
        for start, size in EXPERT_FF_CHUNKS:
            sl = slice(start, start + size)
            act = _silu(_dot(xb, wg_b[:, sl])) * _dot(xb, wu_b[:, sl])
            y = y + _dot(act.astype(BF16), wd_b[sl, :])
        ys_ref[...] = y

    @pl.when(i >= nu_ref[0])
    def _unused():
        ys_ref[...] = jnp.zeros_like(ys_ref)


def _expert_ffn(xs, plan, lw, tm):
    _, tile_expert, n_used, _, _ = plan
    n_tiles = xs.shape[0] // tm
    wspec = lambda w: pl.BlockSpec((1,) + w.shape[1:], lambda i, te, nu: (te[i], 0, 0))
    return pl.pallas_call(
        _expert_ffn_kernel,
        grid_spec=pltpu.PrefetchScalarGridSpec(
            num_scalar_prefetch=2,
            grid=(n_tiles,),
            in_specs=[pl.BlockSpec((tm, D_MODEL), lambda i, te, nu: (jnp.minimum(i, nu[0] - 1), 0)),
                      wspec(lw["moe_w_gate"]), wspec(lw["moe_w_up"]), wspec(lw["moe_w_down"])],
            out_specs=pl.BlockSpec((tm, D_MODEL), lambda i, te, nu: (i, 0)),
            scratch_shapes=[pltpu.VMEM((D_MODEL, D_FF_EXPERT), BF16), pltpu.VMEM((D_MODEL, D_FF_EXPERT), BF16),
                            pltpu.VMEM((D_FF_EXPERT, D_MODEL), BF16)],
        ),
        out_shape=jax.ShapeDtypeStruct(xs.shape, F32),
        compiler_params=_params(("arbitrary",)),
        name="expert_ffn",
    )(tile_expert, n_used, xs, lw["moe_w_gate"], lw["moe_w_up"], lw["moe_w_down"])


def _combine_kernel(dst_ref, x1_ref, topg_ref, mod_ref, fg_ref, ys_ref, out_ref, buf, sem):
    i = pl.program_id(0)
    n = pl.num_programs(0)
    tm = x1_ref.shape[0]

    def issue(tile, slot):
        base = tile * (tm * TOP_K)

        def body(r, carry):
            for k in range(TOP_K):
                d = dst_ref[base + TOP_K * r + k]
                pltpu.make_async_copy(ys_ref.at[pl.ds(d, 1)], buf.at[slot, k, pl.ds(r, 1)],
                                      sem.at[slot]).start()
            return carry

        lax.fori_loop(0, tm, body, 0, unroll=8)

    @pl.when(i == 0)
    def _first():
        issue(0, 0)

    @pl.when(i + 1 < n)
    def _ahead():
        issue(i + 1, (i + 1) % 2)

    slot = i % 2
    for k in range(TOP_K):
        pltpu.make_async_copy(ys_ref.at[pl.ds(0, tm)], buf.at[slot, k], sem.at[slot]).wait()
    g = topg_ref[...]
    f = g[:, 0:1] * buf[slot, 0] + g[:, 1:2] * buf[slot, 1]
    out_ref[...] = _rms(x1_ref[...] + mod_ref[0, 5:6, :] * f, fg_ref[...])


def _combine(x1, top_g, mod, final_g, ys, plan, *, mod_row):
    dst = plan[0]
    n = x1.shape[0]
    tm = TOKEN_TILE
    row = lambda i, *_: (i, 0)
    return pl.pallas_call(
        _combine_kernel,
        grid_spec=pltpu.PrefetchScalarGridSpec(
            num_scalar_prefetch=1,
            grid=(n // tm,),
            in_specs=[pl.BlockSpec((tm, D_MODEL), row), pl.BlockSpec((tm, TOP_K), row),
                      pl.BlockSpec((1, N_MOD, D_MODEL), lambda i, *_: (mod_row(i, tm), 0, 0)),
                      pl.BlockSpec(final_g.shape, lambda i, *_: (0, 0)),
                      pl.BlockSpec(memory_space=pl.ANY)],
            out_specs=pl.BlockSpec((tm, D_MODEL), row),
            scratch_shapes=[pltpu.VMEM((2, TOP_K, tm, D_MODEL), F32), pltpu.SemaphoreType.DMA((2,))],
        ),
        out_shape=jax.ShapeDtypeStruct((n, D_MODEL), F32),
        compiler_params=_params(("arbitrary",)),
        name="moe_combine",
    )(dst, x1, top_g, mod, final_g, ys)


def _sc_chunk_rows(per_worker, row_bytes):
    limit = min(SC_MAX_INDEX_ROWS, SC_STAGING_BYTES // row_bytes)
    return max(c for c in range(8, limit + 1, 8) if per_worker % c == 0)


def _sc_gather_rows(table, idx):
    n_out = idx.shape[0]
    d = table.shape[1]
    per_worker = n_out // SC_WORKERS
    assert n_out % (8 * SC_WORKERS) == 0
    chunk = _sc_chunk_rows(per_worker, d * table.dtype.itemsize)
    n_chunks = per_worker // chunk

    def body(table_hbm, idx_hbm, out_hbm, idx_v, rows_v, sem):
        worker = lax.axis_index("s") * SC_CORES + lax.axis_index("c")
        base = worker * per_worker

        @pl.loop(0, n_chunks)
        def _(j):
            off = pl.multiple_of(base + j * chunk, 8)
            pltpu.sync_copy(idx_hbm.at[pl.ds(off, chunk)], idx_v)
            pltpu.async_copy(table_hbm.at[idx_v], rows_v, sem).wait()
            pltpu.sync_copy(rows_v, out_hbm.at[pl.ds(off, chunk)])

    return pl.kernel(
        body,
        out_type=jax.ShapeDtypeStruct((n_out, d), table.dtype),
        mesh=plsc.VectorSubcoreMesh(core_axis_name="c", subcore_axis_name="s"),
        scratch_types=[pltpu.VMEM((chunk,), jnp.int32), pltpu.VMEM((chunk, d), table.dtype),
                       pltpu.SemaphoreType.DMA],
        name="sc_gather_rows",
    )(table, idx)


def _pair_sum_kernel(x1_ref, topg_ref, mod_ref, fg_ref, y0_ref, y1_ref, out_ref):
    g = topg_ref[...]
    f = g[:, 0:1] * y0_ref[...] + g[:, 1:2] * y1_ref[...]
    out_ref[...] = _rms(x1_ref[...] + mod_ref[0, 5:6, :] * f, fg_ref[...])


def _pair_sum(x1, top_g, mod, final_g, ypair, *, mod_row):
    n = x1.shape[0]
    tm = TOKEN_TILE
    row = lambda i: (i, 0)
    return pl.pallas_call(
        _pair_sum_kernel,
        grid=(n // tm,),
        in_specs=[pl.BlockSpec((tm, D_MODEL), row), pl.BlockSpec((tm, TOP_K), row),
                  pl.BlockSpec((1, N_MOD, D_MODEL), lambda i: (mod_row(i, tm), 0, 0)),
                  _const_spec(final_g.shape),
                  pl.BlockSpec((tm, D_MODEL), row),
                  pl.BlockSpec((tm, D_MODEL), lambda i: (i + n // tm, 0))],
        out_specs=pl.BlockSpec((tm, D_MODEL), row),
        out_shape=jax.ShapeDtypeStruct((n, D_MODEL), F32),
        compiler_params=_params(("parallel",)),
        name="moe_pair_sum",
    )(x1, top_g, mod, final_g, ypair, ypair)


def _sparse_moe(x, mixed, mod, lw, final_g, *, mod_row):
    x1, h, top_i, top_g = _router(x, mixed, mod, lw, mod_row=mod_row)
    n = x.shape[0]
    plan = _route_plan(top_i, MOE_ROW_TILE)
    dst = plan[0]
    n_rows = _max_row_tiles(dst.shape[0], MOE_ROW_TILE) * MOE_ROW_TILE
    src = jnp.zeros((n_rows,), jnp.int32).at[dst].set(jnp.arange(dst.shape[0], dtype=jnp.int32) // TOP_K)
    xs = _sc_gather_rows(h, src)
    ys = _expert_ffn(xs, plan, lw, MOE_ROW_TILE)
    ypair = _sc_gather_rows(ys, dst.reshape(n, TOP_K).T.reshape(-1))
    return _pair_sum(x1, top_g, mod, final_g, ypair, mod_row=mod_row)


def _head_slots(w, used):
    k = w.shape[0]
    w = w.reshape(k, N_HEADS, used)
    return jnp.pad(w, ((0, 0), (0, 0), (0, HEAD_W - used))).reshape(k, N_HEADS * HEAD_W)


def _layer_weights(l, P):
    row = lambda v: v.reshape(1, -1)
    place = jnp.zeros((HEAD_W, N_HEADS, HEAD_W), F32)
    r = jnp.arange(MLA_ROPE_DIM)
    place = place.at[r, :, MLA_NOPE_DIM + r].set(1.0).reshape(HEAD_W, N_HEADS * HEAD_W)
    lw = {
        "norm_mix_g": row(P["norm_mix_g"][l]),
        "norm_ffn_g": row(P["norm_ffn_g"][l]),
        "w_in": jnp.pad(P["w_in"][l], ((0, 0), (0, W_IN_PAD - O_END))).astype(BF16),
        "q_norm_g": row(P["mla_q_norm_g"][l]),
        "kv_norm_g": row(P["mla_kv_norm_g"][l]),
        "w_uq": _head_slots(P["w_uq"][l], MLA_NOPE_DIM + MLA_ROPE_DIM).astype(BF16),
        "w_kc": _head_slots(P["w_uk"][l], MLA_NOPE_DIM).astype(BF16),
        "w_kr": place.astype(BF16),
        "w_kr_rows": place[:MLA_ROPE_DIM].astype(BF16),
        "w_uv": P["w_uv"][l].astype(BF16),
        "lq1": row(P["diff_lq1"][l]), "lk1": row(P["diff_lk1"][l]),
        "lq2": row(P["diff_lq2"][l]), "lk2": row(P["diff_lk2"][l]),
        "subln_g": row(P["diff_subln_g"][l]),
        "w_o": P["w_o"][l].astype(BF16),
    }
    if l % 2 == 0:
        i = l // 2
        lw.update(w_gate=P["w_gate"][i].astype(BF16), w_up=P["w_up"][i].astype(BF16),
                  w_down=P["w_down"][i].astype(BF16))
    else:
        m = l // 2
        lw.update(router=P["router"][m], moe_w_gate=P["moe_w_gate"][m],
                  moe_w_up=P["moe_w_up"][m], moe_w_down=P["moe_w_down"][m])
    return lw


def _run_group(x, mods, weights, final_g, *, batch, seq, mod_row, tables, caches, cache_out):
    assert DEPTH % 2 == 0
    own = []
    for l in range(DEPTH):
        lw = weights[l]
        if not cache_out:
            outs = _pre_mixer(x, mods[l], lw, tables, seq=seq, mod_row=mod_row)
        elif l < DEPTH - 1:
            outs = _pre_mixer(x, mods[l], lw, tables, seq=seq, mod_row=mod_row, cache_out="flat")
            own.append(outs[6:])
        else:
            outs = _pre_mixer(x, mods[l], lw, tables, seq=seq, mod_row=mod_row, cache_out="stacked",
                              prev=own)
            own = outs[6:]
        mixed = _attention(outs[:6], lw, l, batch=batch, seq=seq, cache=caches)
        if l % 2 == 0:
            x = _dense_ffn(x, mixed, mods[l], lw, mod_row=mod_row)
        else:
            assert l == DEPTH - 1
            x = _sparse_moe(x, mixed, mods[l], lw, final_g, mod_row=mod_row)
    return x, own


def kernel(x_prompt, x_sample, cache_diff_k, cache_diff_v, cache_mla_ckv, cache_mla_krope, c, c_ctx, w_ada, b_ada, norm_mix_g, norm_ffn_g, w_in, mla_q_norm_g, mla_kv_norm_g, w_uq, w_uk, w_uv, diff_lq1, diff_lk1, diff_lq2, diff_lk2, diff_subln_g, w_o, w_gate, w_up, w_down, router, moe_w_gate, moe_w_up, moe_w_down, final_norm_g):
    P = dict(norm_mix_g=norm_mix_g, norm_ffn_g=norm_ffn_g, w_in=w_in, mla_q_norm_g=mla_q_norm_g,
             mla_kv_norm_g=mla_kv_norm_g, w_uq=w_uq, w_uk=w_uk, w_uv=w_uv, diff_lq1=diff_lq1,
             diff_lk1=diff_lk1, diff_lq2=diff_lq2, diff_lk2=diff_lk2, diff_subln_g=diff_subln_g,
             w_o=w_o, w_gate=w_gate, w_up=w_up, w_down=w_down, router=router,
             moe_w_gate=moe_w_gate, moe_w_up=moe_w_up, moe_w_down=moe_w_down)
    bp, sp, d = x_prompt.shape
    bs, ss, _ = x_sample.shape
    n_past = cache_diff_k.shape[2]
    w4 = N_HEADS * HEAD_W

    cond = jnp.zeros((COND_ROWS, d), F32).at[0].set(c_ctx).at[1:1 + bs].set(c)
    mod_all = _ada_table(cond, w_ada, b_ada).reshape(DEPTH, COND_ROWS, N_MOD, d)
    mods = [mod_all[l] for l in range(DEPTH)]
    weights = [_layer_weights(l, P) for l in range(DEPTH)]
    final_g = final_norm_g.reshape(1, d)

    yp, own = _run_group(
        x_prompt.reshape(bp * sp, d), mods, weights, final_g, batch=bp, seq=sp,
        mod_row=lambda i, tm: 0, tables=None, caches=None, cache_out=True)
    y_prompt = yp.reshape(bp, sp, d)
    new_diff_k, new_diff_v, new_mla_ckv, new_mla_krope = own

    caches = (cache_diff_k.reshape(bs, DEPTH, n_past, w4), cache_diff_v.reshape(bs, DEPTH, n_past, w4),
              cache_mla_ckv, cache_mla_krope)
    ys, _ = _run_group(
        x_sample.reshape(bs * ss, d), mods, weights, final_g, batch=bs, seq=ss,
        mod_row=lambda i, tm: 1 + (i * tm) // ss,
        tables=_rope_tables(ss), caches=caches, cache_out=False)
    y_sample = ys.reshape(bs, ss, d)

    return (y_prompt, y_sample, new_diff_k, new_diff_v, new_mla_ckv, new_mla_krope)
```

```python
import functools
import math
from typing import NamedTuple

import jax
import jax.numpy as jnp
from jax import lax
from jax.experimental import pallas as pl
from jax.experimental.pallas import tpu as pltpu

F32 = jnp.float32
BF16 = jnp.bfloat16

D_MODEL = 1024
DEPTH = 2
GRID_W = 64
N_HEADS = 4
HEAD_W = 128
DIFF_HEAD_DIM = 64
MLA_NOPE_DIM = 64
MLA_ROPE_DIM = 32
Q_LORA = 256
KV_LORA = 128
D_FF = 2816
FF_CHUNK = 1408
N_EXPERTS = 8
D_FF_EXPERT = 1408
EXPERT_FF_CHUNKS = ((0, 512), (512, 512), (1024, 384))
ROPE_THETA = 10000.0
RMS_EPS = 1e-6
LOG2_E = math.log2(math.e)
N_MOD = 6
COND_ROWS = 16

TOKEN_TILE = 256
FFN_TOKEN_TILE = 512
MOE_ROW_TILE = 512
MOE_TOKEN_TILE = 512
SLAB_ROWS = 8
TOP_K = 2
VMEM_LIMIT_BYTES = 60 * 1024 * 1024

O_QD, O_KD, O_VD, O_CQ, O_CKV, O_KR, O_END = 0, 512, 1024, 1536, 1792, 1920, 1952
W_IN_PAD = 2048


def _params(sem):
    return pltpu.CompilerParams(dimension_semantics=sem, vmem_limit_bytes=VMEM_LIMIT_BYTES)


def _const_spec(shape):
    nd = len(shape)
    return pl.BlockSpec(shape, lambda *_: (0,) * nd)


def _rms(x, g):
    return x * lax.rsqrt(jnp.mean(x * x, axis=-1, keepdims=True) + RMS_EPS) * g


def _split_bf16(x):
    hi = x.astype(BF16)
    lo = (x - hi.astype(F32)).astype(BF16)
    return hi, lo


def _dot(a, b):
    return jnp.dot(a, b, preferred_element_type=F32)


def _dot3(a, b):
    a_hi, a_lo = _split_bf16(a)
    b_hi, b_lo = _split_bf16(b)
    return _dot(a_hi, b_hi) + _dot(a_hi, b_lo) + _dot(a_lo, b_hi)


def _ada_kernel(cond_ref, w_ref, b_ref, out_ref):
    cond = cond_ref[...]
    s = cond * jax.nn.sigmoid(cond)
    out_ref[0] = _dot3(s, w_ref[0]) + b_ref[0]


def _ada_table(cond, w_ada, b_ada):
    d = D_MODEL
    return pl.pallas_call(
        _ada_kernel,
        grid=(DEPTH, N_MOD),
        in_specs=[
            pl.BlockSpec((COND_ROWS, d), lambda l, j: (0, 0)),
            pl.BlockSpec((1, d, d), lambda l, j: (l, 0, j)),
            pl.BlockSpec((1, 1, d), lambda l, j: (l, 0, j)),
        ],
        out_specs=pl.BlockSpec((1, COND_ROWS, d), lambda l, j: (l, 0, j)),
        out_shape=jax.ShapeDtypeStruct((DEPTH, COND_ROWS, N_MOD * d), F32),
        compiler_params=_params(("arbitrary", "arbitrary")),
        name="ada_table",
    )(cond, w_ada, b_ada.reshape(DEPTH, 1, N_MOD * d))


def _axial_tables(seq, dim):
    half = dim // 4
    freqs = ROPE_THETA ** (-jnp.arange(half, dtype=F32) / half)
    pos = jnp.arange(seq, dtype=jnp.int32)
    rows = (pos // GRID_W).astype(F32)[:, None] * freqs[None, :]
    cols = (pos % GRID_W).astype(F32)[:, None] * freqs[None, :]
    cos = jnp.concatenate([jnp.cos(rows), jnp.cos(rows), jnp.cos(cols), jnp.cos(cols)], axis=-1)
    sin = jnp.concatenate([-jnp.sin(rows), jnp.sin(rows), -jnp.sin(cols), jnp.sin(cols)], axis=-1)
    return cos, sin


def _rope_tables(seq):
    cos64, sin64 = _axial_tables(seq, DIFF_HEAD_DIM)
    cos32, sin32 = _axial_tables(seq, MLA_ROPE_DIM)
    ones = lambda n: jnp.ones((seq, n), F32)
    zeros = lambda n: jnp.zeros((seq, n), F32)
    cos_d = jnp.tile(cos64, (1, 2 * N_HEADS))
    sin_d = jnp.tile(sin64, (1, 2 * N_HEADS))
    pad = HEAD_W - MLA_NOPE_DIM - MLA_ROPE_DIM
    cos_m = jnp.tile(jnp.concatenate([ones(MLA_NOPE_DIM), cos32, ones(pad)], axis=-1), (1, N_HEADS))
    sin_m = jnp.tile(jnp.concatenate([zeros(MLA_NOPE_DIM), sin32, zeros(pad)], axis=-1), (1, N_HEADS))
    cos_r = jnp.concatenate([cos32, ones(HEAD_W - MLA_ROPE_DIM)], axis=-1)
    sin_r = jnp.concatenate([sin32, zeros(HEAD_W - MLA_ROPE_DIM)], axis=-1)
    return cos_d, sin_d, cos_m, sin_m, cos_r, sin_r


def _rope(x, cos, sin, block):
    width = x.shape[-1]
    lane = lax.broadcasted_iota(jnp.int32, x.shape, 1)
    first = (lane % (2 * block)) < block
    partner = jnp.where(first, pltpu.roll(x, width - block, 1), pltpu.roll(x, block, 1))
    return x * cos + partner * sin


def _store_heads(ref, layer, x):
    for hd in range(N_HEADS):
        ref[0, layer, :, hd, :] = x[:, hd * HEAD_W:(hd + 1) * HEAD_W]


def _pre_mixer_kernel(*refs, rope, cache_out, n_prev):
    it = iter(refs)
    x_ref, mod_ref, g_ref, win_ref, qg_ref, kvg_ref, wuq_ref, wkc_ref, wkr_ref, wuv_ref = (
        next(it) for _ in range(10))
    if rope:
        cos_d, sin_d, cos_m, sin_m, cos_r, sin_r = (next(it) for _ in range(6))
    prev = [[next(it) for _ in range(4)] for _ in range(n_prev)]
    qd_ref, kd_ref, vd_ref, qm_ref, km_ref, vm_ref = (next(it) for _ in range(6))
    if cache_out:
        kd32_ref, vd32_ref, ckv32_ref, kr32_ref = (next(it) for _ in range(4))

    x = x_ref[...]
    h = _rms(x, g_ref[...]) * (1.0 + mod_ref[0, 1:2, :]) + mod_ref[0, 0:1, :]
    z = _dot(h.astype(BF16), win_ref[...])

    qd = z[:, O_QD:O_KD]
    kd = z[:, O_KD:O_VD]
    vd = z[:, O_VD:O_CQ]
    cq = z[:, O_CQ:O_CKV]
    ckv = z[:, O_CKV:O_KR]
    kr = z[:, O_KR:W_IN_PAD]

    qm = _dot(_rms(cq, qg_ref[...]).astype(BF16), wuq_ref[...])
    ckv = _rms(ckv, kvg_ref[...])
    if rope:
        qd = _rope(qd, cos_d[...], sin_d[...], DIFF_HEAD_DIM // 4)
        kd = _rope(kd, cos_d[...], sin_d[...], DIFF_HEAD_DIM // 4)
        qm = _rope(qm, cos_m[...], sin_m[...], MLA_ROPE_DIM // 4)
        kr = _rope(kr, cos_r[...], sin_r[...], MLA_ROPE_DIM // 4)

    ckv_b = ckv.astype(BF16)
    qd_ref[...] = (qd * (LOG2_E * DIFF_HEAD_DIM ** -0.5)).astype(BF16)
    kd_ref[...] = kd.astype(BF16)
    vd_ref[...] = vd.astype(BF16)
    qm_ref[...] = (qm * (LOG2_E * (MLA_NOPE_DIM + MLA_ROPE_DIM) ** -0.5)).astype(BF16)
    km_ref[...] = (_dot(ckv_b, wkc_ref[...]) + _dot(kr.astype(BF16), wkr_ref[...])).astype(BF16)
    vm_ref[...] = _dot(ckv_b, wuv_ref[...]).astype(BF16)
    if cache_out == "flat":
        kd32_ref[...] = kd
        vd32_ref[...] = vd
        ckv32_ref[...] = ckv
        kr32_ref[...] = kr[:, :MLA_ROPE_DIM]
    elif cache_out == "stacked":
        for l, (pk, pv, pc, pr) in enumerate(prev):
            _store_heads(kd32_ref, l, pk[...])
            _store_heads(vd32_ref, l, pv[...])
            ckv32_ref[0, l] = pc[...]
            kr32_ref[0, l] = pr[...]
        _store_heads(kd32_ref, n_prev, kd)
        _store_heads(vd32_ref, n_prev, vd)
        ckv32_ref[0, n_prev] = ckv
        kr32_ref[0, n_prev] = kr[:, :MLA_ROPE_DIM]


def _pre_mixer(x, mod, lw, tables, *, seq, mod_row, cache_out=None, prev=()):
    n = x.shape[0]
    tm = TOKEN_TILE
    tiles_per_seq = seq // tm
    rope = tables is not None
    row = lambda i: (i, 0)
    args = [x, mod, lw["norm_mix_g"], lw["w_in"], lw["q_norm_g"], lw["kv_norm_g"],
            lw["w_uq"], lw["w_kc"], lw["w_kr"], lw["w_uv"]]
    specs = [pl.BlockSpec((tm, D_MODEL), row),
             pl.BlockSpec((1, N_MOD, D_MODEL), lambda i: (mod_row(i, tm), 0, 0))]
    specs += [_const_spec(a.shape) for a in args[2:]]
    if rope:
        for t in tables:
            args.append(t)
            specs.append(pl.BlockSpec((tm, t.shape[1]), lambda i: (i % tiles_per_seq, 0)))
    for layer_rows in prev:
        for a in layer_rows:
            args.append(a)
            specs.append(pl.BlockSpec((tm, a.shape[1]), row))
    w4 = N_HEADS * HEAD_W
    out_shape = [jax.ShapeDtypeStruct((n, w4), BF16)] * 6
    out_specs = [pl.BlockSpec((tm, w4), row)] * 6
    if cache_out == "flat":
        for w in (w4, w4, KV_LORA, MLA_ROPE_DIM):
            out_shape.append(jax.ShapeDtypeStruct((n, w), F32))
            out_specs.append(pl.BlockSpec((tm, w), row))
    elif cache_out == "stacked":
        n_layers = len(prev) + 1
        for tail in ((N_HEADS, HEAD_W), (N_HEADS, HEAD_W), (KV_LORA,), (MLA_ROPE_DIM,)):
            zeros = (0,) * len(tail)
            out_shape.append(jax.ShapeDtypeStruct((n // seq, n_layers, seq) + tail, F32))
            out_specs.append(pl.BlockSpec(
                (1, n_layers, tm) + tail,
                lambda i, zeros=zeros: (i // tiles_per_seq, 0, i % tiles_per_seq) + zeros))
    return pl.pallas_call(
        functools.partial(_pre_mixer_kernel, rope=rope, cache_out=cache_out, n_prev=len(prev)),
        grid=(n // tm,),
        in_specs=specs,
        out_specs=out_specs,
        out_shape=out_shape,
        compiler_params=_params(("parallel",)),
        name="pre_mixer",
    )(*args)


def _softmax_terms(s):
    m = jnp.max(s, axis=-1, keepdims=True)
    e = jnp.exp2(s - m)
    return e, 1.0 / jnp.sum(e, axis=-1, keepdims=True)


def _nt_dot(a, b):
    return lax.dot_general(a, b, (((1,), (1,)), ((), ())), preferred_element_type=F32)


def _attention_kernel(*refs, with_cache, lam_init):
    it = iter(refs)
    qd_ref, qm_ref, kd_ref, vd_ref, km_ref, vm_ref = (next(it) for _ in range(6))
    lq1, lk1, lq2, lk2, subg_ref = (next(it) for _ in range(5))
    if with_cache:
        ckd_ref, cvd_ref, cckv_ref, ckr_ref, wkc_ref, wkr_ref, wuv_ref = (next(it) for _ in range(7))
    out_ref = next(it)
    if with_cache:
        kd_all, vd_all, km_all, vm_all = (next(it) for _ in range(4))
        n_cache = cckv_ref.shape[2]

        @pl.when(pl.program_id(1) == 0)
        def _fill():
            kd_all[:n_cache, :] = ckd_ref[0, 0].astype(BF16)
            vd_all[:n_cache, :] = cvd_ref[0, 0].astype(BF16)
            cckv = cckv_ref[0, 0].astype(BF16)
            km_all[:n_cache, :] = (_dot(cckv, wkc_ref[...])
                                   + _dot(ckr_ref[0, 0].astype(BF16), wkr_ref[...])).astype(BF16)
            vm_all[:n_cache, :] = _dot(cckv, wuv_ref[...]).astype(BF16)
            kd_all[n_cache:, :] = kd_ref[...]
            vd_all[n_cache:, :] = vd_ref[...]
            km_all[n_cache:, :] = km_ref[...]
            vm_all[n_cache:, :] = vm_ref[...]
    else:
        kd_all, vd_all, km_all, vm_all = kd_ref, vd_ref, km_ref, vm_ref

    lam = (jnp.exp(jnp.sum(lq1[...] * lk1[...], axis=-1, keepdims=True))
           - jnp.exp(jnp.sum(lq2[...] * lk2[...], axis=-1, keepdims=True)) + lam_init)
    subg = subg_ref[...]
    tq = qd_ref.shape[0]
    lane = lax.broadcasted_iota(jnp.int32, (tq, HEAD_W), 1)
    first = lane < DIFF_HEAD_DIM

    for hd in range(N_HEADS):
        sl = slice(hd * HEAD_W, (hd + 1) * HEAD_W)
        q = qd_ref[:, sl]
        k = kd_all[:, sl]
        zero = jnp.zeros_like(q)
        e1, r1 = _softmax_terms(_nt_dot(jnp.where(first, q, zero), k))
        e2, r2 = _softmax_terms(_nt_dot(jnp.where(first, zero, q), k))
        p = (e1 * r1 - e2 * (lam * r2)).astype(BF16)
        o = _dot(p, vd_all[:, sl])
        out_ref[:, sl] = (_rms(o, subg) * (1.0 - lam_init)).astype(BF16)

    for hd in range(N_HEADS):
        sl = slice(hd * HEAD_W, (hd + 1) * HEAD_W)
        e, r = _softmax_terms(_nt_dot(qm_ref[:, sl], km_all[:, sl]))
        o = _dot(e.astype(BF16), vm_all[:, sl]) * r
        out_ref[:, N_HEADS * HEAD_W + hd * HEAD_W:N_HEADS * HEAD_W + (hd + 1) * HEAD_W] = o.astype(BF16)


def _attention(qkv, lw, layer, *, batch, seq, cache):
    qd, kd, vd, qm, km, vm = qkv
    w4 = N_HEADS * HEAD_W
    tq = TOKEN_TILE
    nq = seq // tq
    lam_init = 0.8 - 0.6 * math.exp(-0.3 * layer)
    q_spec = pl.BlockSpec((tq, w4), lambda b, j: (b * nq + j, 0))
    kv_spec = pl.BlockSpec((seq, w4), lambda b, j: (b, 0))
    args = [qd, qm, kd, vd, km, vm, lw["lq1"], lw["lk1"], lw["lq2"], lw["lk2"], lw["subln_g"]]
    specs = [q_spec, q_spec, kv_spec, kv_spec, kv_spec, kv_spec] + [_const_spec(a.shape) for a in args[6:]]
    scratch = []
    if cache is not None:
        ckd, cvd, cckv, ckr = cache
        n_cache = ckd.shape[2]
        for a in (ckd, cvd, cckv, ckr):
            args.append(a)
            specs.append(pl.BlockSpec((1, 1) + a.shape[2:], lambda b, j: (b, layer, 0, 0)))
        for name in ("w_kc", "w_kr_rows", "w_uv"):
            args.append(lw[name])
            specs.append(_const_spec(lw[name].shape))
        scratch = [pltpu.VMEM((n_cache + seq, w4), BF16)] * 4
    return pl.pallas_call(
        functools.partial(_attention_kernel, with_cache=cache is not None, lam_init=lam_init),
        grid=(batch, nq),
        in_specs=specs,
        out_specs=pl.BlockSpec((tq, 2 * w4), lambda b, j: (b * nq + j, 0)),
        out_shape=jax.ShapeDtypeStruct((batch * seq, 2 * w4), BF16),
        scratch_shapes=scratch,
        compiler_params=_params(("parallel", "arbitrary")),
        name="attention",
    )(*args)


def _post_mixer(x_ref, mixed_ref, mod_ref, wo_ref, g_ref):
    x1 = x_ref[...] + mod_ref[0, 2:3, :] * _dot(mixed_ref[...], wo_ref[...])
    h = _rms(x1, g_ref[...]) * (1.0 + mod_ref[0, 4:5, :]) + mod_ref[0, 3:4, :]
    return x1, h


def _silu(g):
    return g * jax.nn.sigmoid(g)


def _dense_ffn_kernel(x_ref, mixed_ref, mod_ref, wo_ref, g_ref, wg_ref, wu_ref, wd_ref, out_ref):
    x1, h = _post_mixer(x_ref, mixed_ref, mod_ref, wo_ref, g_ref)
    hb = h.astype(BF16)
    acc = jnp.zeros_like(x1)
    for c in range(D_FF // FF_CHUNK):
        sl = slice(c * FF_CHUNK, (c + 1) * FF_CHUNK)
        act = _silu(_dot(hb, wg_ref[:, sl])) * _dot(hb, wu_ref[:, sl])
        acc = acc + _dot(act.astype(BF16), wd_ref[sl, :])
    out_ref[...] = x1 + mod_ref[0, 5:6, :] * acc


def _dense_ffn(x, mixed, mod, lw, *, mod_row):
    n = x.shape[0]
    tm = FFN_TOKEN_TILE
    row = lambda i: (i, 0)
    weights = [lw["w_o"], lw["norm_ffn_g"], lw["w_gate"], lw["w_up"], lw["w_down"]]
    wspecs = [pl.BlockSpec(w.shape, lambda i: (0, 0), pipeline_mode=pl.Buffered(1)) for w in weights]
    return pl.pallas_call(
        _dense_ffn_kernel,
        grid=(n // tm,),
        in_specs=[pl.BlockSpec((tm, D_MODEL), row), pl.BlockSpec((tm, D_MODEL), row),
                  pl.BlockSpec((1, N_MOD, D_MODEL), lambda i: (mod_row(i, tm), 0, 0))] + wspecs,
        out_specs=pl.BlockSpec((tm, D_MODEL), row),
        out_shape=jax.ShapeDtypeStruct((n, D_MODEL), F32),
        compiler_params=_params(("parallel",)),
        name="dense_ffn",
    )(x, mixed, mod, *weights)


def _router_kernel(x_ref, mixed_ref, mod_ref, wo_ref, g_ref, router_ref, x1_ref, h_ref, topi_ref, topg_ref):
    x1, h = _post_mixer(x_ref, mixed_ref, mod_ref, wo_ref, g_ref)
    x1_ref[...] = x1
    h_ref[...] = h
    logits = _dot3(h, router_ref[...])
    ex = jnp.exp(logits - jnp.max(logits, axis=-1, keepdims=True))
    probs = ex / jnp.sum(ex, axis=-1, keepdims=True)
    idx = lax.broadcasted_iota(jnp.int32, probs.shape, 1)
    p1 = jnp.max(probs, axis=-1, keepdims=True)
    i1 = jnp.min(jnp.where(probs == p1, idx, N_EXPERTS), axis=-1, keepdims=True)
    rest = jnp.where(idx == i1, -1.0, probs)
    p2 = jnp.max(rest, axis=-1, keepdims=True)
    i2 = jnp.min(jnp.where(rest == p2, idx, N_EXPERTS), axis=-1, keepdims=True)
    den = p1 + p2
    first = lax.broadcasted_iota(jnp.int32, topi_ref.shape, 1) == 0
    topi_ref[...] = jnp.where(first, i1, i2)
    topg_ref[...] = jnp.where(first, p1 / den, p2 / den)


def _router(x, mixed, mod, lw, *, mod_row):
    n = x.shape[0]
    tm = TOKEN_TILE
    row = lambda i: (i, 0)
    weights = [lw["w_o"], lw["norm_ffn_g"], lw["router"]]
    return pl.pallas_call(
        _router_kernel,
        grid=(n // tm,),
        in_specs=[pl.BlockSpec((tm, D_MODEL), row), pl.BlockSpec((tm, D_MODEL), row),
                  pl.BlockSpec((1, N_MOD, D_MODEL), lambda i: (mod_row(i, tm), 0, 0))]
                 + [_const_spec(w.shape) for w in weights],
        out_specs=[pl.BlockSpec((tm, D_MODEL), row), pl.BlockSpec((tm, D_MODEL), row),
                   pl.BlockSpec((tm, TOP_K), row), pl.BlockSpec((tm, TOP_K), row)],
        out_shape=[jax.ShapeDtypeStruct((n, D_MODEL), F32), jax.ShapeDtypeStruct((n, D_MODEL), F32),
                   jax.ShapeDtypeStruct((n, TOP_K), jnp.int32), jax.ShapeDtypeStruct((n, TOP_K), F32)],
        compiler_params=_params(("parallel",)),
        name="router",
    )(x, mixed, mod, *weights)


class _RoutePlan(NamedTuple):
    pos: jax.Array
    dst: jax.Array
    seg_slabs: jax.Array
    seg_local: jax.Array
    seg_global: jax.Array
    tile_expert: jax.Array
    n_used: jax.Array
    gsize: jax.Array
    gend: jax.Array


def _max_row_tiles(n_pairs, tt, tm):
    n_segments = n_pairs // (tt * TOP_K) * N_EXPERTS
    return (n_pairs + n_segments * (SLAB_ROWS - 1) + N_EXPERTS * (tm - 1)) // tm


def _route_plan(top_i, tt, tm):
    i32 = jnp.int32
    n = top_i.shape[0]
    pairs = tt * TOP_K
    n_tt = n // tt
    onehot = (top_i.reshape(n_tt, pairs)[..., None] == jnp.arange(N_EXPERTS, dtype=i32)).astype(i32)
    csum = jnp.cumsum(onehot, axis=1)
    rank = jnp.sum(onehot * (csum - 1), axis=-1)
    seg = (csum[:, -1, :] + SLAB_ROWS - 1) // SLAB_ROWS * SLAB_ROWS
    seg_local = jnp.cumsum(seg, axis=1) - seg
    gsize = (jnp.sum(seg, axis=0) + tm - 1) // tm * tm
    gend = jnp.cumsum(gsize)
    seg_global = (gend - gsize)[None, :] + jnp.cumsum(seg, axis=0) - seg
    take = lambda table: jnp.sum(onehot * table[:, None, :], axis=-1)
    pos = (take(seg_local) + rank).reshape(n, TOP_K).T
    dst = (take(seg_global) + rank).reshape(-1)
    n_tiles = _max_row_tiles(n * TOP_K, tt, tm)
    n_used = gend[-1] // tm
    tile_start = jnp.minimum(jnp.arange(n_tiles, dtype=i32), n_used - 1) * tm
    tile_expert = jnp.sum((gend[None, :] <= tile_start[:, None]).astype(i32), axis=1)
    flat = lambda a: a.reshape(-1).astype(i32)
    return _RoutePlan(pos.astype(i32), dst.astype(i32), flat(seg // SLAB_ROWS), flat(seg_local),
                      flat(seg_global), tile_expert.astype(i32), n_used.reshape(1).astype(i32),
                      gsize.astype(i32), gend.astype(i32))


def _dispatch_kernel(slabs_ref, local_ref, global_ref, gsize_ref, gend_ref, nu_ref,
                     pos_ref, h_ref, xs_ref, sorted_ref, zero_ref, sem):
    i = pl.program_id(0)
    tm = zero_ref.shape[0]
    n_sorted, tt = sorted_ref.shape[0], h_ref.shape[0]

    @pl.when(i == 0)
    def _zero_unused():
        zero_ref[...] = jnp.zeros_like(zero_ref)

        def zero_tile(start):
            cp = pltpu.make_async_copy(zero_ref, xs_ref.at[pl.ds(pl.multiple_of(start, tm), tm)], sem)
            cp.start()
            cp.wait()

        for e in range(N_EXPERTS):
            @pl.when(gsize_ref[e] > 0)
            def _():
                zero_tile(gend_ref[e] - tm)

        def unused(t, carry):
            zero_tile(t * tm)
            return carry

        lax.fori_loop(nu_ref[0], xs_ref.shape[0] // tm, unused, 0)

    rows = lax.broadcasted_iota(jnp.int32, (n_sorted, tt), 0)
    hit = (rows == pos_ref[0:1, :]) | (rows == pos_ref[1:2, :])
    sorted_ref[...] = _dot(jnp.where(hit, 1.0, 0.0).astype(BF16), h_ref[...].astype(BF16))

    def slab_copy(src_row, dst_row):
        return pltpu.make_async_copy(
            sorted_ref.at[pl.ds(pl.multiple_of(src_row, SLAB_ROWS), SLAB_ROWS)],
            xs_ref.at[pl.ds(pl.multiple_of(dst_row, SLAB_ROWS), SLAB_ROWS)], sem)

    for e in range(N_EXPERTS):
        s = i * N_EXPERTS + e

        def issue(j, carry, s=s):
            slab_copy(local_ref[s] + j * SLAB_ROWS, global_ref[s] + j * SLAB_ROWS).start()
            return carry

        lax.fori_loop(0, slabs_ref[s], issue, 0)

    for e in range(N_EXPERTS):
        def drain(j, carry):
            slab_copy(0, 0).wait()
            return carry

        lax.fori_loop(0, slabs_ref[i * N_EXPERTS + e], drain, 0)


def _dispatch(h, plan, tt, tm):
    n = h.shape[0]
    n_rows = _max_row_tiles(n * TOP_K, tt, tm) * tm
    n_sorted = tt * TOP_K + N_EXPERTS * SLAB_ROWS
    return pl.pallas_call(
        _dispatch_kernel,
        grid_spec=pltpu.PrefetchScalarGridSpec(
            num_scalar_prefetch=6,
            grid=(n // tt,),
            in_specs=[pl.BlockSpec((TOP_K, tt), lambda i, *_: (0, i)),
                      pl.BlockSpec((tt, D_MODEL), lambda i, *_: (i, 0))],
            out_specs=pl.BlockSpec(memory_space=pl.ANY),
            scratch_shapes=[pltpu.VMEM((n_sorted, D_MODEL), F32), pltpu.VMEM((tm, D_MODEL), F32),
                            pltpu.SemaphoreType.DMA(())],
        ),
        out_shape=jax.ShapeDtypeStruct((n_rows, D_MODEL), F32),
        compiler_params=_params(("arbitrary",)),
        name="moe_dispatch",
    )(plan.seg_slabs, plan.seg_local, plan.seg_global, plan.gsize, plan.gend, plan.n_used, plan.pos, h)


def _expert_ffn_kernel(te_ref, nu_ref, xs_ref, wg_ref, wu_ref, wd_ref, ys_ref, wg_b, wu_b, wd_b):
    i = pl.program_id(0)

    @pl.when((i == 0) | (te_ref[i] != te_ref[jnp.maximum(i - 1, 0)]))
    def _new_expert():
        wg_b[...] = wg_ref[0].astype(BF16)
        wu_b[...] = wu_ref[0].astype(BF16)
        wd_b[...] = wd_ref[0].astype(BF16)

    @pl.when(i < nu_ref[0])
    def _compute():
        xb = xs_ref[...].astype(BF16)
        y = jnp.zeros(ys_ref.shape, F32)
        for start, size in EXPERT_FF_CHUNKS:
            sl = slice(start, start + size)
            act = _silu(_dot(xb, wg_b[:, sl])) * _dot(xb, wu_b[:, sl])
            y = y + _dot(act.astype(BF16), wd_b[sl, :])
        ys_ref[...] = y

    @pl.when(i >= nu_ref[0])
    def _unused():
        ys_ref[...] = jnp.zeros_like(ys_ref)


def _expert_ffn(xs, plan, lw, tm):
    tile_expert, n_used = plan.tile_expert, plan.n_used
    n_tiles = xs.shape[0] // tm
    wspec = lambda w: pl.BlockSpec((1,) + w.shape[1:], lambda i, te, nu: (te[i], 0, 0))
    return pl.pallas_call(
        _expert_ffn_kernel,
        grid_spec=pltpu.PrefetchScalarGridSpec(
            num_scalar_prefetch=2,
            grid=(n_tiles,),
            in_specs=[pl.BlockSpec((tm, D_MODEL), lambda i, te, nu: (jnp.minimum(i, nu[0] - 1), 0)),
                      wspec(lw["moe_w_gate"]), wspec(lw["moe_w_up"]), wspec(lw["moe_w_down"])],
            out_specs=pl.BlockSpec((tm, D_MODEL), lambda i, te, nu: (i, 0)),
            scratch_shapes=[pltpu.VMEM((D_MODEL, D_FF_EXPERT), BF16), pltpu.VMEM((D_MODEL, D_FF_EXPERT), BF16),
                            pltpu.VMEM((D_FF_EXPERT, D_MODEL), BF16)],
        ),
        out_shape=jax.ShapeDtypeStruct(xs.shape, F32),
        compiler_params=_params(("arbitrary",)),
        name="expert_ffn",
    )(tile_expert, n_used, xs, lw["moe_w_gate"], lw["moe_w_up"], lw["moe_w_down"])


def _combine_kernel(dst_ref, x1_ref, topg_ref, mod_ref, fg_ref, ys_ref, out_ref, buf, sem):
    i = pl.program_id(0)
    n = pl.num_programs(0)
    tm = x1_ref.shape[0]

    def issue(tile, slot):
        base = tile * (tm * TOP_K)

        def body(r, carry):
            for k in range(TOP_K):
                d = dst_ref[base + TOP_K * r + k]
                pltpu.make_async_copy(ys_ref.at[pl.ds(d, 1)], buf.at[slot, k, pl.ds(r, 1)],
                                      sem.at[slot]).start()
            return carry

        lax.fori_loop(0, tm, body, 0, unroll=8)

    @pl.when(i == 0)
    def _first():
        issue(0, 0)

    @pl.when(i + 1 < n)
    def _ahead():
        issue(i + 1, (i + 1) % 2)

    slot = i % 2
    for k in range(TOP_K):
        pltpu.make_async_copy(ys_ref.at[pl.ds(0, tm)], buf.at[slot, k], sem.at[slot]).wait()
    g = topg_ref[...]
    f = g[:, 0:1] * buf[slot, 0] + g[:, 1:2] * buf[slot, 1]
    out_ref[...] = _rms(x1_ref[...] + mod_ref[0, 5:6, :] * f, fg_ref[...])


def _combine(x1, top_g, mod, final_g, ys, plan, *, mod_row):
    dst = plan.dst
    n = x1.shape[0]
    tm = TOKEN_TILE
    row = lambda i, *_: (i, 0)
    return pl.pallas_call(
        _combine_kernel,
        grid_spec=pltpu.PrefetchScalarGridSpec(
            num_scalar_prefetch=1,
            grid=(n // tm,),
            in_specs=[pl.BlockSpec((tm, D_MODEL), row), pl.BlockSpec((tm, TOP_K), row),
                      pl.BlockSpec((1, N_MOD, D_MODEL), lambda i, *_: (mod_row(i, tm), 0, 0)),
                      pl.BlockSpec(final_g.shape, lambda i, *_: (0, 0)),
                      pl.BlockSpec(memory_space=pl.ANY)],
            out_specs=pl.BlockSpec((tm, D_MODEL), row),
            scratch_shapes=[pltpu.VMEM((2, TOP_K, tm, D_MODEL), F32), pltpu.SemaphoreType.DMA((2,))],
        ),
        out_shape=jax.ShapeDtypeStruct((n, D_MODEL), F32),
        compiler_params=_params(("arbitrary",)),
        name="moe_combine",
    )(dst, x1, top_g, mod, final_g, ys)


def _sparse_moe(x, mixed, mod, lw, final_g, *, mod_row):
    x1, h, top_i, top_g = _router(x, mixed, mod, lw, mod_row=mod_row)
    plan = _route_plan(top_i, MOE_TOKEN_TILE, MOE_ROW_TILE)
    xs = _dispatch(h, plan, MOE_TOKEN_TILE, MOE_ROW_TILE)
    ys = _expert_ffn(xs, plan, lw, MOE_ROW_TILE)
    return _combine(x1, top_g, mod, final_g, ys, plan, mod_row=mod_row)


def _head_slots(w, used):
    k = w.shape[0]
    w = w.reshape(k, N_HEADS, used)
    return jnp.pad(w, ((0, 0), (0, 0), (0, HEAD_W - used))).reshape(k, N_HEADS * HEAD_W)


def _layer_weights(l, P):
    row = lambda v: v.reshape(1, -1)
    place = jnp.zeros((HEAD_W, N_HEADS, HEAD_W), F32)
    r = jnp.arange(MLA_ROPE_DIM)
    place = place.at[r, :, MLA_NOPE_DIM + r].set(1.0).reshape(HEAD_W, N_HEADS * HEAD_W)
    lw = {
        "norm_mix_g": row(P["norm_mix_g"][l]),
        "norm_ffn_g": row(P["norm_ffn_g"][l]),
        "w_in": jnp.pad(P["w_in"][l], ((0, 0), (0, W_IN_PAD - O_END))).astype(BF16),
        "q_norm_g": row(P["mla_q_norm_g"][l]),
        "kv_norm_g": row(P["mla_kv_norm_g"][l]),
        "w_uq": _head_slots(P["w_uq"][l], MLA_NOPE_DIM + MLA_ROPE_DIM).astype(BF16),
        "w_kc": _head_slots(P["w_uk"][l], MLA_NOPE_DIM).astype(BF16),
        "w_kr": place.astype(BF16),
        "w_kr_rows": place[:MLA_ROPE_DIM].astype(BF16),
        "w_uv": P["w_uv"][l].astype(BF16),
        "lq1": row(P["diff_lq1"][l]), "lk1": row(P["diff_lk1"][l]),
        "lq2": row(P["diff_lq2"][l]), "lk2": row(P["diff_lk2"][l]),
        "subln_g": row(P["diff_subln_g"][l]),
        "w_o": P["w_o"][l].astype(BF16),
    }
    if l % 2 == 0:
        i = l // 2
        lw.update(w_gate=P["w_gate"][i].astype(BF16), w_up=P["w_up"][i].astype(BF16),
                  w_down=P["w_down"][i].astype(BF16))
    else:
        m = l // 2
        lw.update(router=P["router"][m], moe_w_gate=P["moe_w_gate"][m],
                  moe_w_up=P["moe_w_up"][m], moe_w_down=P["moe_w_down"][m])
    return lw


def _run_group(x, mods, weights, final_g, *, batch, seq, mod_row, tables, caches, cache_out):
    assert DEPTH % 2 == 0
    own = []
    for l in range(DEPTH):
        lw = weights[l]
        if not cache_out:
            outs = _pre_mixer(x, mods[l], lw, tables, seq=seq, mod_row=mod_row)
        elif l < DEPTH - 1:
            outs = _pre_mixer(x, mods[l], lw, tables, seq=seq, mod_row=mod_row, cache_out="flat")
            own.append(outs[6:])
        else:
            outs = _pre_mixer(x, mods[l], lw, tables, seq=seq, mod_row=mod_row, cache_out="stacked",
                              prev=own)
            own = outs[6:]
        mixed = _attention(outs[:6], lw, l, batch=batch, seq=seq, cache=caches)
        if l % 2 == 0:
            x = _dense_ffn(x, mixed, mods[l], lw, mod_row=mod_row)
        else:
            assert l == DEPTH - 1
            x = _sparse_moe(x, mixed, mods[l], lw, final_g, mod_row=mod_row)
    return x, own


def kernel(x_prompt, x_sample, cache_diff_k, cache_diff_v, cache_mla_ckv, cache_mla_krope, c, c_ctx, w_ada, b_ada, norm_mix_g, norm_ffn_g, w_in, mla_q_norm_g, mla_kv_norm_g, w_uq, w_uk, w_uv, diff_lq1, diff_lk1, diff_lq2, diff_lk2, diff_subln_g, w_o, w_gate, w_up, w_down, router, moe_w_gate, moe_w_up, moe_w_down, final_norm_g):
    P = dict(norm_mix_g=norm_mix_g, norm_ffn_g=norm_ffn_g, w_in=w_in, mla_q_norm_g=mla_q_norm_g,
             mla_kv_norm_g=mla_kv_norm_g, w_uq=w_uq, w_uk=w_uk, w_uv=w_uv, diff_lq1=diff_lq1,
             diff_lk1=diff_lk1, diff_lq2=diff_lq2, diff_lk2=diff_lk2, diff_subln_g=diff_subln_g,
             w_o=w_o, w_gate=w_gate, w_up=w_up, w_down=w_down, router=router,
             moe_w_gate=moe_w_gate, moe_w_up=moe_w_up, moe_w_down=moe_w_down)
    bp, sp, d = x_prompt.shape
    bs, ss, _ = x_sample.shape
    n_past = cache_diff_k.shape[2]
    w4 = N_HEADS * HEAD_W

    cond = jnp.zeros((COND_ROWS, d), F32).at[0].set(c_ctx).at[1:1 + bs].set(c)
    mod_all = _ada_table(cond, w_ada, b_ada).reshape(DEPTH, COND_ROWS, N_MOD, d)
    mods = [mod_all[l] for l in range(DEPTH)]
    weights = [_layer_weights(l, P) for l in range(DEPTH)]
    final_g = final_norm_g.reshape(1, d)

    yp, own = _run_group(
        x_prompt.reshape(bp * sp, d), mods, weights, final_g, batch=bp, seq=sp,
        mod_row=lambda i, tm: 0, tables=None, caches=None, cache_out=True)
    y_prompt = yp.reshape(bp, sp, d)
    new_diff_k, new_diff_v, new_mla_ckv, new_mla_krope = own

    caches = (cache_diff_k.reshape(bs, DEPTH, n_past, w4), cache_diff_v.reshape(bs, DEPTH, n_past, w4),
              cache_mla_ckv, cache_mla_krope)
    ys, _ = _run_group(
        x_sample.reshape(bs * ss, d), mods, weights, final_g, batch=bs, seq=ss,
        mod_row=lambda i, tm: 1 + (i * tm) // ss,
        tables=_rope_tables(ss), caches=caches, cache_out=False)
    y_sample = ys.reshape(bs, ss, d)

    return (y_prompt, y_sample, new_diff_k, new_diff_v, new_mla_ckv, new_mla_krope)
```

```python
import functools
import math
from typing import NamedTuple

import jax
import jax.numpy as jnp
from jax import lax
from jax.experimental import pallas as pl
from jax.experimental.pallas import tpu as pltpu

F32 = jnp.float32
BF16 = jnp.bfloat16

D_MODEL = 1024
DEPTH = 2
GRID_W = 64
N_HEADS = 4
HEAD_W = 128
DIFF_HEAD_DIM = 64
MLA_NOPE_DIM = 64
MLA_ROPE_DIM = 32
Q_LORA = 256
KV_LORA = 128
D_FF = 2816
FF_CHUNK = 1408
N_EXPERTS = 8
D_FF_EXPERT = 1408
EXPERT_FF_CHUNKS = ((0, 512), (512, 512), (1024, 384))
ROPE_THETA = 10000.0
RMS_EPS = 1e-6
LOG2_E = math.log2(math.e)
N_MOD = 6
COND_ROWS = 16

TOKEN_TILE = 256
FFN_TOKEN_TILE = 512
MOE_ROW_TILE = 512
MOE_TOKEN_TILE = 512
SLAB_ROWS = 8
TOP_K = 2
VMEM_LIMIT_BYTES = 60 * 1024 * 1024

O_QD, O_KD, O_VD, O_CQ, O_CKV, O_KR, O_END = 0, 512, 1024, 1536, 1792, 1920, 1952
W_IN_PAD = 2048


def _params(sem):
    return pltpu.CompilerParams(dimension_semantics=sem, vmem_limit_bytes=VMEM_LIMIT_BYTES)


def _const_spec(shape):
    nd = len(shape)
    return pl.BlockSpec(shape, lambda *_: (0,) * nd)


def _rms(x, g):
    return x * lax.rsqrt(jnp.mean(x * x, axis=-1, keepdims=True) + RMS_EPS) * g


def _split_bf16(x):
    hi = x.astype(BF16)
    lo = (x - hi.astype(F32)).astype(BF16)
    return hi, lo


def _dot(a, b):
    return jnp.dot(a, b, preferred_element_type=F32)


def _dot3(a, b):
    a_hi, a_lo = _split_bf16(a)
    b_hi, b_lo = _split_bf16(b)
    return _dot(a_hi, b_hi) + _dot(a_hi, b_lo) + _dot(a_lo, b_hi)


def _ada_kernel(cond_ref, w_ref, b_ref, out_ref):
    cond = cond_ref[...]
    s = cond * jax.nn.sigmoid(cond)
    out_ref[0] = _dot3(s, w_ref[0]) + b_ref[0]


def _ada_table(cond, w_ada, b_ada):
    d = D_MODEL
    return pl.pallas_call(
        _ada_kernel,
        grid=(DEPTH, N_MOD),
        in_specs=[
            pl.BlockSpec((COND_ROWS, d), lambda l, j: (0, 0)),
            pl.BlockSpec((1, d, d), lambda l, j: (l, 0, j)),
            pl.BlockSpec((1, 1, d), lambda l, j: (l, 0, j)),
        ],
        out_specs=pl.BlockSpec((1, COND_ROWS, d), lambda l, j: (l, 0, j)),
        out_shape=jax.ShapeDtypeStruct((DEPTH, COND_ROWS, N_MOD * d), F32),
        compiler_params=_params(("arbitrary", "arbitrary")),
        name="ada_table",
    )(cond, w_ada, b_ada.reshape(DEPTH, 1, N_MOD * d))


def _axial_tables(seq, dim):
    half = dim // 4
    freqs = ROPE_THETA ** (-jnp.arange(half, dtype=F32) / half)
    pos = jnp.arange(seq, dtype=jnp.int32)
    rows = (pos // GRID_W).astype(F32)[:, None] * freqs[None, :]
    cols = (pos % GRID_W).astype(F32)[:, None] * freqs[None, :]
    cos = jnp.concatenate([jnp.cos(rows), jnp.cos(rows), jnp.cos(cols), jnp.cos(cols)], axis=-1)
    sin = jnp.concatenate([-jnp.sin(rows), jnp.sin(rows), -jnp.sin(cols), jnp.sin(cols)], axis=-1)
    return cos, sin


def _rope_tables(seq):
    cos64, sin64 = _axial_tables(seq, DIFF_HEAD_DIM)
    cos32, sin32 = _axial_tables(seq, MLA_ROPE_DIM)
    ones = lambda n: jnp.ones((seq, n), F32)
    zeros = lambda n: jnp.zeros((seq, n), F32)
    cos_d = jnp.tile(cos64, (1, 2 * N_HEADS))
    sin_d = jnp.tile(sin64, (1, 2 * N_HEADS))
    pad = HEAD_W - MLA_NOPE_DIM - MLA_ROPE_DIM
    cos_m = jnp.tile(jnp.concatenate([ones(MLA_NOPE_DIM), cos32, ones(pad)], axis=-1), (1, N_HEADS))
    sin_m = jnp.tile(jnp.concatenate([zeros(MLA_NOPE_DIM), sin32, zeros(pad)], axis=-1), (1, N_HEADS))
    cos_r = jnp.concatenate([cos32, ones(HEAD_W - MLA_ROPE_DIM)], axis=-1)
    sin_r = jnp.concatenate([sin32, zeros(HEAD_W - MLA_ROPE_DIM)], axis=-1)
    return cos_d, sin_d, cos_m, sin_m, cos_r, sin_r


def _rope(x, cos, sin, block):
    width = x.shape[-1]
    lane = lax.broadcasted_iota(jnp.int32, x.shape, 1)
    first = (lane % (2 * block)) < block
    partner = jnp.where(first, pltpu.roll(x, width - block, 1), pltpu.roll(x, block, 1))
    return x * cos + partner * sin


def _store_heads(ref, layer, x):
    for hd in range(N_HEADS):
        ref[0, layer, :, hd, :] = x[:, hd * HEAD_W:(hd + 1) * HEAD_W]


def _pre_mixer_kernel(*refs, rope, cache_out, n_prev):
    it = iter(refs)
    x_ref, mod_ref, g_ref, win_ref, qg_ref, kvg_ref, wuq_ref, wkc_ref, wkr_ref, wuv_ref = (
        next(it) for _ in range(10))
    if rope:
        cos_d, sin_d, cos_m, sin_m, cos_r, sin_r = (next(it) for _ in range(6))
    prev = [[next(it) for _ in range(4)] for _ in range(n_prev)]
    qd_ref, kd_ref, vd_ref, qm_ref, km_ref, vm_ref = (next(it) for _ in range(6))
    if cache_out:
        kd32_ref, vd32_ref, ckv32_ref, kr32_ref = (next(it) for _ in range(4))

    x = x_ref[...]
    h = _rms(x, g_ref[...]) * (1.0 + mod_ref[0, 1:2, :]) + mod_ref[0, 0:1, :]
    z = _dot(h.astype(BF16), win_ref[...])

    qd = z[:, O_QD:O_KD]
    kd = z[:, O_KD:O_VD]
    vd = z[:, O_VD:O_CQ]
    cq = z[:, O_CQ:O_CKV]
    ckv = z[:, O_CKV:O_KR]
    kr = z[:, O_KR:W_IN_PAD]

    qm = _dot(_rms(cq, qg_ref[...]).astype(BF16), wuq_ref[...])
    ckv = _rms(ckv, kvg_ref[...])
    if rope:
        qd = _rope(qd, cos_d[...], sin_d[...], DIFF_HEAD_DIM // 4)
        kd = _rope(kd, cos_d[...], sin_d[...], DIFF_HEAD_DIM // 4)
        qm = _rope(qm, cos_m[...], sin_m[...], MLA_ROPE_DIM // 4)
        kr = _rope(kr, cos_r[...], sin_r[...], MLA_ROPE_DIM // 4)

    ckv_b = ckv.astype(BF16)
    qd_ref[...] = (qd * (LOG2_E * DIFF_HEAD_DIM ** -0.5)).astype(BF16)
    kd_ref[...] = kd.astype(BF16)
    vd_ref[...] = vd.astype(BF16)
    qm_ref[...] = (qm * (LOG2_E * (MLA_NOPE_DIM + MLA_ROPE_DIM) ** -0.5)).astype(BF16)
    km_ref[...] = (_dot(ckv_b, wkc_ref[...]) + _dot(kr.astype(BF16), wkr_ref[...])).astype(BF16)
    vm_ref[...] = _dot(ckv_b, wuv_ref[...]).astype(BF16)
    if cache_out == "flat":
        kd32_ref[...] = kd
        vd32_ref[...] = vd
        ckv32_ref[...] = ckv
        kr32_ref[...] = kr[:, :MLA_ROPE_DIM]
    elif cache_out == "stacked":
        for l, (pk, pv, pc, pr) in enumerate(prev):
            _store_heads(kd32_ref, l, pk[...])
            _store_heads(vd32_ref, l, pv[...])
            ckv32_ref[0, l] = pc[...]
            kr32_ref[0, l] = pr[...]
        _store_heads(kd32_ref, n_prev, kd)
        _store_heads(vd32_ref, n_prev, vd)
        ckv32_ref[0, n_prev] = ckv
        kr32_ref[0, n_prev] = kr[:, :MLA_ROPE_DIM]


def _pre_mixer(x, mod, lw, tables, *, seq, mod_row, cache_out=None, prev=()):
    n = x.shape[0]
    tm = TOKEN_TILE
    tiles_per_seq = seq // tm
    rope = tables is not None
    row = lambda i: (i, 0)
    args = [x, mod, lw["norm_mix_g"], lw["w_in"], lw["q_norm_g"], lw["kv_norm_g"],
            lw["w_uq"], lw["w_kc"], lw["w_kr"], lw["w_uv"]]
    specs = [pl.BlockSpec((tm, D_MODEL), row),
             pl.BlockSpec((1, N_MOD, D_MODEL), lambda i: (mod_row(i, tm), 0, 0))]
    specs += [_const_spec(a.shape) for a in args[2:]]
    if rope:
        for t in tables:
            args.append(t)
            specs.append(pl.BlockSpec((tm, t.shape[1]), lambda i: (i % tiles_per_seq, 0)))
    for layer_rows in prev:
        for a in layer_rows:
            args.append(a)
            specs.append(pl.BlockSpec((tm, a.shape[1]), row))
    w4 = N_HEADS * HEAD_W
    out_shape = [jax.ShapeDtypeStruct((n, w4), BF16)] * 6
    out_specs = [pl.BlockSpec((tm, w4), row)] * 6
    if cache_out == "flat":
        for w in (w4, w4, KV_LORA, MLA_ROPE_DIM):
            out_shape.append(jax.ShapeDtypeStruct((n, w), F32))
            out_specs.append(pl.BlockSpec((tm, w), row))
    elif cache_out == "stacked":
        n_layers = len(prev) + 1
        for tail in ((N_HEADS, HEAD_W), (N_HEADS, HEAD_W), (KV_LORA,), (MLA_ROPE_DIM,)):
            zeros = (0,) * len(tail)
            out_shape.append(jax.ShapeDtypeStruct((n // seq, n_layers, seq) + tail, F32))
            out_specs.append(pl.BlockSpec(
                (1, n_layers, tm) + tail,
                lambda i, zeros=zeros: (i // tiles_per_seq, 0, i % tiles_per_seq) + zeros))
    return pl.pallas_call(
        functools.partial(_pre_mixer_kernel, rope=rope, cache_out=cache_out, n_prev=len(prev)),
        grid=(n // tm,),
        in_specs=specs,
        out_specs=out_specs,
        out_shape=out_shape,
        compiler_params=_params(("parallel",)),
        name="pre_mixer",
    )(*args)


def _softmax_terms(s):
    m = jnp.max(s, axis=-1, keepdims=True)
    e = jnp.exp2(s - m)
    return e, 1.0 / jnp.sum(e, axis=-1, keepdims=True)


def _nt_dot(a, b):
    return lax.dot_general(a, b, (((1,), (1,)), ((), ())), preferred_element_type=F32)


def _attention_kernel(*refs, with_cache, lam_init):
    it = iter(refs)
    qd_ref, qm_ref, kd_ref, vd_ref, km_ref, vm_ref = (next(it) for _ in range(6))
    lq1, lk1, lq2, lk2, subg_ref = (next(it) for _ in range(5))
    if with_cache:
        ckd_ref, cvd_ref, cckv_ref, ckr_ref, wkc_ref, wkr_ref, wuv_ref = (next(it) for _ in range(7))
    out_ref = next(it)
    if with_cache:
        kd_all, vd_all, km_all, vm_all = (next(it) for _ in range(4))
        n_cache = cckv_ref.shape[2]

        @pl.when(pl.program_id(1) == 0)
        def _fill():
            kd_all[:n_cache, :] = ckd_ref[0, 0].astype(BF16)
            vd_all[:n_cache, :] = cvd_ref[0, 0].astype(BF16)
            cckv = cckv_ref[0, 0].astype(BF16)
            km_all[:n_cache, :] = (_dot(cckv, wkc_ref[...])
                                   + _dot(ckr_ref[0, 0].astype(BF16), wkr_ref[...])).astype(BF16)
            vm_all[:n_cache, :] = _dot(cckv, wuv_ref[...]).astype(BF16)
            kd_all[n_cache:, :] = kd_ref[...]
            vd_all[n_cache:, :] = vd_ref[...]
            km_all[n_cache:, :] = km_ref[...]
            vm_all[n_cache:, :] = vm_ref[...]
    else:
        kd_all, vd_all, km_all, vm_all = kd_ref, vd_ref, km_ref, vm_ref

    lam = (jnp.exp(jnp.sum(lq1[...] * lk1[...], axis=-1, keepdims=True))
           - jnp.exp(jnp.sum(lq2[...] * lk2[...], axis=-1, keepdims=True)) + lam_init)
    subg = subg_ref[...]
    tq = qd_ref.shape[0]
    lane = lax.broadcasted_iota(jnp.int32, (tq, HEAD_W), 1)
    first = lane < DIFF_HEAD_DIM

    for hd in range(N_HEADS):
        sl = slice(hd * HEAD_W, (hd + 1) * HEAD_W)
        q = qd_ref[:, sl]
        k = kd_all[:, sl]
        zero = jnp.zeros_like(q)
        e1, r1 = _softmax_terms(_nt_dot(jnp.where(first, q, zero), k))
        e2, r2 = _softmax_terms(_nt_dot(jnp.where(first, zero, q), k))
        p = (e1 * r1 - e2 * (lam * r2)).astype(BF16)
        o = _dot(p, vd_all[:, sl])
        out_ref[:, sl] = (_rms(o, subg) * (1.0 - lam_init)).astype(BF16)

    for hd in range(N_HEADS):
        sl = slice(hd * HEAD_W, (hd + 1) * HEAD_W)
        e, r = _softmax_terms(_nt_dot(qm_ref[:, sl], km_all[:, sl]))
        o = _dot(e.astype(BF16), vm_all[:, sl]) * r
        out_ref[:, N_HEADS * HEAD_W + hd * HEAD_W:N_HEADS * HEAD_W + (hd + 1) * HEAD_W] = o.astype(BF16)


def _attention(qkv, lw, layer, *, batch, seq, cache):
    qd, kd, vd, qm, km, vm = qkv
    w4 = N_HEADS * HEAD_W
    tq = TOKEN_TILE
    nq = seq // tq
    lam_init = 0.8 - 0.6 * math.exp(-0.3 * layer)
    q_spec = pl.BlockSpec((tq, w4), lambda b, j: (b * nq + j, 0))
    kv_spec = pl.BlockSpec((seq, w4), lambda b, j: (b, 0))
    args = [qd, qm, kd, vd, km, vm, lw["lq1"], lw["lk1"], lw["lq2"], lw["lk2"], lw["subln_g"]]
    specs = [q_spec, q_spec, kv_spec, kv_spec, kv_spec, kv_spec] + [_const_spec(a.shape) for a in args[6:]]
    scratch = []
    if cache is not None:
        ckd, cvd, cckv, ckr = cache
        n_cache = ckd.shape[2]
        for a in (ckd, cvd, cckv, ckr):
            args.append(a)
            specs.append(pl.BlockSpec((1, 1) + a.shape[2:], lambda b, j: (b, layer, 0, 0)))
        for name in ("w_kc", "w_kr_rows", "w_uv"):
            args.append(lw[name])
            specs.append(_const_spec(lw[name].shape))
        scratch = [pltpu.VMEM((n_cache + seq, w4), BF16)] * 4
    return pl.pallas_call(
        functools.partial(_attention_kernel, with_cache=cache is not None, lam_init=lam_init),
        grid=(batch, nq),
        in_specs=specs,
        out_specs=pl.BlockSpec((tq, 2 * w4), lambda b, j: (b * nq + j, 0)),
        out_shape=jax.ShapeDtypeStruct((batch * seq, 2 * w4), BF16),
        scratch_shapes=scratch,
        compiler_params=_params(("parallel", "arbitrary")),
        name="attention",
    )(*args)


def _post_mixer(x_ref, mixed_ref, mod_ref, wo_ref, g_ref):
    x1 = x_ref[...] + mod_ref[0, 2:3, :] * _dot(mixed_ref[...], wo_ref[...])
    h = _rms(x1, g_ref[...]) * (1.0 + mod_ref[0, 4:5, :]) + mod_ref[0, 3:4, :]
    return x1, h


def _silu(g):
    return g * jax.nn.sigmoid(g)


def _dense_ffn_kernel(x_ref, mixed_ref, mod_ref, wo_ref, g_ref, wg_ref, wu_ref, wd_ref, out_ref):
    x1, h = _post_mixer(x_ref, mixed_ref, mod_ref, wo_ref, g_ref)
    hb = h.astype(BF16)
    acc = jnp.zeros_like(x1)
    for c in range(D_FF // FF_CHUNK):
        sl = slice(c * FF_CHUNK, (c + 1) * FF_CHUNK)
        act = _silu(_dot(hb, wg_ref[:, sl])) * _dot(hb, wu_ref[:, sl])
        acc = acc + _dot(act.astype(BF16), wd_ref[sl, :])
    out_ref[...] = x1 + mod_ref[0, 5:6, :] * acc


def _dense_ffn(x, mixed, mod, lw, *, mod_row):
    n = x.shape[0]
    tm = FFN_TOKEN_TILE
    row = lambda i: (i, 0)
    weights = [lw["w_o"], lw["norm_ffn_g"], lw["w_gate"], lw["w_up"], lw["w_down"]]
    wspecs = [pl.BlockSpec(w.shape, lambda i: (0, 0), pipeline_mode=pl.Buffered(1)) for w in weights]
    return pl.pallas_call(
        _dense_ffn_kernel,
        grid=(n // tm,),
        in_specs=[pl.BlockSpec((tm, D_MODEL), row), pl.BlockSpec((tm, D_MODEL), row),
                  pl.BlockSpec((1, N_MOD, D_MODEL), lambda i: (mod_row(i, tm), 0, 0))] + wspecs,
        out_specs=pl.BlockSpec((tm, D_MODEL), row),
        out_shape=jax.ShapeDtypeStruct((n, D_MODEL), F32),
        compiler_params=_params(("parallel",)),
        name="dense_ffn",
    )(x, mixed, mod, *weights)


def _router_kernel(x_ref, mixed_ref, mod_ref, wo_ref, g_ref, router_ref,
                   x1_ref, h_ref, topi_ref, topg_ref, pos_ref, slabs_ref):
    x1, h = _post_mixer(x_ref, mixed_ref, mod_ref, wo_ref, g_ref)
    x1_ref[...] = x1
    h_ref[...] = h
    logits = _dot3(h, router_ref[...])
    ex = jnp.exp(logits - jnp.max(logits, axis=-1, keepdims=True))
    probs = ex / jnp.sum(ex, axis=-1, keepdims=True)
    idx = lax.broadcasted_iota(jnp.int32, probs.shape, 1)
    p1 = jnp.max(probs, axis=-1, keepdims=True)
    i1 = jnp.min(jnp.where(probs == p1, idx, N_EXPERTS), axis=-1, keepdims=True)
    rest = jnp.where(idx == i1, -1.0, probs)
    p2 = jnp.max(rest, axis=-1, keepdims=True)
    i2 = jnp.min(jnp.where(rest == p2, idx, N_EXPERTS), axis=-1, keepdims=True)
    den = p1 + p2
    first = lax.broadcasted_iota(jnp.int32, topi_ref.shape, 1) == 0
    topi_ref[...] = jnp.where(first, i1, i2)
    topg_ref[...] = jnp.where(first, p1 / den, p2 / den)

    tt = idx.shape[0]
    hot1 = jnp.where(idx == i1, 1.0, 0.0)
    hot2 = jnp.where(idx == i2, 1.0, 0.0)
    tri = jnp.where(lax.broadcasted_iota(jnp.int32, (tt, tt), 0) >= lax.broadcasted_iota(jnp.int32, (tt, tt), 1),
                    1.0, 0.0).astype(BF16)
    before1 = _dot(tri, hot1.astype(BF16)) - hot1
    before2 = _dot(tri, hot2.astype(BF16)) - hot2
    count1 = jnp.sum(hot1, axis=0, keepdims=True)
    count2 = jnp.sum(hot2, axis=0, keepdims=True)
    slabs = jnp.floor((count1 + count2 + (SLAB_ROWS - 1)) * (1.0 / SLAB_ROWS))
    e_row = lax.broadcasted_iota(jnp.int32, (N_EXPERTS, N_EXPERTS), 0)
    e_col = lax.broadcasted_iota(jnp.int32, (N_EXPERTS, N_EXPERTS), 1)
    earlier = jnp.where(e_row < e_col, 1.0, 0.0).astype(BF16)
    seg_start = _dot(slabs.astype(BF16), earlier) * SLAB_ROWS
    pos1 = jnp.sum(hot1 * (seg_start + before1), axis=-1, keepdims=True)
    pos2 = jnp.sum(hot2 * (seg_start + count1 + before2), axis=-1, keepdims=True)
    pos_ref[...] = jnp.where(first, pos1, pos2).astype(jnp.int32)
    slabs_ref[0] = slabs.astype(jnp.int32)


def _router(x, mixed, mod, lw, *, mod_row):
    n = x.shape[0]
    tm = MOE_TOKEN_TILE
    row = lambda i: (i, 0)
    weights = [lw["w_o"], lw["norm_ffn_g"], lw["router"]]
    i32 = jnp.int32
    return pl.pallas_call(
        _router_kernel,
        grid=(n // tm,),
        in_specs=[pl.BlockSpec((tm, D_MODEL), row), pl.BlockSpec((tm, D_MODEL), row),
                  pl.BlockSpec((1, N_MOD, D_MODEL), lambda i: (mod_row(i, tm), 0, 0))]
                 + [_const_spec(w.shape) for w in weights],
        out_specs=[pl.BlockSpec((tm, D_MODEL), row), pl.BlockSpec((tm, D_MODEL), row),
                   pl.BlockSpec((tm, TOP_K), row), pl.BlockSpec((tm, TOP_K), row),
                   pl.BlockSpec((tm, TOP_K), row), pl.BlockSpec((1, 1, N_EXPERTS), lambda i: (i, 0, 0))],
        out_shape=[jax.ShapeDtypeStruct((n, D_MODEL), F32), jax.ShapeDtypeStruct((n, D_MODEL), F32),
                   jax.ShapeDtypeStruct((n, TOP_K), i32), jax.ShapeDtypeStruct((n, TOP_K), F32),
                   jax.ShapeDtypeStruct((n, TOP_K), i32), jax.ShapeDtypeStruct((n // tm, 1, N_EXPERTS), i32)],
        compiler_params=_params(("parallel",)),
        name="router",
    )(x, mixed, mod, *weights)


class _RoutePlan(NamedTuple):
    pos: jax.Array
    dst: jax.Array
    seg_slabs: jax.Array
    seg_local: jax.Array
    seg_global: jax.Array
    tile_expert: jax.Array
    n_used: jax.Array
    gsize: jax.Array
    gend: jax.Array


def _max_row_tiles(n_pairs, tt, tm):
    n_segments = n_pairs // (tt * TOP_K) * N_EXPERTS
    return (n_pairs + n_segments * (SLAB_ROWS - 1) + N_EXPERTS * (tm - 1)) // tm


def _route_plan(top_i, pos, slabs, tm):
    i32 = jnp.int32
    n = top_i.shape[0]
    n_tt = slabs.shape[0]
    tt = n // n_tt
    seg = slabs.reshape(n_tt, N_EXPERTS) * SLAB_ROWS
    seg_local = jnp.cumsum(seg, axis=1) - seg
    gsize = (jnp.sum(seg, axis=0) + tm - 1) // tm * tm
    gend = jnp.cumsum(gsize)
    seg_global = (gend - gsize)[None, :] + jnp.cumsum(seg, axis=0) - seg
    chosen = top_i.reshape(n_tt, tt, TOP_K)[..., None] == jnp.arange(N_EXPERTS, dtype=i32)
    shift = jnp.sum(jnp.where(chosen, (seg_global - seg_local)[:, None, None, :], 0), axis=-1)
    dst = (pos.reshape(n_tt, tt, TOP_K) + shift).reshape(-1)
    pos = pos.T
    n_tiles = _max_row_tiles(n * TOP_K, tt, tm)
    n_used = gend[-1] // tm
    tile_start = jnp.minimum(jnp.arange(n_tiles, dtype=i32), n_used - 1) * tm
    tile_expert = jnp.sum((gend[None, :] <= tile_start[:, None]).astype(i32), axis=1)
    flat = lambda a: a.reshape(-1).astype(i32)
    return _RoutePlan(pos.astype(i32), dst.astype(i32), flat(seg // SLAB_ROWS), flat(seg_local),
                      flat(seg_global), tile_expert.astype(i32), n_used.reshape(1).astype(i32),
                      gsize.astype(i32), gend.astype(i32))


def _dispatch_kernel(slabs_ref, local_ref, global_ref, gsize_ref, gend_ref, nu_ref,
                     pos_ref, h_ref, xs_ref, sorted_ref, zero_ref, sem):
    i = pl.program_id(0)
    tm = zero_ref.shape[0]
    n_sorted, tt = sorted_ref.shape[0], h_ref.shape[0]

    @pl.when(i == 0)
    def _zero_unused():
        zero_ref[...] = jnp.zeros_like(zero_ref)

        def zero_tile(start):
            cp = pltpu.make_async_copy(zero_ref, xs_ref.at[pl.ds(pl.multiple_of(start, tm), tm)], sem)
            cp.start()
            cp.wait()

        for e in range(N_EXPERTS):
            @pl.when(gsize_ref[e] > 0)
            def _():
                zero_tile(gend_ref[e] - tm)

        def unused(t, carry):
            zero_tile(t * tm)
            return carry

        lax.fori_loop(nu_ref[0], xs_ref.shape[0] // tm, unused, 0)

    rows = lax.broadcasted_iota(jnp.int32, (n_sorted, tt), 0)
    hit = (rows == pos_ref[0:1, :]) | (rows == pos_ref[1:2, :])
    sorted_ref[...] = _dot(jnp.where(hit, 1.0, 0.0).astype(BF16), h_ref[...].astype(BF16))

    def slab_copy(src_row, dst_row):
        return pltpu.make_async_copy(
            sorted_ref.at[pl.ds(pl.multiple_of(src_row, SLAB_ROWS), SLAB_ROWS)],
            xs_ref.at[pl.ds(pl.multiple_of(dst_row, SLAB_ROWS), SLAB_ROWS)], sem)

    for e in range(N_EXPERTS):
        s = i * N_EXPERTS + e

        def issue(j, carry, s=s):
            slab_copy(local_ref[s] + j * SLAB_ROWS, global_ref[s] + j * SLAB_ROWS).start()
            return carry

        lax.fori_loop(0, slabs_ref[s], issue, 0)

    for e in range(N_EXPERTS):
        def drain(j, carry):
            slab_copy(0, 0).wait()
            return carry

        lax.fori_loop(0, slabs_ref[i * N_EXPERTS + e], drain, 0)


def _dispatch(h, plan, tt, tm):
    n = h.shape[0]
    n_rows = _max_row_tiles(n * TOP_K, tt, tm) * tm
    n_sorted = tt * TOP_K + N_EXPERTS * SLAB_ROWS
    return pl.pallas_call(
        _dispatch_kernel,
        grid_spec=pltpu.PrefetchScalarGridSpec(
            num_scalar_prefetch=6,
            grid=(n // tt,),
            in_specs=[pl.BlockSpec((TOP_K, tt), lambda i, *_: (0, i)),
                      pl.BlockSpec((tt, D_MODEL), lambda i, *_: (i, 0))],
            out_specs=pl.BlockSpec(memory_space=pl.ANY),
            scratch_shapes=[pltpu.VMEM((n_sorted, D_MODEL), F32), pltpu.VMEM((tm, D_MODEL), F32),
                            pltpu.SemaphoreType.DMA(())],
        ),
        out_shape=jax.ShapeDtypeStruct((n_rows, D_MODEL), F32),
        compiler_params=_params(("arbitrary",)),
        name="moe_dispatch",
    )(plan.seg_slabs, plan.seg_local, plan.seg_global, plan.gsize, plan.gend, plan.n_used, plan.pos, h)


def _expert_ffn_kernel(te_ref, nu_ref, xs_ref, wg_ref, wu_ref, wd_ref, ys_ref, wg_b, wu_b, wd_b):
    i = pl.program_id(0)

    @pl.when((i == 0) | (te_ref[i] != te_ref[jnp.maximum(i - 1, 0)]))
    def _new_expert():
        wg_b[...] = wg_ref[0].astype(BF16)
        wu_b[...] = wu_ref[0].astype(BF16)
        wd_b[...] = wd_ref[0].astype(BF16)

    @pl.when(i < nu_ref[0])
    def _compute():
        xb = xs_ref[...].astype(BF16)
        y = jnp.zeros(ys_ref.shape, F32)
        for start, size in EXPERT_FF_CHUNKS:
            sl = slice(start, start + size)
            act = _silu(_dot(xb, wg_b[:, sl])) * _dot(xb, wu_b[:, sl])
            y = y + _dot(act.astype(BF16), wd_b[sl, :])
        ys_ref[...] = y

    @pl.when(i >= nu_ref[0])
    def _unused():
        ys_ref[...] = jnp.zeros_like(ys_ref)


def _expert_ffn(xs, plan, lw, tm):
    tile_expert, n_used = plan.tile_expert, plan.n_used
    n_tiles = xs.shape[0] // tm
    wspec = lambda w: pl.BlockSpec((1,) + w.shape[1:], lambda i, te, nu: (te[i], 0, 0))
    return pl.pallas_call(
        _expert_ffn_kernel,
        grid_spec=pltpu.PrefetchScalarGridSpec(
            num_scalar_prefetch=2,
            grid=(n_tiles,),
            in_specs=[pl.BlockSpec((tm, D_MODEL), lambda i, te, nu: (jnp.minimum(i, nu[0] - 1), 0)),
                      wspec(lw["moe_w_gate"]), wspec(lw["moe_w_up"]), wspec(lw["moe_w_down"])],
            out_specs=pl.BlockSpec((tm, D_MODEL), lambda i, te, nu: (i, 0)),
            scratch_shapes=[pltpu.VMEM((D_MODEL, D_FF_EXPERT), BF16), pltpu.VMEM((D_MODEL, D_FF_EXPERT), BF16),
                            pltpu.VMEM((D_FF_EXPERT, D_MODEL), BF16)],
        ),
        out_shape=jax.ShapeDtypeStruct(xs.shape, F32),
        compiler_params=_params(("arbitrary",)),
        name="expert_ffn",
    )(tile_expert, n_used, xs, lw["moe_w_gate"], lw["moe_w_up"], lw["moe_w_down"])


def _combine_kernel(dst_ref, x1_ref, topg_ref, mod_ref, fg_ref, ys_ref, out_ref, buf, sem):
    i = pl.program_id(0)
    n = pl.num_programs(0)
    tm = x1_ref.shape[0]

    def issue(tile, slot):
        base = tile * (tm * TOP_K)

        def body(r, carry):
            for k in range(TOP_K):
                d = dst_ref[base + TOP_K * r + k]
                pltpu.make_async_copy(ys_ref.at[pl.ds(d, 1)], buf.at[slot, k, pl.ds(r, 1)],
                                      sem.at[slot]).start()
            return carry

        lax.fori_loop(0, tm, body, 0, unroll=8)

    @pl.when(i == 0)
    def _first():
        issue(0, 0)

    @pl.when(i + 1 < n)
    def _ahead():
        issue(i + 1, (i + 1) % 2)

    slot = i % 2
    for k in range(TOP_K):
        pltpu.make_async_copy(ys_ref.at[pl.ds(0, tm)], buf.at[slot, k], sem.at[slot]).wait()
    g = topg_ref[...]
    f = g[:, 0:1] * buf[slot, 0] + g[:, 1:2] * buf[slot, 1]
    out_ref[...] = _rms(x1_ref[...] + mod_ref[0, 5:6, :] * f, fg_ref[...])


def _combine(x1, top_g, mod, final_g, ys, plan, *, mod_row):
    dst = plan.dst
    n = x1.shape[0]
    tm = TOKEN_TILE
    row = lambda i, *_: (i, 0)
    return pl.pallas_call(
        _combine_kernel,
        grid_spec=pltpu.PrefetchScalarGridSpec(
            num_scalar_prefetch=1,
            grid=(n // tm,),
            in_specs=[pl.BlockSpec((tm, D_MODEL), row), pl.BlockSpec((tm, TOP_K), row),
                      pl.BlockSpec((1, N_MOD, D_MODEL), lambda i, *_: (mod_row(i, tm), 0, 0)),
                      pl.BlockSpec(final_g.shape, lambda i, *_: (0, 0)),
                      pl.BlockSpec(memory_space=pl.ANY)],
            out_specs=pl.BlockSpec((tm, D_MODEL), row),
            scratch_shapes=[pltpu.VMEM((2, TOP_K, tm, D_MODEL), F32), pltpu.SemaphoreType.DMA((2,))],
        ),
        out_shape=jax.ShapeDtypeStruct((n, D_MODEL), F32),
        compiler_params=_params(("arbitrary",)),
        name="moe_combine",
    )(dst, x1, top_g, mod, final_g, ys)


def _sparse_moe(x, mixed, mod, lw, final_g, *, mod_row):
    x1, h, top_i, top_g, pos, slabs = _router(x, mixed, mod, lw, mod_row=mod_row)
    plan = _route_plan(top_i, pos, slabs, MOE_ROW_TILE)
    xs = _dispatch(h, plan, MOE_TOKEN_TILE, MOE_ROW_TILE)
    ys = _expert_ffn(xs, plan, lw, MOE_ROW_TILE)
    return _combine(x1, top_g, mod, final_g, ys, plan, mod_row=mod_row)


def _head_slots(w, used):
    k = w.shape[0]
    w = w.reshape(k, N_HEADS, used)
    return jnp.pad(w, ((0, 0), (0, 0), (0, HEAD_W - used))).reshape(k, N_HEADS * HEAD_W)


def _layer_weights(l, P):
    row = lambda v: v.reshape(1, -1)
    place = jnp.zeros((HEAD_W, N_HEADS, HEAD_W), F32)
    r = jnp.arange(MLA_ROPE_DIM)
    place = place.at[r, :, MLA_NOPE_DIM + r].set(1.0).reshape(HEAD_W, N_HEADS * HEAD_W)
    lw = {
        "norm_mix_g": row(P["norm_mix_g"][l]),
        "norm_ffn_g": row(P["norm_ffn_g"][l]),
        "w_in": jnp.pad(P["w_in"][l], ((0, 0), (0, W_IN_PAD - O_END))).astype(BF16),
        "q_norm_g": row(P["mla_q_norm_g"][l]),
        "kv_norm_g": row(P["mla_kv_norm_g"][l]),
        "w_uq": _head_slots(P["w_uq"][l], MLA_NOPE_DIM + MLA_ROPE_DIM).astype(BF16),
        "w_kc": _head_slots(P["w_uk"][l], MLA_NOPE_DIM).astype(BF16),
        "w_kr": place.astype(BF16),
        "w_kr_rows": place[:MLA_ROPE_DIM].astype(BF16),
        "w_uv": P["w_uv"][l].astype(BF16),
        "lq1": row(P["diff_lq1"][l]), "lk1": row(P["diff_lk1"][l]),
        "lq2": row(P["diff_lq2"][l]), "lk2": row(P["diff_lk2"][l]),
        "subln_g": row(P["diff_subln_g"][l]),
        "w_o": P["w_o"][l].astype(BF16),
    }
    if l % 2 == 0:
        i = l // 2
        lw.update(w_gate=P["w_gate"][i].astype(BF16), w_up=P["w_up"][i].astype(BF16),
                  w_down=P["w_down"][i].astype(BF16))
    else:
        m = l // 2
        lw.update(router=P["router"][m], moe_w_gate=P["moe_w_gate"][m],
                  moe_w_up=P["moe_w_up"][m], moe_w_down=P["moe_w_down"][m])
    return lw


def _run_group(x, mods, weights, final_g, *, batch, seq, mod_row, tables, caches, cache_out):
    assert DEPTH % 2 == 0
    own = []
    for l in range(DEPTH):
        lw = weights[l]
        if not cache_out:
            outs = _pre_mixer(x, mods[l], lw, tables, seq=seq, mod_row=mod_row)
        elif l < DEPTH - 1:
            outs = _pre_mixer(x, mods[l], lw, tables, seq=seq, mod_row=mod_row, cache_out="flat")
            own.append(outs[6:])
        else:
            outs = _pre_mixer(x, mods[l], lw, tables, seq=seq, mod_row=mod_row, cache_out="stacked",
                              prev=own)
            own = outs[6:]
        mixed = _attention(outs[:6], lw, l, batch=batch, seq=seq, cache=caches)
        if l % 2 == 0:
            x = _dense_ffn(x, mixed, mods[l], lw, mod_row=mod_row)
        else:
            assert l == DEPTH - 1
            x = _sparse_moe(x, mixed, mods[l], lw, final_g, mod_row=mod_row)
    return x, own


def kernel(x_prompt, x_sample, cache_diff_k, cache_diff_v, cache_mla_ckv, cache_mla_krope, c, c_ctx, w_ada, b_ada, norm_mix_g, norm_ffn_g, w_in, mla_q_norm_g, mla_kv_norm_g, w_uq, w_uk, w_uv, diff_lq1, diff_lk1, diff_lq2, diff_lk2, diff_subln_g, w_o, w_gate, w_up, w_down, router, moe_w_gate, moe_w_up, moe_w_down, final_norm_g):
    P = dict(norm_mix_g=norm_mix_g, norm_ffn_g=norm_ffn_g, w_in=w_in, mla_q_norm_g=mla_q_norm_g,
             mla_kv_norm_g=mla_kv_norm_g, w_uq=w_uq, w_uk=w_uk, w_uv=w_uv, diff_lq1=diff_lq1,
             diff_lk1=diff_lk1, diff_lq2=diff_lq2, diff_lk2=diff_lk2, diff_subln_g=diff_subln_g,
             w_o=w_o, w_gate=w_gate, w_up=w_up, w_down=w_down, router=router,
             moe_w_gate=moe_w_gate, moe_w_up=moe_w_up, moe_w_down=moe_w_down)
    bp, sp, d = x_prompt.shape
    bs, ss, _ = x_sample.shape
    n_past = cache_diff_k.shape[2]
    w4 = N_HEADS * HEAD_W

    cond = jnp.zeros((COND_ROWS, d), F32).at[0].set(c_ctx).at[1:1 + bs].set(c)
    mod_all = _ada_table(cond, w_ada, b_ada).reshape(DEPTH, COND_ROWS, N_MOD, d)
    mods = [mod_all[l] for l in range(DEPTH)]
    weights = [_layer_weights(l, P) for l in range(DEPTH)]
    final_g = final_norm_g.reshape(1, d)

    yp, own = _run_group(
        x_prompt.reshape(bp * sp, d), mods, weights, final_g, batch=bp, seq=sp,
        mod_row=lambda i, tm: 0, tables=None, caches=None, cache_out=True)
    y_prompt = yp.reshape(bp, sp, d)
    new_diff_k, new_diff_v, new_mla_ckv, new_mla_krope = own

    caches = (cache_diff_k.reshape(bs, DEPTH, n_past, w4), cache_diff_v.reshape(bs, DEPTH, n_past, w4),
              cache_mla_ckv, cache_mla_krope)
    ys, _ = _run_group(
        x_sample.reshape(bs * ss, d), mods, weights, final_g, batch=bs, seq=ss,
        mod_row=lambda i, tm: 1 + (i * tm) // ss,
        tables=_rope_tables(ss), caches=caches, cache_out=False)
    y_sample = ys.reshape(bs, ss, d)

    return (y_prompt, y_sample, new_diff_k, new_diff_v, new_mla_ckv, new_mla_krope)
```

```python
import functools
import math
from typing import NamedTuple

import jax
import jax.numpy as jnp
from jax import lax
from jax.experimental import pallas as pl
from jax.experimental.pallas import tpu as pltpu

F32 = jnp.float32
BF16 = jnp.bfloat16

D_MODEL = 1024
DEPTH = 2
GRID_W = 64
N_HEADS = 4
HEAD_W = 128
DIFF_HEAD_DIM = 64
MLA_NOPE_DIM = 64
MLA_ROPE_DIM = 32
Q_LORA = 256
KV_LORA = 128
D_FF = 2816
FF_CHUNK = 1408
N_EXPERTS = 8
D_FF_EXPERT = 1408
EXPERT_FF_CHUNKS = ((0, 512), (512, 512), (1024, 384))
ROPE_THETA = 10000.0
RMS_EPS = 1e-6
LOG2_E = math.log2(math.e)
N_MOD = 6
COND_ROWS = 16

TOKEN_TILE = 256
ATTN_Q_TILE = 1024
FFN_TOKEN_TILE = 512
MOE_ROW_TILE = 512
MOE_TOKEN_TILE = 512
SLAB_ROWS = 8
TOP_K = 2
VMEM_LIMIT_BYTES = 60 * 1024 * 1024

O_QD, O_KD, O_VD, O_CQ, O_CKV, O_KR, O_END = 0, 512, 1024, 1536, 1792, 1920, 1952
W_IN_PAD = 2048


def _params(sem):
    return pltpu.CompilerParams(dimension_semantics=sem, vmem_limit_bytes=VMEM_LIMIT_BYTES)


def _const_spec(shape):
    nd = len(shape)
    return pl.BlockSpec(shape, lambda *_: (0,) * nd)


def _rms(x, g):
    return x * lax.rsqrt(jnp.mean(x * x, axis=-1, keepdims=True) + RMS_EPS) * g


def _split_bf16(x):
    hi = x.astype(BF16)
    lo = (x - hi.astype(F32)).astype(BF16)
    return hi, lo


def _dot(a, b):
    return jnp.dot(a, b, preferred_element_type=F32)


def _dot3(a, b):
    a_hi, a_lo = _split_bf16(a)
    b_hi, b_lo = _split_bf16(b)
    return _dot(a_hi, b_hi) + _dot(a_hi, b_lo) + _dot(a_lo, b_hi)


def _ada_kernel(cond_ref, w_ref, b_ref, out_ref):
    cond = cond_ref[...]
    s = cond * jax.nn.sigmoid(cond)
    out_ref[0] = _dot3(s, w_ref[0]) + b_ref[0]


def _ada_table(cond, w_ada, b_ada):
    d = D_MODEL
    return pl.pallas_call(
        _ada_kernel,
        grid=(DEPTH, N_MOD),
        in_specs=[
            pl.BlockSpec((COND_ROWS, d), lambda l, j: (0, 0)),
            pl.BlockSpec((1, d, d), lambda l, j: (l, 0, j)),
            pl.BlockSpec((1, 1, d), lambda l, j: (l, 0, j)),
        ],
        out_specs=pl.BlockSpec((1, COND_ROWS, d), lambda l, j: (l, 0, j)),
        out_shape=jax.ShapeDtypeStruct((DEPTH, COND_ROWS, N_MOD * d), F32),
        compiler_params=_params(("arbitrary", "arbitrary")),
        name="ada_table",
    )(cond, w_ada, b_ada.reshape(DEPTH, 1, N_MOD * d))


def _axial_tables(seq, dim):
    half = dim // 4
    freqs = ROPE_THETA ** (-jnp.arange(half, dtype=F32) / half)
    pos = jnp.arange(seq, dtype=jnp.int32)
    rows = (pos // GRID_W).astype(F32)[:, None] * freqs[None, :]
    cols = (pos % GRID_W).astype(F32)[:, None] * freqs[None, :]
    cos = jnp.concatenate([jnp.cos(rows), jnp.cos(rows), jnp.cos(cols), jnp.cos(cols)], axis=-1)
    sin = jnp.concatenate([-jnp.sin(rows), jnp.sin(rows), -jnp.sin(cols), jnp.sin(cols)], axis=-1)
    return cos, sin


def _rope_tables(seq):
    cos64, sin64 = _axial_tables(seq, DIFF_HEAD_DIM)
    cos32, sin32 = _axial_tables(seq, MLA_ROPE_DIM)
    ones = lambda n: jnp.ones((seq, n), F32)
    zeros = lambda n: jnp.zeros((seq, n), F32)
    cos_d = jnp.tile(cos64, (1, 2 * N_HEADS))
    sin_d = jnp.tile(sin64, (1, 2 * N_HEADS))
    pad = HEAD_W - MLA_NOPE_DIM - MLA_ROPE_DIM
    cos_m = jnp.tile(jnp.concatenate([ones(MLA_NOPE_DIM), cos32, ones(pad)], axis=-1), (1, N_HEADS))
    sin_m = jnp.tile(jnp.concatenate([zeros(MLA_NOPE_DIM), sin32, zeros(pad)], axis=-1), (1, N_HEADS))
    cos_r = jnp.concatenate([cos32, ones(HEAD_W - MLA_ROPE_DIM)], axis=-1)
    sin_r = jnp.concatenate([sin32, zeros(HEAD_W - MLA_ROPE_DIM)], axis=-1)
    return cos_d, sin_d, cos_m, sin_m, cos_r, sin_r


def _rope(x, cos, sin, block):
    width = x.shape[-1]
    lane = lax.broadcasted_iota(jnp.int32, x.shape, 1)
    first = (lane % (2 * block)) < block
    partner = jnp.where(first, pltpu.roll(x, width - block, 1), pltpu.roll(x, block, 1))
    return x * cos + partner * sin


def _store_heads(ref, layer, x):
    for hd in range(N_HEADS):
        ref[0, layer, :, hd, :] = x[:, hd * HEAD_W:(hd + 1) * HEAD_W]


def _pre_mixer_kernel(*refs, rope, cache_out, n_prev):
    it = iter(refs)
    x_ref, mod_ref, g_ref, win_ref, qg_ref, kvg_ref, wuq_ref, wkc_ref, wkr_ref, wuv_ref = (
        next(it) for _ in range(10))
    if rope:
        cos_d, sin_d, cos_m, sin_m, cos_r, sin_r = (next(it) for _ in range(6))
    prev = [[next(it) for _ in range(4)] for _ in range(n_prev)]
    qd_ref, kd_ref, vd_ref, qm_ref, km_ref, vm_ref = (next(it) for _ in range(6))
    if cache_out:
        kd32_ref, vd32_ref, ckv32_ref, kr32_ref = (next(it) for _ in range(4))

    x = x_ref[...]
    h = _rms(x, g_ref[...]) * (1.0 + mod_ref[0, 1:2, :]) + mod_ref[0, 0:1, :]
    z = _dot(h.astype(BF16), win_ref[...])

    qd = z[:, O_QD:O_KD]
    kd = z[:, O_KD:O_VD]
    vd = z[:, O_VD:O_CQ]
    cq = z[:, O_CQ:O_CKV]
    ckv = z[:, O_CKV:O_KR]
    kr = z[:, O_KR:W_IN_PAD]

    qm = _dot(_rms(cq, qg_ref[...]).astype(BF16), wuq_ref[...])
    ckv = _rms(ckv, kvg_ref[...])
    if rope:
        qd = _rope(qd, cos_d[...], sin_d[...], DIFF_HEAD_DIM // 4)
        kd = _rope(kd, cos_d[...], sin_d[...], DIFF_HEAD_DIM // 4)
        qm = _rope(qm, cos_m[...], sin_m[...], MLA_ROPE_DIM // 4)
        kr = _rope(kr, cos_r[...], sin_r[...], MLA_ROPE_DIM // 4)

    ckv_b = ckv.astype(BF16)
    qd_ref[...] = (qd * (LOG2_E * DIFF_HEAD_DIM ** -0.5)).astype(BF16)
    kd_ref[...] = kd.astype(BF16)
    vd_ref[...] = vd.astype(BF16)
    qm_ref[...] = (qm * (LOG2_E * (MLA_NOPE_DIM + MLA_ROPE_DIM) ** -0.5)).astype(BF16)
    km_ref[...] = (_dot(ckv_b, wkc_ref[...]) + _dot(kr.astype(BF16), wkr_ref[...])).astype(BF16)
    vm_ref[...] = _dot(ckv_b, wuv_ref[...]).astype(BF16)
    if cache_out == "flat":
        kd32_ref[...] = kd
        vd32_ref[...] = vd
        ckv32_ref[...] = ckv
        kr32_ref[...] = kr[:, :MLA_ROPE_DIM]
    elif cache_out == "stacked":
        for l, (pk, pv, pc, pr) in enumerate(prev):
            _store_heads(kd32_ref, l, pk[...])
            _store_heads(vd32_ref, l, pv[...])
            ckv32_ref[0, l] = pc[...]
            kr32_ref[0, l] = pr[...]
        _store_heads(kd32_ref, n_prev, kd)
        _store_heads(vd32_ref, n_prev, vd)
        ckv32_ref[0, n_prev] = ckv
        kr32_ref[0, n_prev] = kr[:, :MLA_ROPE_DIM]


def _pre_mixer(x, mod, lw, tables, *, seq, mod_row, cache_out=None, prev=()):
    n = x.shape[0]
    tm = TOKEN_TILE
    tiles_per_seq = seq // tm
    rope = tables is not None
    row = lambda i: (i, 0)
    args = [x, mod, lw["norm_mix_g"], lw["w_in"], lw["q_norm_g"], lw["kv_norm_g"],
            lw["w_uq"], lw["w_kc"], lw["w_kr"], lw["w_uv"]]
    specs = [pl.BlockSpec((tm, D_MODEL), row),
             pl.BlockSpec((1, N_MOD, D_MODEL), lambda i: (mod_row(i, tm), 0, 0))]
    specs += [_const_spec(a.shape) for a in args[2:]]
    if rope:
        for t in tables:
            args.append(t)
            specs.append(pl.BlockSpec((tm, t.shape[1]), lambda i: (i % tiles_per_seq, 0)))
    for layer_rows in prev:
        for a in layer_rows:
            args.append(a)
            specs.append(pl.BlockSpec((tm, a.shape[1]), row))
    w4 = N_HEADS * HEAD_W
    out_shape = [jax.ShapeDtypeStruct((n, w4), BF16)] * 6
    out_specs = [pl.BlockSpec((tm, w4), row)] * 6
    if cache_out == "flat":
        for w in (w4, w4, KV_LORA, MLA_ROPE_DIM):
            out_shape.append(jax.ShapeDtypeStruct((n, w), F32))
            out_specs.append(pl.BlockSpec((tm, w), row))
    elif cache_out == "stacked":
        n_layers = len(prev) + 1
        for tail in ((N_HEADS, HEAD_W), (N_HEADS, HEAD_W), (KV_LORA,), (MLA_ROPE_DIM,)):
            zeros = (0,) * len(tail)
            out_shape.append(jax.ShapeDtypeStruct((n // seq, n_layers, seq) + tail, F32))
            out_specs.append(pl.BlockSpec(
                (1, n_layers, tm) + tail,
                lambda i, zeros=zeros: (i // tiles_per_seq, 0, i % tiles_per_seq) + zeros))
    return pl.pallas_call(
        functools.partial(_pre_mixer_kernel, rope=rope, cache_out=cache_out, n_prev=len(prev)),
        grid=(n // tm,),
        in_specs=specs,
        out_specs=out_specs,
        out_shape=out_shape,
        compiler_params=_params(("parallel",)),
        name="pre_mixer",
    )(*args)


def _softmax_terms(s):
    m = jnp.max(s, axis=-1, keepdims=True)
    e = jnp.exp2(s - m)
    return e, 1.0 / jnp.sum(e, axis=-1, keepdims=True)


def _nt_dot(a, b):
    return lax.dot_general(a, b, (((1,), (1,)), ((), ())), preferred_element_type=F32)


def _attention_kernel(*refs, with_cache, lam_init):
    it = iter(refs)
    qd_ref, qm_ref, kd_ref, vd_ref, km_ref, vm_ref = (next(it) for _ in range(6))
    lq1, lk1, lq2, lk2, subg_ref = (next(it) for _ in range(5))
    if with_cache:
        ckd_ref, cvd_ref, cckv_ref, ckr_ref, wkc_ref, wkr_ref, wuv_ref = (next(it) for _ in range(7))
    out_ref = next(it)
    if with_cache:
        kd_all, vd_all, km_all, vm_all = (next(it) for _ in range(4))
        n_cache = cckv_ref.shape[2]

        @pl.when(pl.program_id(1) == 0)
        def _fill():
            kd_all[:n_cache, :] = ckd_ref[0, 0].astype(BF16)
            vd_all[:n_cache, :] = cvd_ref[0, 0].astype(BF16)
            cckv = cckv_ref[0, 0].astype(BF16)
            km_all[:n_cache, :] = (_dot(cckv, wkc_ref[...])
                                   + _dot(ckr_ref[0, 0].astype(BF16), wkr_ref[...])).astype(BF16)
            vm_all[:n_cache, :] = _dot(cckv, wuv_ref[...]).astype(BF16)
            kd_all[n_cache:, :] = kd_ref[...]
            vd_all[n_cache:, :] = vd_ref[...]
            km_all[n_cache:, :] = km_ref[...]
            vm_all[n_cache:, :] = vm_ref[...]
    else:
        kd_all, vd_all, km_all, vm_all = kd_ref, vd_ref, km_ref, vm_ref

    lam = (jnp.exp(jnp.sum(lq1[...] * lk1[...], axis=-1, keepdims=True))
           - jnp.exp(jnp.sum(lq2[...] * lk2[...], axis=-1, keepdims=True)) + lam_init)
    subg = subg_ref[...]
    tq = qd_ref.shape[0]
    lane = lax.broadcasted_iota(jnp.int32, (tq, HEAD_W), 1)
    first = lane < DIFF_HEAD_DIM

    for hd in range(N_HEADS):
        sl = slice(hd * HEAD_W, (hd + 1) * HEAD_W)
        q = qd_ref[:, sl]
        k = kd_all[:, sl]
        zero = jnp.zeros_like(q)
        e1, r1 = _softmax_terms(_nt_dot(jnp.where(first, q, zero), k))
        e2, r2 = _softmax_terms(_nt_dot(jnp.where(first, zero, q), k))
        p = (e1 * r1 - e2 * (lam * r2)).astype(BF16)
        o = _dot(p, vd_all[:, sl])
        out_ref[:, sl] = (_rms(o, subg) * (1.0 - lam_init)).astype(BF16)

    for hd in range(N_HEADS):
        sl = slice(hd * HEAD_W, (hd + 1) * HEAD_W)
        e, r = _softmax_terms(_nt_dot(qm_ref[:, sl], km_all[:, sl]))
        o = _dot(e.astype(BF16), vm_all[:, sl]) * r
        out_ref[:, N_HEADS * HEAD_W + hd * HEAD_W:N_HEADS * HEAD_W + (hd + 1) * HEAD_W] = o.astype(BF16)


def _attention(qkv, lw, layer, *, batch, seq, cache):
    qd, kd, vd, qm, km, vm = qkv
    w4 = N_HEADS * HEAD_W
    tq = min(ATTN_Q_TILE, seq)
    nq = seq // tq
    lam_init = 0.8 - 0.6 * math.exp(-0.3 * layer)
    q_spec = pl.BlockSpec((tq, w4), lambda b, j: (b * nq + j, 0))
    kv_spec = pl.BlockSpec((seq, w4), lambda b, j: (b, 0))
    args = [qd, qm, kd, vd, km, vm, lw["lq1"], lw["lk1"], lw["lq2"], lw["lk2"], lw["subln_g"]]
    specs = [q_spec, q_spec, kv_spec, kv_spec, kv_spec, kv_spec] + [_const_spec(a.shape) for a in args[6:]]
    scratch = []
    if cache is not None:
        ckd, cvd, cckv, ckr = cache
        n_cache = ckd.shape[2]
        for a in (ckd, cvd, cckv, ckr):
            args.append(a)
            specs.append(pl.BlockSpec((1, 1) + a.shape[2:], lambda b, j: (b, layer, 0, 0)))
        for name in ("w_kc", "w_kr_rows", "w_uv"):
            args.append(lw[name])
            specs.append(_const_spec(lw[name].shape))
        scratch = [pltpu.VMEM((n_cache + seq, w4), BF16)] * 4
    return pl.pallas_call(
        functools.partial(_attention_kernel, with_cache=cache is not None, lam_init=lam_init),
        grid=(batch, nq),
        in_specs=specs,
        out_specs=pl.BlockSpec((tq, 2 * w4), lambda b, j: (b * nq + j, 0)),
        out_shape=jax.ShapeDtypeStruct((batch * seq, 2 * w4), BF16),
        scratch_shapes=scratch,
        compiler_params=_params(("parallel", "arbitrary")),
        name="attention",
    )(*args)


def _post_mixer(x_ref, mixed_ref, mod_ref, wo_ref, g_ref):
    x1 = x_ref[...] + mod_ref[0, 2:3, :] * _dot(mixed_ref[...], wo_ref[...])
    h = _rms(x1, g_ref[...]) * (1.0 + mod_ref[0, 4:5, :]) + mod_ref[0, 3:4, :]
    return x1, h


def _silu(g):
    return g * jax.nn.sigmoid(g)


def _dense_ffn_kernel(x_ref, mixed_ref, mod_ref, wo_ref, g_ref, wg_ref, wu_ref, wd_ref, out_ref):
    x1, h = _post_mixer(x_ref, mixed_ref, mod_ref, wo_ref, g_ref)
    hb = h.astype(BF16)
    acc = jnp.zeros_like(x1)
    for c in range(D_FF // FF_CHUNK):
        sl = slice(c * FF_CHUNK, (c + 1) * FF_CHUNK)
        act = _silu(_dot(hb, wg_ref[:, sl])) * _dot(hb, wu_ref[:, sl])
        acc = acc + _dot(act.astype(BF16), wd_ref[sl, :])
    out_ref[...] = x1 + mod_ref[0, 5:6, :] * acc


def _dense_ffn(x, mixed, mod, lw, *, mod_row):
    n = x.shape[0]
    tm = FFN_TOKEN_TILE
    row = lambda i: (i, 0)
    weights = [lw["w_o"], lw["norm_ffn_g"], lw["w_gate"], lw["w_up"], lw["w_down"]]
    wspecs = [pl.BlockSpec(w.shape, lambda i: (0, 0), pipeline_mode=pl.Buffered(1)) for w in weights]
    return pl.pallas_call(
        _dense_ffn_kernel,
        grid=(n // tm,),
        in_specs=[pl.BlockSpec((tm, D_MODEL), row), pl.BlockSpec((tm, D_MODEL), row),
                  pl.BlockSpec((1, N_MOD, D_MODEL), lambda i: (mod_row(i, tm), 0, 0))] + wspecs,
        out_specs=pl.BlockSpec((tm, D_MODEL), row),
        out_shape=jax.ShapeDtypeStruct((n, D_MODEL), F32),
        compiler_params=_params(("parallel",)),
        name="dense_ffn",
    )(x, mixed, mod, *weights)


def _router_kernel(x_ref, mixed_ref, mod_ref, wo_ref, g_ref, router_ref,
                   x1_ref, h_ref, topi_ref, topg_ref, pos_ref, slabs_ref):
    x1, h = _post_mixer(x_ref, mixed_ref, mod_ref, wo_ref, g_ref)
    x1_ref[...] = x1
    h_ref[...] = h
    logits = _dot3(h, router_ref[...])
    ex = jnp.exp(logits - jnp.max(logits, axis=-1, keepdims=True))
    probs = ex / jnp.sum(ex, axis=-1, keepdims=True)
    idx = lax.broadcasted_iota(jnp.int32, probs.shape, 1)
    p1 = jnp.max(probs, axis=-1, keepdims=True)
    i1 = jnp.min(jnp.where(probs == p1, idx, N_EXPERTS), axis=-1, keepdims=True)
    rest = jnp.where(idx == i1, -1.0, probs)
    p2 = jnp.max(rest, axis=-1, keepdims=True)
    i2 = jnp.min(jnp.where(rest == p2, idx, N_EXPERTS), axis=-1, keepdims=True)
    den = p1 + p2
    first = lax.broadcasted_iota(jnp.int32, topi_ref.shape, 1) == 0
    topi_ref[...] = jnp.where(first, i1, i2)
    topg_ref[...] = jnp.where(first, p1 / den, p2 / den)

    tt = idx.shape[0]
    hot1 = jnp.where(idx == i1, 1.0, 0.0)
    hot2 = jnp.where(idx == i2, 1.0, 0.0)
    tri = jnp.where(lax.broadcasted_iota(jnp.int32, (tt, tt), 0) >= lax.broadcasted_iota(jnp.int32, (tt, tt), 1),
                    1.0, 0.0).astype(BF16)
    before1 = _dot(tri, hot1.astype(BF16)) - hot1
    before2 = _dot(tri, hot2.astype(BF16)) - hot2
    count1 = jnp.sum(hot1, axis=0, keepdims=True)
    count2 = jnp.sum(hot2, axis=0, keepdims=True)
    slabs = jnp.floor((count1 + count2 + (SLAB_ROWS - 1)) * (1.0 / SLAB_ROWS))
    e_row = lax.broadcasted_iota(jnp.int32, (N_EXPERTS, N_EXPERTS), 0)
    e_col = lax.broadcasted_iota(jnp.int32, (N_EXPERTS, N_EXPERTS), 1)
    earlier = jnp.where(e_row < e_col, 1.0, 0.0).astype(BF16)
    seg_start = _dot(slabs.astype(BF16), earlier) * SLAB_ROWS
    pos1 = jnp.sum(hot1 * (seg_start + before1), axis=-1, keepdims=True)
    pos2 = jnp.sum(hot2 * (seg_start + count1 + before2), axis=-1, keepdims=True)
    pos_ref[...] = jnp.where(first, pos1, pos2).astype(jnp.int32)
    slabs_ref[0] = slabs.astype(jnp.int32)


def _router(x, mixed, mod, lw, *, mod_row):
    n = x.shape[0]
    tm = MOE_TOKEN_TILE
    row = lambda i: (i, 0)
    weights = [lw["w_o"], lw["norm_ffn_g"], lw["router"]]
    i32 = jnp.int32
    return pl.pallas_call(
        _router_kernel,
        grid=(n // tm,),
        in_specs=[pl.BlockSpec((tm, D_MODEL), row), pl.BlockSpec((tm, D_MODEL), row),
                  pl.BlockSpec((1, N_MOD, D_MODEL), lambda i: (mod_row(i, tm), 0, 0))]
                 + [_const_spec(w.shape) for w in weights],
        out_specs=[pl.BlockSpec((tm, D_MODEL), row), pl.BlockSpec((tm, D_MODEL), row),
                   pl.BlockSpec((tm, TOP_K), row), pl.BlockSpec((tm, TOP_K), row),
                   pl.BlockSpec((tm, TOP_K), row), pl.BlockSpec((1, 1, N_EXPERTS), lambda i: (i, 0, 0))],
        out_shape=[jax.ShapeDtypeStruct((n, D_MODEL), F32), jax.ShapeDtypeStruct((n, D_MODEL), F32),
                   jax.ShapeDtypeStruct((n, TOP_K), i32), jax.ShapeDtypeStruct((n, TOP_K), F32),
                   jax.ShapeDtypeStruct((n, TOP_K), i32), jax.ShapeDtypeStruct((n // tm, 1, N_EXPERTS), i32)],
        compiler_params=_params(("parallel",)),
        name="router",
    )(x, mixed, mod, *weights)


class _RoutePlan(NamedTuple):
    pos: jax.Array
    dst: jax.Array
    seg_slabs: jax.Array
    seg_local: jax.Array
    seg_global: jax.Array
    tile_expert: jax.Array
    n_used: jax.Array
    gsize: jax.Array
    gend: jax.Array


def _max_row_tiles(n_pairs, tt, tm):
    n_segments = n_pairs // (tt * TOP_K) * N_EXPERTS
    return (n_pairs + n_segments * (SLAB_ROWS - 1) + N_EXPERTS * (tm - 1)) // tm


def _route_plan(top_i, pos, slabs, tm):
    i32 = jnp.int32
    n = top_i.shape[0]
    n_tt = slabs.shape[0]
    tt = n // n_tt
    seg = slabs.reshape(n_tt, N_EXPERTS) * SLAB_ROWS
    seg_local = jnp.cumsum(seg, axis=1) - seg
    gsize = (jnp.sum(seg, axis=0) + tm - 1) // tm * tm
    gend = jnp.cumsum(gsize)
    seg_global = (gend - gsize)[None, :] + jnp.cumsum(seg, axis=0) - seg
    chosen = top_i.reshape(n_tt, tt, TOP_K)[..., None] == jnp.arange(N_EXPERTS, dtype=i32)
    shift = jnp.sum(jnp.where(chosen, (seg_global - seg_local)[:, None, None, :], 0), axis=-1)
    dst = (pos.reshape(n_tt, tt, TOP_K) + shift).reshape(-1)
    pos = pos.T
    n_tiles = _max_row_tiles(n * TOP_K, tt, tm)
    n_used = gend[-1] // tm
    tile_start = jnp.minimum(jnp.arange(n_tiles, dtype=i32), n_used - 1) * tm
    tile_expert = jnp.sum((gend[None, :] <= tile_start[:, None]).astype(i32), axis=1)
    flat = lambda a: a.reshape(-1).astype(i32)
    return _RoutePlan(pos.astype(i32), dst.astype(i32), flat(seg // SLAB_ROWS), flat(seg_local),
                      flat(seg_global), tile_expert.astype(i32), n_used.reshape(1).astype(i32),
                      gsize.astype(i32), gend.astype(i32))


def _dispatch_kernel(slabs_ref, local_ref, global_ref, gsize_ref, gend_ref, nu_ref,
                     pos_ref, h_ref, xs_ref, sorted_ref, zero_ref, sem):
    i = pl.program_id(0)
    tm = zero_ref.shape[0]
    n_sorted, tt = sorted_ref.shape[1], h_ref.shape[0]

    @pl.when(i == 0)
    def _zero_unused():
        zero_ref[...] = jnp.zeros_like(zero_ref)

        def zero_tile(start):
            cp = pltpu.make_async_copy(zero_ref, xs_ref.at[pl.ds(pl.multiple_of(start, tm), tm)], sem.at[0])
            cp.start()
            cp.wait()

        for e in range(N_EXPERTS):
            @pl.when(gsize_ref[e] > 0)
            def _():
                zero_tile(gend_ref[e] - tm)

        def unused(t, carry):
            zero_tile(t * tm)
            return carry

        lax.fori_loop(nu_ref[0], xs_ref.shape[0] // tm, unused, 0)

    slot = i % 2

    def slab_copy(buf, src_row, dst_row):
        return pltpu.make_async_copy(
            sorted_ref.at[buf, pl.ds(pl.multiple_of(src_row, SLAB_ROWS), SLAB_ROWS)],
            xs_ref.at[pl.ds(pl.multiple_of(dst_row, SLAB_ROWS), SLAB_ROWS)], sem.at[buf])

    def drain(step, buf):
        for e in range(N_EXPERTS):
            def wait_one(j, carry):
                slab_copy(buf, 0, 0).wait()
                return carry

            lax.fori_loop(0, slabs_ref[step * N_EXPERTS + e], wait_one, 0)

    @pl.when(i >= 2)
    def _():
        drain(i - 2, slot)

    rows = lax.broadcasted_iota(jnp.int32, (n_sorted, tt), 0)
    hit = (rows == pos_ref[0:1, :]) | (rows == pos_ref[1:2, :])
    sorted_ref[slot] = _dot(jnp.where(hit, 1.0, 0.0).astype(BF16), h_ref[...].astype(BF16))

    for e in range(N_EXPERTS):
        s = i * N_EXPERTS + e

        def issue(j, carry, s=s):
            slab_copy(slot, local_ref[s] + j * SLAB_ROWS, global_ref[s] + j * SLAB_ROWS).start()
            return carry

        lax.fori_loop(0, slabs_ref[s], issue, 0)

    @pl.when(i == pl.num_programs(0) - 1)
    def _():
        @pl.when(i >= 1)
        def _():
            drain(i - 1, 1 - slot)

        drain(i, slot)


def _dispatch(h, plan, tt, tm):
    n = h.shape[0]
    n_rows = _max_row_tiles(n * TOP_K, tt, tm) * tm
    n_sorted = tt * TOP_K + N_EXPERTS * SLAB_ROWS
    return pl.pallas_call(
        _dispatch_kernel,
        grid_spec=pltpu.PrefetchScalarGridSpec(
            num_scalar_prefetch=6,
            grid=(n // tt,),
            in_specs=[pl.BlockSpec((TOP_K, tt), lambda i, *_: (0, i)),
                      pl.BlockSpec((tt, D_MODEL), lambda i, *_: (i, 0))],
            out_specs=pl.BlockSpec(memory_space=pl.ANY),
            scratch_shapes=[pltpu.VMEM((2, n_sorted, D_MODEL), F32), pltpu.VMEM((tm, D_MODEL), F32),
                            pltpu.SemaphoreType.DMA((2,))],
        ),
        out_shape=jax.ShapeDtypeStruct((n_rows, D_MODEL), F32),
        compiler_params=_params(("arbitrary",)),
        name="moe_dispatch",
    )(plan.seg_slabs, plan.seg_local, plan.seg_global, plan.gsize, plan.gend, plan.n_used, plan.pos, h)


def _expert_ffn_kernel(te_ref, nu_ref, xs_ref, wg_ref, wu_ref, wd_ref, ys_ref, wg_b, wu_b, wd_b):
    i = pl.program_id(0)

    @pl.when((i == 0) | (te_ref[i] != te_ref[jnp.maximum(i - 1, 0)]))
    def _new_expert():
        wg_b[...] = wg_ref[0].astype(BF16)
        wu_b[...] = wu_ref[0].astype(BF16)
        wd_b[...] = wd_ref[0].astype(BF16)

    @pl.when(i < nu_ref[0])
    def _compute():
        xb = xs_ref[...].astype(BF16)
        y = jnp.zeros(ys_ref.shape, F32)
        for start, size in EXPERT_FF_CHUNKS:
            sl = slice(start, start + size)
            act = _silu(_dot(xb, wg_b[:, sl])) * _dot(xb, wu_b[:, sl])
            y = y + _dot(act.astype(BF16), wd_b[sl, :])
        ys_ref[...] = y

    @pl.when(i >= nu_ref[0])
    def _unused():
        ys_ref[...] = jnp.zeros_like(ys_ref)


def _expert_ffn(xs, plan, lw, tm):
    tile_expert, n_used = plan.tile_expert, plan.n_used
    n_tiles = xs.shape[0] // tm
    wspec = lambda w: pl.BlockSpec((1,) + w.shape[1:], lambda i, te, nu: (te[i], 0, 0))
    return pl.pallas_call(
        _expert_ffn_kernel,
        grid_spec=pltpu.PrefetchScalarGridSpec(
            num_scalar_prefetch=2,
            grid=(n_tiles,),
            in_specs=[pl.BlockSpec((tm, D_MODEL), lambda i, te, nu: (jnp.minimum(i, nu[0] - 1), 0)),
                      wspec(lw["moe_w_gate"]), wspec(lw["moe_w_up"]), wspec(lw["moe_w_down"])],
            out_specs=pl.BlockSpec((tm, D_MODEL), lambda i, te, nu: (i, 0)),
            scratch_shapes=[pltpu.VMEM((D_MODEL, D_FF_EXPERT), BF16), pltpu.VMEM((D_MODEL, D_FF_EXPERT), BF16),
                            pltpu.VMEM((D_FF_EXPERT, D_MODEL), BF16)],
        ),
        out_shape=jax.ShapeDtypeStruct(xs.shape, F32),
        compiler_params=_params(("arbitrary",)),
        name="expert_ffn",
    )(tile_expert, n_used, xs, lw["moe_w_gate"], lw["moe_w_up"], lw["moe_w_down"])


def _combine_kernel(dst_ref, x1_ref, topg_ref, mod_ref, fg_ref, ys_ref, out_ref, buf, sem):
    i = pl.program_id(0)
    n = pl.num_programs(0)
    tm = x1_ref.shape[0]

    def issue(tile, slot):
        base = tile * (tm * TOP_K)

        def body(r, carry):
            for k in range(TOP_K):
                d = dst_ref[base + TOP_K * r + k]
                pltpu.make_async_copy(ys_ref.at[pl.ds(d, 1)], buf.at[slot, k, pl.ds(r, 1)],
                                      sem.at[slot]).start()
            return carry

        lax.fori_loop(0, tm, body, 0, unroll=8)

    @pl.when(i == 0)
    def _first():
        issue(0, 0)

    @pl.when(i + 1 < n)
    def _ahead():
        issue(i + 1, (i + 1) % 2)

    slot = i % 2
    for k in range(TOP_K):
        pltpu.make_async_copy(ys_ref.at[pl.ds(0, tm)], buf.at[slot, k], sem.at[slot]).wait()
    g = topg_ref[...]
    f = g[:, 0:1] * buf[slot, 0] + g[:, 1:2] * buf[slot, 1]
    out_ref[...] = _rms(x1_ref[...] + mod_ref[0, 5:6, :] * f, fg_ref[...])


def _combine(x1, top_g, mod, final_g, ys, plan, *, mod_row):
    dst = plan.dst
    n = x1.shape[0]
    tm = TOKEN_TILE
    row = lambda i, *_: (i, 0)
    return pl.pallas_call(
        _combine_kernel,
        grid_spec=pltpu.PrefetchScalarGridSpec(
            num_scalar_prefetch=1,
            grid=(n // tm,),
            in_specs=[pl.BlockSpec((tm, D_MODEL), row), pl.BlockSpec((tm, TOP_K), row),
                      pl.BlockSpec((1, N_MOD, D_MODEL), lambda i, *_: (mod_row(i, tm), 0, 0)),
                      pl.BlockSpec(final_g.shape, lambda i, *_: (0, 0)),
                      pl.BlockSpec(memory_space=pl.ANY)],
            out_specs=pl.BlockSpec((tm, D_MODEL), row),
            scratch_shapes=[pltpu.VMEM((2, TOP_K, tm, D_MODEL), F32), pltpu.SemaphoreType.DMA((2,))],
        ),
        out_shape=jax.ShapeDtypeStruct((n, D_MODEL), F32),
        compiler_params=_params(("arbitrary",)),
        name="moe_combine",
    )(dst, x1, top_g, mod, final_g, ys)


def _sparse_moe(x, mixed, mod, lw, final_g, *, mod_row):
    x1, h, top_i, top_g, pos, slabs = _router(x, mixed, mod, lw, mod_row=mod_row)
    plan = _route_plan(top_i, pos, slabs, MOE_ROW_TILE)
    xs = _dispatch(h, plan, MOE_TOKEN_TILE, MOE_ROW_TILE)
    ys = _expert_ffn(xs, plan, lw, MOE_ROW_TILE)
    return _combine(x1, top_g, mod, final_g, ys, plan, mod_row=mod_row)


def _head_slots(w, used):
    k = w.shape[0]
    w = w.reshape(k, N_HEADS, used)
    return jnp.pad(w, ((0, 0), (0, 0), (0, HEAD_W - used))).reshape(k, N_HEADS * HEAD_W)


def _layer_weights(l, P):
    row = lambda v: v.reshape(1, -1)
    place = jnp.zeros((HEAD_W, N_HEADS, HEAD_W), F32)
    r = jnp.arange(MLA_ROPE_DIM)
    place = place.at[r, :, MLA_NOPE_DIM + r].set(1.0).reshape(HEAD_W, N_HEADS * HEAD_W)
    lw = {
        "norm_mix_g": row(P["norm_mix_g"][l]),
        "norm_ffn_g": row(P["norm_ffn_g"][l]),
        "w_in": jnp.pad(P["w_in"][l], ((0, 0), (0, W_IN_PAD - O_END))).astype(BF16),
        "q_norm_g": row(P["mla_q_norm_g"][l]),
        "kv_norm_g": row(P["mla_kv_norm_g"][l]),
        "w_uq": _head_slots(P["w_uq"][l], MLA_NOPE_DIM + MLA_ROPE_DIM).astype(BF16),
        "w_kc": _head_slots(P["w_uk"][l], MLA_NOPE_DIM).astype(BF16),
        "w_kr": place.astype(BF16),
        "w_kr_rows": place[:MLA_ROPE_DIM].astype(BF16),
        "w_uv": P["w_uv"][l].astype(BF16),
        "lq1": row(P["diff_lq1"][l]), "lk1": row(P["diff_lk1"][l]),
        "lq2": row(P["diff_lq2"][l]), "lk2": row(P["diff_lk2"][l]),
        "subln_g": row(P["diff_subln_g"][l]),
        "w_o": P["w_o"][l].astype(BF16),
    }
    if l % 2 == 0:
        i = l // 2
        lw.update(w_gate=P["w_gate"][i].astype(BF16), w_up=P["w_up"][i].astype(BF16),
                  w_down=P["w_down"][i].astype(BF16))
    else:
        m = l // 2
        lw.update(router=P["router"][m], moe_w_gate=P["moe_w_gate"][m],
                  moe_w_up=P["moe_w_up"][m], moe_w_down=P["moe_w_down"][m])
    return lw


def _run_group(x, mods, weights, final_g, *, batch, seq, mod_row, tables, caches, cache_out):
    assert DEPTH % 2 == 0
    own = []
    for l in range(DEPTH):
        lw = weights[l]
        if not cache_out:
            outs = _pre_mixer(x, mods[l], lw, tables, seq=seq, mod_row=mod_row)
        elif l < DEPTH - 1:
            outs = _pre_mixer(x, mods[l], lw, tables, seq=seq, mod_row=mod_row, cache_out="flat")
            own.append(outs[6:])
        else:
            outs = _pre_mixer(x, mods[l], lw, tables, seq=seq, mod_row=mod_row, cache_out="stacked",
                              prev=own)
            own = outs[6:]
        mixed = _attention(outs[:6], lw, l, batch=batch, seq=seq, cache=caches)
        if l % 2 == 0:
            x = _dense_ffn(x, mixed, mods[l], lw, mod_row=mod_row)
        else:
            assert l == DEPTH - 1
            x = _sparse_moe(x, mixed, mods[l], lw, final_g, mod_row=mod_row)
    return x, own


def kernel(x_prompt, x_sample, cache_diff_k, cache_diff_v, cache_mla_ckv, cache_mla_krope, c, c_ctx, w_ada, b_ada, norm_mix_g, norm_ffn_g, w_in, mla_q_norm_g, mla_kv_norm_g, w_uq, w_uk, w_uv, diff_lq1, diff_lk1, diff_lq2, diff_lk2, diff_subln_g, w_o, w_gate, w_up, w_down, router, moe_w_gate, moe_w_up, moe_w_down, final_norm_g):
    P = dict(norm_mix_g=norm_mix_g, norm_ffn_g=norm_ffn_g, w_in=w_in, mla_q_norm_g=mla_q_norm_g,
             mla_kv_norm_g=mla_kv_norm_g, w_uq=w_uq, w_uk=w_uk, w_uv=w_uv, diff_lq1=diff_lq1,
             diff_lk1=diff_lk1, diff_lq2=diff_lq2, diff_lk2=diff_lk2, diff_subln_g=diff_subln_g,
             w_o=w_o, w_gate=w_gate, w_up=w_up, w_down=w_down, router=router,
             moe_w_gate=moe_w_gate, moe_w_up=moe_w_up, moe_w_down=moe_w_down)
    bp, sp, d = x_prompt.shape
    bs, ss, _ = x_sample.shape
    n_past = cache_diff_k.shape[2]
    w4 = N_HEADS * HEAD_W

    cond = jnp.zeros((COND_ROWS, d), F32).at[0].set(c_ctx).at[1:1 + bs].set(c)
    mod_all = _ada_table(cond, w_ada, b_ada).reshape(DEPTH, COND_ROWS, N_MOD, d)
    mods = [mod_all[l] for l in range(DEPTH)]
    weights = [_layer_weights(l, P) for l in range(DEPTH)]
    final_g = final_norm_g.reshape(1, d)

    yp, own = _run_group(
        x_prompt.reshape(bp * sp, d), mods, weights, final_g, batch=bp, seq=sp,
        mod_row=lambda i, tm: 0, tables=None, caches=None, cache_out=True)
    y_prompt = yp.reshape(bp, sp, d)
    new_diff_k, new_diff_v, new_mla_ckv, new_mla_krope = own

    caches = (cache_diff_k.reshape(bs, DEPTH, n_past, w4), cache_diff_v.reshape(bs, DEPTH, n_past, w4),
              cache_mla_ckv, cache_mla_krope)
    ys, _ = _run_group(
        x_sample.reshape(bs * ss, d), mods, weights, final_g, batch=bs, seq=ss,
        mod_row=lambda i, tm: 1 + (i * tm) // ss,
        tables=_rope_tables(ss), caches=caches, cache_out=False)
    y_sample = ys.reshape(bs, ss, d)

    return (y_prompt, y_sample, new_diff_k, new_diff_v, new_mla_ckv, new_mla_krope)
```

```python
import functools
import math
from typing import NamedTuple

import jax
import jax.numpy as jnp
from jax import lax
from jax.experimental import pallas as pl
from jax.experimental.pallas import tpu as pltpu

F32 = jnp.float32
BF16 = jnp.bfloat16

D_MODEL = 1024
DEPTH = 2
GRID_W = 64
N_HEADS = 4
HEAD_W = 128
DIFF_HEAD_DIM = 64
MLA_NOPE_DIM = 64
MLA_ROPE_DIM = 32
Q_LORA = 256
KV_LORA = 128
D_FF = 2816
FF_CHUNK = 1408
N_EXPERTS = 8
D_FF_EXPERT = 1408
EXPERT_FF_CHUNKS = ((0, 512), (512, 512), (1024, 384))
ROPE_THETA = 10000.0
RMS_EPS = 1e-6
LOG2_E = math.log2(math.e)
N_MOD = 6
COND_ROWS = 16

TOKEN_TILE = 256
ATTN_Q_TILE = 1024
PRE_MIXER_TILE = 512
FFN_TOKEN_TILE = 512
MOE_ROW_TILE = 512
MOE_TOKEN_TILE = 512
SLAB_ROWS = 8
CHUNK_SLABS = 8
TOP_K = 2
VMEM_LIMIT_BYTES = 60 * 1024 * 1024

O_QD, O_KD, O_VD, O_CQ, O_CKV, O_KR, O_END = 0, 512, 1024, 1536, 1792, 1920, 1952
W_IN_PAD = 2048


def _params(sem):
    return pltpu.CompilerParams(dimension_semantics=sem, vmem_limit_bytes=VMEM_LIMIT_BYTES)


def _const_spec(shape):
    nd = len(shape)
    return pl.BlockSpec(shape, lambda *_: (0,) * nd)


def _rms(x, g):
    return x * lax.rsqrt(jnp.mean(x * x, axis=-1, keepdims=True) + RMS_EPS) * g


def _split_bf16(x):
    hi = x.astype(BF16)
    lo = (x - hi.astype(F32)).astype(BF16)
    return hi, lo


def _dot(a, b):
    return jnp.dot(a, b, preferred_element_type=F32)


def _dot3(a, b):
    a_hi, a_lo = _split_bf16(a)
    b_hi, b_lo = _split_bf16(b)
    return _dot(a_hi, b_hi) + _dot(a_hi, b_lo) + _dot(a_lo, b_hi)


def _ada_kernel(cond_ref, w_ref, b_ref, out_ref):
    cond = cond_ref[...]
    s = cond * jax.nn.sigmoid(cond)
    out_ref[0] = _dot3(s, w_ref[0]) + b_ref[0]


def _ada_table(cond, w_ada, b_ada):
    d = D_MODEL
    return pl.pallas_call(
        _ada_kernel,
        grid=(DEPTH, N_MOD),
        in_specs=[
            pl.BlockSpec((COND_ROWS, d), lambda l, j: (0, 0)),
            pl.BlockSpec((1, d, d), lambda l, j: (l, 0, j)),
            pl.BlockSpec((1, 1, d), lambda l, j: (l, 0, j)),
        ],
        out_specs=pl.BlockSpec((1, COND_ROWS, d), lambda l, j: (l, 0, j)),
        out_shape=jax.ShapeDtypeStruct((DEPTH, COND_ROWS, N_MOD * d), F32),
        compiler_params=_params(("arbitrary", "arbitrary")),
        name="ada_table",
    )(cond, w_ada, b_ada.reshape(DEPTH, 1, N_MOD * d))


def _axial_tables(seq, dim):
    half = dim // 4
    freqs = ROPE_THETA ** (-jnp.arange(half, dtype=F32) / half)
    pos = jnp.arange(seq, dtype=jnp.int32)
    rows = (pos // GRID_W).astype(F32)[:, None] * freqs[None, :]
    cols = (pos % GRID_W).astype(F32)[:, None] * freqs[None, :]
    cos = jnp.concatenate([jnp.cos(rows), jnp.cos(rows), jnp.cos(cols), jnp.cos(cols)], axis=-1)
    sin = jnp.concatenate([-jnp.sin(rows), jnp.sin(rows), -jnp.sin(cols), jnp.sin(cols)], axis=-1)
    return cos, sin


def _rope_tables(seq):
    cos64, sin64 = _axial_tables(seq, DIFF_HEAD_DIM)
    cos32, sin32 = _axial_tables(seq, MLA_ROPE_DIM)
    ones = lambda n: jnp.ones((seq, n), F32)
    zeros = lambda n: jnp.zeros((seq, n), F32)
    cos_d = jnp.tile(cos64, (1, 2 * N_HEADS))
    sin_d = jnp.tile(sin64, (1, 2 * N_HEADS))
    pad = HEAD_W - MLA_NOPE_DIM - MLA_ROPE_DIM
    cos_m = jnp.tile(jnp.concatenate([ones(MLA_NOPE_DIM), cos32, ones(pad)], axis=-1), (1, N_HEADS))
    sin_m = jnp.tile(jnp.concatenate([zeros(MLA_NOPE_DIM), sin32, zeros(pad)], axis=-1), (1, N_HEADS))
    cos_r = jnp.concatenate([cos32, ones(HEAD_W - MLA_ROPE_DIM)], axis=-1)
    sin_r = jnp.concatenate([sin32, zeros(HEAD_W - MLA_ROPE_DIM)], axis=-1)
    return cos_d, sin_d, cos_m, sin_m, cos_r, sin_r


def _rope(x, cos, sin, block):
    width = x.shape[-1]
    lane = lax.broadcasted_iota(jnp.int32, x.shape, 1)
    first = (lane % (2 * block)) < block
    partner = jnp.where(first, pltpu.roll(x, width - block, 1), pltpu.roll(x, block, 1))
    return x * cos + partner * sin


def _store_heads(ref, layer, x):
    for hd in range(N_HEADS):
        ref[0, layer, :, hd, :] = x[:, hd * HEAD_W:(hd + 1) * HEAD_W]


def _pre_mixer_kernel(*refs, rope, cache_out, n_prev):
    it = iter(refs)
    x_ref, mod_ref, g_ref, win_ref, qg_ref, kvg_ref, wuq_ref, wkc_ref, wkr_ref, wuv_ref = (
        next(it) for _ in range(10))
    if rope:
        cos_d, sin_d, cos_m, sin_m, cos_r, sin_r = (next(it) for _ in range(6))
    prev = [[next(it) for _ in range(4)] for _ in range(n_prev)]
    qd_ref, kd_ref, vd_ref, qm_ref, km_ref, vm_ref = (next(it) for _ in range(6))
    if cache_out:
        kd32_ref, vd32_ref, ckv32_ref, kr32_ref = (next(it) for _ in range(4))

    x = x_ref[...]
    h = _rms(x, g_ref[...]) * (1.0 + mod_ref[0, 1:2, :]) + mod_ref[0, 0:1, :]
    z = _dot(h.astype(BF16), win_ref[...])

    qd = z[:, O_QD:O_KD]
    kd = z[:, O_KD:O_VD]
    vd = z[:, O_VD:O_CQ]
    cq = z[:, O_CQ:O_CKV]
    ckv = z[:, O_CKV:O_KR]
    kr = z[:, O_KR:W_IN_PAD]

    qm = _dot(_rms(cq, qg_ref[...]).astype(BF16), wuq_ref[...])
    ckv = _rms(ckv, kvg_ref[...])
    if rope:
        qd = _rope(qd, cos_d[...], sin_d[...], DIFF_HEAD_DIM // 4)
        kd = _rope(kd, cos_d[...], sin_d[...], DIFF_HEAD_DIM // 4)
        qm = _rope(qm, cos_m[...], sin_m[...], MLA_ROPE_DIM // 4)
        kr = _rope(kr, cos_r[...], sin_r[...], MLA_ROPE_DIM // 4)

    ckv_b = ckv.astype(BF16)
    qd_ref[...] = (qd * (LOG2_E * DIFF_HEAD_DIM ** -0.5)).astype(BF16)
    kd_ref[...] = kd.astype(BF16)
    vd_ref[...] = vd.astype(BF16)
    qm_ref[...] = (qm * (LOG2_E * (MLA_NOPE_DIM + MLA_ROPE_DIM) ** -0.5)).astype(BF16)
    km_ref[...] = (_dot(ckv_b, wkc_ref[...]) + _dot(kr.astype(BF16), wkr_ref[...])).astype(BF16)
    vm_ref[...] = _dot(ckv_b, wuv_ref[...]).astype(BF16)
    if cache_out == "flat":
        kd32_ref[...] = kd
        vd32_ref[...] = vd
        ckv32_ref[...] = ckv
        kr32_ref[...] = kr[:, :MLA_ROPE_DIM]
    elif cache_out == "stacked":
        for l, (pk, pv, pc, pr) in enumerate(prev):
            _store_heads(kd32_ref, l, pk[...])
            _store_heads(vd32_ref, l, pv[...])
            ckv32_ref[0, l] = pc[...]
            kr32_ref[0, l] = pr[...]
        _store_heads(kd32_ref, n_prev, kd)
        _store_heads(vd32_ref, n_prev, vd)
        ckv32_ref[0, n_prev] = ckv
        kr32_ref[0, n_prev] = kr[:, :MLA_ROPE_DIM]


def _pre_mixer(x, mod, lw, tables, *, seq, mod_row, cache_out=None, prev=()):
    n = x.shape[0]
    tm = min(PRE_MIXER_TILE, seq)
    tiles_per_seq = seq // tm
    rope = tables is not None
    row = lambda i: (i, 0)
    args = [x, mod, lw["norm_mix_g"], lw["w_in"], lw["q_norm_g"], lw["kv_norm_g"],
            lw["w_uq"], lw["w_kc"], lw["w_kr"], lw["w_uv"]]
    specs = [pl.BlockSpec((tm, D_MODEL), row),
             pl.BlockSpec((1, N_MOD, D_MODEL), lambda i: (mod_row(i, tm), 0, 0))]
    specs += [_const_spec(a.shape) for a in args[2:]]
    if rope:
        for t in tables:
            args.append(t)
            specs.append(pl.BlockSpec((tm, t.shape[1]), lambda i: (i % tiles_per_seq, 0)))
    for layer_rows in prev:
        for a in layer_rows:
            args.append(a)
            specs.append(pl.BlockSpec((tm, a.shape[1]), row))
    w4 = N_HEADS * HEAD_W
    out_shape = [jax.ShapeDtypeStruct((n, w4), BF16)] * 6
    out_specs = [pl.BlockSpec((tm, w4), row)] * 6
    if cache_out == "flat":
        for w in (w4, w4, KV_LORA, MLA_ROPE_DIM):
            out_shape.append(jax.ShapeDtypeStruct((n, w), F32))
            out_specs.append(pl.BlockSpec((tm, w), row))
    elif cache_out == "stacked":
        n_layers = len(prev) + 1
        for tail in ((N_HEADS, HEAD_W), (N_HEADS, HEAD_W), (KV_LORA,), (MLA_ROPE_DIM,)):
            zeros = (0,) * len(tail)
            out_shape.append(jax.ShapeDtypeStruct((n // seq, n_layers, seq) + tail, F32))
            out_specs.append(pl.BlockSpec(
                (1, n_layers, tm) + tail,
                lambda i, zeros=zeros: (i // tiles_per_seq, 0, i % tiles_per_seq) + zeros))
    return pl.pallas_call(
        functools.partial(_pre_mixer_kernel, rope=rope, cache_out=cache_out, n_prev=len(prev)),
        grid=(n // tm,),
        in_specs=specs,
        out_specs=out_specs,
        out_shape=out_shape,
        compiler_params=_params(("parallel",)),
        name="pre_mixer",
    )(*args)


def _softmax_terms(s):
    m = jnp.max(s, axis=-1, keepdims=True)
    e = jnp.exp2(s - m)
    return e, 1.0 / jnp.sum(e, axis=-1, keepdims=True)


def _nt_dot(a, b):
    return lax.dot_general(a, b, (((1,), (1,)), ((), ())), preferred_element_type=F32)


def _attention_kernel(*refs, with_cache, lam_init, n_seqs):
    it = iter(refs)
    qd_ref, qm_ref, kd_ref, vd_ref, km_ref, vm_ref = (next(it) for _ in range(6))
    lq1, lk1, lq2, lk2, subg_ref = (next(it) for _ in range(5))
    if with_cache:
        ckd_ref, cvd_ref, cckv_ref, ckr_ref, wkc_ref, wkr_ref, wuv_ref = (next(it) for _ in range(7))
    out_ref = next(it)
    if with_cache:
        kd_all, vd_all, km_all, vm_all = (next(it) for _ in range(4))
        n_cache = cckv_ref.shape[2]

        @pl.when(pl.program_id(1) == 0)
        def _fill():
            kd_all[:n_cache, :] = ckd_ref[0, 0].astype(BF16)
            vd_all[:n_cache, :] = cvd_ref[0, 0].astype(BF16)
            cckv = cckv_ref[0, 0].astype(BF16)
            km_all[:n_cache, :] = (_dot(cckv, wkc_ref[...])
                                   + _dot(ckr_ref[0, 0].astype(BF16), wkr_ref[...])).astype(BF16)
            vm_all[:n_cache, :] = _dot(cckv, wuv_ref[...]).astype(BF16)
            kd_all[n_cache:, :] = kd_ref[...]
            vd_all[n_cache:, :] = vd_ref[...]
            km_all[n_cache:, :] = km_ref[...]
            vm_all[n_cache:, :] = vm_ref[...]
    else:
        kd_all, vd_all, km_all, vm_all = kd_ref, vd_ref, km_ref, vm_ref

    lam = (jnp.exp(jnp.sum(lq1[...] * lk1[...], axis=-1, keepdims=True))
           - jnp.exp(jnp.sum(lq2[...] * lk2[...], axis=-1, keepdims=True)) + lam_init)
    subg = subg_ref[...]
    tq = qd_ref.shape[0] // n_seqs
    tk = kd_all.shape[0] // n_seqs
    lane = lax.broadcasted_iota(jnp.int32, (tq, HEAD_W), 1)
    first = lane < DIFF_HEAD_DIM

    for s in range(n_seqs):
        qrows = slice(s * tq, (s + 1) * tq)
        krows = slice(s * tk, (s + 1) * tk)
        for hd in range(N_HEADS):
            sl = slice(hd * HEAD_W, (hd + 1) * HEAD_W)
            q = qd_ref[qrows, sl]
            k = kd_all[krows, sl]
            zero = jnp.zeros_like(q)
            e1, r1 = _softmax_terms(_nt_dot(jnp.where(first, q, zero), k))
            e2, r2 = _softmax_terms(_nt_dot(jnp.where(first, zero, q), k))
            p = (e1 * r1 - e2 * (lam * r2)).astype(BF16)
            o = _dot(p, vd_all[krows, sl])
            out_ref[qrows, sl] = (_rms(o, subg) * (1.0 - lam_init)).astype(BF16)

        for hd in range(N_HEADS):
            sl = slice(hd * HEAD_W, (hd + 1) * HEAD_W)
            osl = slice(N_HEADS * HEAD_W + hd * HEAD_W, N_HEADS * HEAD_W + (hd + 1) * HEAD_W)
            e, r = _softmax_terms(_nt_dot(qm_ref[qrows, sl], km_all[krows, sl]))
            o = _dot(e.astype(BF16), vm_all[krows, sl]) * r
            out_ref[qrows, osl] = o.astype(BF16)


def _attention(qkv, lw, layer, *, batch, seq, cache):
    qd, kd, vd, qm, km, vm = qkv
    w4 = N_HEADS * HEAD_W
    tq = min(ATTN_Q_TILE, seq)
    nq = seq // tq
    n_seqs = max(1, ATTN_Q_TILE // seq) if cache is None else 1
    assert batch % n_seqs == 0
    lam_init = 0.8 - 0.6 * math.exp(-0.3 * layer)
    q_spec = pl.BlockSpec((n_seqs * tq, w4), lambda b, j: (b * nq + j, 0))
    kv_spec = pl.BlockSpec((n_seqs * seq, w4), lambda b, j: (b, 0))
    args = [qd, qm, kd, vd, km, vm, lw["lq1"], lw["lk1"], lw["lq2"], lw["lk2"], lw["subln_g"]]
    specs = [q_spec, q_spec, kv_spec, kv_spec, kv_spec, kv_spec] + [_const_spec(a.shape) for a in args[6:]]
    scratch = []
    if cache is not None:
        ckd, cvd, cckv, ckr = cache
        n_cache = ckd.shape[2]
        for a in (ckd, cvd, cckv, ckr):
            args.append(a)
            specs.append(pl.BlockSpec((1, 1) + a.shape[2:], lambda b, j: (b, layer, 0, 0)))
        for name in ("w_kc", "w_kr_rows", "w_uv"):
            args.append(lw[name])
            specs.append(_const_spec(lw[name].shape))
        scratch = [pltpu.VMEM((n_cache + seq, w4), BF16)] * 4
    return pl.pallas_call(
        functools.partial(_attention_kernel, with_cache=cache is not None, lam_init=lam_init, n_seqs=n_seqs),
        grid=(batch // n_seqs, nq),
        in_specs=specs,
        out_specs=pl.BlockSpec((n_seqs * tq, 2 * w4), lambda b, j: (b * nq + j, 0)),
        out_shape=jax.ShapeDtypeStruct((batch * seq, 2 * w4), BF16),
        scratch_shapes=scratch,
        compiler_params=_params(("parallel", "arbitrary")),
        name="attention",
    )(*args)


def _post_mixer(x_ref, mixed_ref, mod_ref, wo_ref, g_ref):
    x1 = x_ref[...] + mod_ref[0, 2:3, :] * _dot(mixed_ref[...], wo_ref[...])
    h = _rms(x1, g_ref[...]) * (1.0 + mod_ref[0, 4:5, :]) + mod_ref[0, 3:4, :]
    return x1, h


def _silu(g):
    return g * jax.nn.sigmoid(g)


def _dense_ffn_kernel(x_ref, mixed_ref, mod_ref, wo_ref, g_ref, wg_ref, wu_ref, wd_ref, out_ref):
    x1, h = _post_mixer(x_ref, mixed_ref, mod_ref, wo_ref, g_ref)
    hb = h.astype(BF16)
    acc = jnp.zeros_like(x1)
    for c in range(D_FF // FF_CHUNK):
        sl = slice(c * FF_CHUNK, (c + 1) * FF_CHUNK)
        act = _silu(_dot(hb, wg_ref[:, sl])) * _dot(hb, wu_ref[:, sl])
        acc = acc + _dot(act.astype(BF16), wd_ref[sl, :])
    out_ref[...] = x1 + mod_ref[0, 5:6, :] * acc


def _dense_ffn(x, mixed, mod, lw, *, mod_row):
    n = x.shape[0]
    tm = FFN_TOKEN_TILE
    row = lambda i: (i, 0)
    weights = [lw["w_o"], lw["norm_ffn_g"], lw["w_gate"], lw["w_up"], lw["w_down"]]
    wspecs = [pl.BlockSpec(w.shape, lambda i: (0, 0), pipeline_mode=pl.Buffered(1)) for w in weights]
    return pl.pallas_call(
        _dense_ffn_kernel,
        grid=(n // tm,),
        in_specs=[pl.BlockSpec((tm, D_MODEL), row), pl.BlockSpec((tm, D_MODEL), row),
                  pl.BlockSpec((1, N_MOD, D_MODEL), lambda i: (mod_row(i, tm), 0, 0))] + wspecs,
        out_specs=pl.BlockSpec((tm, D_MODEL), row),
        out_shape=jax.ShapeDtypeStruct((n, D_MODEL), F32),
        compiler_params=_params(("parallel",)),
        name="dense_ffn",
    )(x, mixed, mod, *weights)


def _router_kernel(x_ref, mixed_ref, mod_ref, wo_ref, g_ref, router_ref,
                   x1_ref, h_ref, topi_ref, topg_ref, pos_ref, slabs_ref):
    x1, h = _post_mixer(x_ref, mixed_ref, mod_ref, wo_ref, g_ref)
    x1_ref[...] = x1
    h_ref[...] = h
    logits = _dot3(h, router_ref[...])
    ex = jnp.exp(logits - jnp.max(logits, axis=-1, keepdims=True))
    probs = ex / jnp.sum(ex, axis=-1, keepdims=True)
    idx = lax.broadcasted_iota(jnp.int32, probs.shape, 1)
    p1 = jnp.max(probs, axis=-1, keepdims=True)
    i1 = jnp.min(jnp.where(probs == p1, idx, N_EXPERTS), axis=-1, keepdims=True)
    rest = jnp.where(idx == i1, -1.0, probs)
    p2 = jnp.max(rest, axis=-1, keepdims=True)
    i2 = jnp.min(jnp.where(rest == p2, idx, N_EXPERTS), axis=-1, keepdims=True)
    den = p1 + p2
    first = lax.broadcasted_iota(jnp.int32, topi_ref.shape, 1) == 0
    topi_ref[...] = jnp.where(first, i1, i2)
    topg_ref[...] = jnp.where(first, p1 / den, p2 / den)

    tt = idx.shape[0]
    hot1 = jnp.where(idx == i1, 1.0, 0.0)
    hot2 = jnp.where(idx == i2, 1.0, 0.0)
    tri = jnp.where(lax.broadcasted_iota(jnp.int32, (tt, tt), 0) >= lax.broadcasted_iota(jnp.int32, (tt, tt), 1),
                    1.0, 0.0).astype(BF16)
    before1 = _dot(tri, hot1.astype(BF16)) - hot1
    before2 = _dot(tri, hot2.astype(BF16)) - hot2
    count1 = jnp.sum(hot1, axis=0, keepdims=True)
    count2 = jnp.sum(hot2, axis=0, keepdims=True)
    slabs = jnp.floor((count1 + count2 + (SLAB_ROWS - 1)) * (1.0 / SLAB_ROWS))
    e_row = lax.broadcasted_iota(jnp.int32, (N_EXPERTS, N_EXPERTS), 0)
    e_col = lax.broadcasted_iota(jnp.int32, (N_EXPERTS, N_EXPERTS), 1)
    earlier = jnp.where(e_row < e_col, 1.0, 0.0).astype(BF16)
    seg_start = _dot(slabs.astype(BF16), earlier) * SLAB_ROWS
    pos1 = jnp.sum(hot1 * (seg_start + before1), axis=-1, keepdims=True)
    pos2 = jnp.sum(hot2 * (seg_start + count1 + before2), axis=-1, keepdims=True)
    pos_ref[...] = jnp.where(first, pos1, pos2).astype(jnp.int32)
    slabs_ref[0] = slabs.astype(jnp.int32)


def _router(x, mixed, mod, lw, *, mod_row):
    n = x.shape[0]
    tm = MOE_TOKEN_TILE
    row = lambda i: (i, 0)
    weights = [lw["w_o"], lw["norm_ffn_g"], lw["router"]]
    i32 = jnp.int32
    return pl.pallas_call(
        _router_kernel,
        grid=(n // tm,),
        in_specs=[pl.BlockSpec((tm, D_MODEL), row), pl.BlockSpec((tm, D_MODEL), row),
                  pl.BlockSpec((1, N_MOD, D_MODEL), lambda i: (mod_row(i, tm), 0, 0))]
                 + [_const_spec(w.shape) for w in weights],
        out_specs=[pl.BlockSpec((tm, D_MODEL), row), pl.BlockSpec((tm, D_MODEL), row),
                   pl.BlockSpec((tm, TOP_K), row), pl.BlockSpec((tm, TOP_K), row),
                   pl.BlockSpec((tm, TOP_K), row), pl.BlockSpec((1, 1, N_EXPERTS), lambda i: (i, 0, 0))],
        out_shape=[jax.ShapeDtypeStruct((n, D_MODEL), F32), jax.ShapeDtypeStruct((n, D_MODEL), F32),
                   jax.ShapeDtypeStruct((n, TOP_K), i32), jax.ShapeDtypeStruct((n, TOP_K), F32),
                   jax.ShapeDtypeStruct((n, TOP_K), i32), jax.ShapeDtypeStruct((n // tm, 1, N_EXPERTS), i32)],
        compiler_params=_params(("parallel",)),
        name="router",
    )(x, mixed, mod, *weights)


class _RoutePlan(NamedTuple):
    pos: jax.Array
    dst: jax.Array
    seg_slabs: jax.Array
    seg_local: jax.Array
    seg_global: jax.Array
    tile_expert: jax.Array
    n_used: jax.Array
    gsize: jax.Array
    gend: jax.Array


def _max_row_tiles(n_pairs, tt, tm):
    n_segments = n_pairs // (tt * TOP_K) * N_EXPERTS
    return (n_pairs + n_segments * (SLAB_ROWS - 1) + N_EXPERTS * (tm - 1)) // tm


def _route_plan(top_i, pos, slabs, tm):
    i32 = jnp.int32
    n = top_i.shape[0]
    n_tt = slabs.shape[0]
    tt = n // n_tt
    seg = slabs.reshape(n_tt, N_EXPERTS) * SLAB_ROWS
    seg_local = jnp.cumsum(seg, axis=1) - seg
    gsize = (jnp.sum(seg, axis=0) + tm - 1) // tm * tm
    gend = jnp.cumsum(gsize)
    seg_global = (gend - gsize)[None, :] + jnp.cumsum(seg, axis=0) - seg
    chosen = top_i.reshape(n_tt, tt, TOP_K)[..., None] == jnp.arange(N_EXPERTS, dtype=i32)
    shift = jnp.sum(jnp.where(chosen, (seg_global - seg_local)[:, None, None, :], 0), axis=-1)
    dst = (pos.reshape(n_tt, tt, TOP_K) + shift).reshape(-1)
    pos = pos.T
    n_tiles = _max_row_tiles(n * TOP_K, tt, tm)
    n_used = gend[-1] // tm
    tile_start = jnp.minimum(jnp.arange(n_tiles, dtype=i32), n_used - 1) * tm
    tile_expert = jnp.sum((gend[None, :] <= tile_start[:, None]).astype(i32), axis=1)
    flat = lambda a: a.reshape(-1).astype(i32)
    return _RoutePlan(pos.astype(i32), dst.astype(i32), flat(seg // SLAB_ROWS), flat(seg_local),
                      flat(seg_global), tile_expert.astype(i32), n_used.reshape(1).astype(i32),
                      gsize.astype(i32), gend.astype(i32))


def _for_segment_copies(n_slabs, fn):
    chunk_rows = CHUNK_SLABS * SLAB_ROWS
    n_chunks = n_slabs // CHUNK_SLABS

    def chunk(j, carry):
        fn(j * chunk_rows, chunk_rows)
        return carry

    def single(j, carry):
        fn(n_chunks * chunk_rows + j * SLAB_ROWS, SLAB_ROWS)
        return carry

    lax.fori_loop(0, n_chunks, chunk, 0)
    lax.fori_loop(0, n_slabs - n_chunks * CHUNK_SLABS, single, 0)


def _dispatch_kernel(slabs_ref, local_ref, global_ref, gsize_ref, gend_ref, nu_ref,
                     pos_ref, h_ref, xs_ref, sorted_ref, zero_ref, sem):
    i = pl.program_id(0)
    tm = zero_ref.shape[0]
    n_sorted, tt = sorted_ref.shape[1], h_ref.shape[0]

    @pl.when(i == 0)
    def _zero_unused():
        zero_ref[...] = jnp.zeros_like(zero_ref)

        def zero_tile(start):
            cp = pltpu.make_async_copy(zero_ref, xs_ref.at[pl.ds(pl.multiple_of(start, tm), tm)], sem.at[0])
            cp.start()
            cp.wait()

        for e in range(N_EXPERTS):
            @pl.when(gsize_ref[e] > 0)
            def _():
                zero_tile(gend_ref[e] - tm)

        def unused(t, carry):
            zero_tile(t * tm)
            return carry

        lax.fori_loop(nu_ref[0], xs_ref.shape[0] // tm, unused, 0)

    slot = i % 2

    def seg_copy(buf, src_row, dst_row, rows):
        return pltpu.make_async_copy(
            sorted_ref.at[buf, pl.ds(pl.multiple_of(src_row, SLAB_ROWS), rows)],
            xs_ref.at[pl.ds(pl.multiple_of(dst_row, SLAB_ROWS), rows)], sem.at[buf])

    def drain(step, buf):
        for e in range(N_EXPERTS):
            _for_segment_copies(slabs_ref[step * N_EXPERTS + e],
                                lambda off, rows: seg_copy(buf, 0, 0, rows).wait())

    @pl.when(i >= 2)
    def _():
        drain(i - 2, slot)

    rows = lax.broadcasted_iota(jnp.int32, (n_sorted, tt), 0)
    hit = (rows == pos_ref[0:1, :]) | (rows == pos_ref[1:2, :])
    sorted_ref[slot] = _dot(jnp.where(hit, 1.0, 0.0).astype(BF16), h_ref[...].astype(BF16))

    for e in range(N_EXPERTS):
        s = i * N_EXPERTS + e
        _for_segment_copies(
            slabs_ref[s],
            lambda off, rows, s=s: seg_copy(slot, local_ref[s] + off, global_ref[s] + off, rows).start())

    @pl.when(i == pl.num_programs(0) - 1)
    def _():
        @pl.when(i >= 1)
        def _():
            drain(i - 1, 1 - slot)

        drain(i, slot)


def _dispatch(h, plan, tt, tm):
    n = h.shape[0]
    n_rows = _max_row_tiles(n * TOP_K, tt, tm) * tm
    n_sorted = tt * TOP_K + N_EXPERTS * SLAB_ROWS
    return pl.pallas_call(
        _dispatch_kernel,
        grid_spec=pltpu.PrefetchScalarGridSpec(
            num_scalar_prefetch=6,
            grid=(n // tt,),
            in_specs=[pl.BlockSpec((TOP_K, tt), lambda i, *_: (0, i)),
                      pl.BlockSpec((tt, D_MODEL), lambda i, *_: (i, 0))],
            out_specs=pl.BlockSpec(memory_space=pl.ANY),
            scratch_shapes=[pltpu.VMEM((2, n_sorted, D_MODEL), F32), pltpu.VMEM((tm, D_MODEL), F32),
                            pltpu.SemaphoreType.DMA((2,))],
        ),
        out_shape=jax.ShapeDtypeStruct((n_rows, D_MODEL), F32),
        compiler_params=_params(("arbitrary",)),
        name="moe_dispatch",
    )(plan.seg_slabs, plan.seg_local, plan.seg_global, plan.gsize, plan.gend, plan.n_used, plan.pos, h)


def _expert_ffn_kernel(te_ref, nu_ref, xs_ref, wg_ref, wu_ref, wd_ref, ys_ref, wg_b, wu_b, wd_b):
    i = pl.program_id(0)

    @pl.when((i == 0) | (te_ref[i] != te_ref[jnp.maximum(i - 1, 0)]))
    def _new_expert():
        wg_b[...] = wg_ref[0].astype(BF16)
        wu_b[...] = wu_ref[0].astype(BF16)
        wd_b[...] = wd_ref[0].astype(BF16)

    @pl.when(i < nu_ref[0])
    def _compute():
        xb = xs_ref[...].astype(BF16)
        y = jnp.zeros(ys_ref.shape, F32)
        for start, size in EXPERT_FF_CHUNKS:
            sl = slice(start, start + size)
            act = _silu(_dot(xb, wg_b[:, sl])) * _dot(xb, wu_b[:, sl])
            y = y + _dot(act.astype(BF16), wd_b[sl, :])
        ys_ref[...] = y

    @pl.when(i >= nu_ref[0])
    def _unused():
        ys_ref[...] = jnp.zeros_like(ys_ref)


def _expert_ffn(xs, plan, lw, tm):
    tile_expert, n_used = plan.tile_expert, plan.n_used
    n_tiles = xs.shape[0] // tm
    wspec = lambda w: pl.BlockSpec((1,) + w.shape[1:], lambda i, te, nu: (te[i], 0, 0))
    return pl.pallas_call(
        _expert_ffn_kernel,
        grid_spec=pltpu.PrefetchScalarGridSpec(
            num_scalar_prefetch=2,
            grid=(n_tiles,),
            in_specs=[pl.BlockSpec((tm, D_MODEL), lambda i, te, nu: (jnp.minimum(i, nu[0] - 1), 0)),
                      wspec(lw["moe_w_gate"]), wspec(lw["moe_w_up"]), wspec(lw["moe_w_down"])],
            out_specs=pl.BlockSpec((tm, D_MODEL), lambda i, te, nu: (i, 0)),
            scratch_shapes=[pltpu.VMEM((D_MODEL, D_FF_EXPERT), BF16), pltpu.VMEM((D_MODEL, D_FF_EXPERT), BF16),
                            pltpu.VMEM((D_FF_EXPERT, D_MODEL), BF16)],
        ),
        out_shape=jax.ShapeDtypeStruct(xs.shape, F32),
        compiler_params=_params(("arbitrary",)),
        name="expert_ffn",
    )(tile_expert, n_used, xs, lw["moe_w_gate"], lw["moe_w_up"], lw["moe_w_down"])


def _combine_kernel(dst_ref, x1_ref, topg_ref, mod_ref, fg_ref, ys_ref, out_ref, buf, sem):
    i = pl.program_id(0)
    n = pl.num_programs(0)
    tm = x1_ref.shape[0]

    def issue(tile, slot):
        base = tile * (tm * TOP_K)

        def body(r, carry):
            for k in range(TOP_K):
                d = dst_ref[base + TOP_K * r + k]
                pltpu.make_async_copy(ys_ref.at[pl.ds(d, 1)], buf.at[slot, k, pl.ds(r, 1)],
                                      sem.at[slot]).start()
            return carry

        lax.fori_loop(0, tm, body, 0, unroll=8)

    @pl.when(i == 0)
    def _first():
        issue(0, 0)

    @pl.when(i + 1 < n)
    def _ahead():
        issue(i + 1, (i + 1) % 2)

    slot = i % 2
    for k in range(TOP_K):
        pltpu.make_async_copy(ys_ref.at[pl.ds(0, tm)], buf.at[slot, k], sem.at[slot]).wait()
    g = topg_ref[...]
    f = g[:, 0:1] * buf[slot, 0] + g[:, 1:2] * buf[slot, 1]
    out_ref[...] = _rms(x1_ref[...] + mod_ref[0, 5:6, :] * f, fg_ref[...])


def _combine(x1, top_g, mod, final_g, ys, plan, *, mod_row):
    dst = plan.dst
    n = x1.shape[0]
    tm = TOKEN_TILE
    row = lambda i, *_: (i, 0)
    return pl.pallas_call(
        _combine_kernel,
        grid_spec=pltpu.PrefetchScalarGridSpec(
            num_scalar_prefetch=1,
            grid=(n // tm,),
            in_specs=[pl.BlockSpec((tm, D_MODEL), row), pl.BlockSpec((tm, TOP_K), row),
                      pl.BlockSpec((1, N_MOD, D_MODEL), lambda i, *_: (mod_row(i, tm), 0, 0)),
                      pl.BlockSpec(final_g.shape, lambda i, *_: (0, 0)),
                      pl.BlockSpec(memory_space=pl.ANY)],
            out_specs=pl.BlockSpec((tm, D_MODEL), row),
            scratch_shapes=[pltpu.VMEM((2, TOP_K, tm, D_MODEL), F32), pltpu.SemaphoreType.DMA((2,))],
        ),
        out_shape=jax.ShapeDtypeStruct((n, D_MODEL), F32),
        compiler_params=_params(("arbitrary",)),
        name="moe_combine",
    )(dst, x1, top_g, mod, final_g, ys)


def _sparse_moe(x, mixed, mod, lw, final_g, *, mod_row):
    x1, h, top_i, top_g, pos, slabs = _router(x, mixed, mod, lw, mod_row=mod_row)
    plan = _route_plan(top_i, pos, slabs, MOE_ROW_TILE)
    xs = _dispatch(h, plan, MOE_TOKEN_TILE, MOE_ROW_TILE)
    ys = _expert_ffn(xs, plan, lw, MOE_ROW_TILE)
    return _combine(x1, top_g, mod, final_g, ys, plan, mod_row=mod_row)


def _head_slots(w, used):
    k = w.shape[0]
    w = w.reshape(k, N_HEADS, used)
    return jnp.pad(w, ((0, 0), (0, 0), (0, HEAD_W - used))).reshape(k, N_HEADS * HEAD_W)


def _layer_weights(l, P):
    row = lambda v: v.reshape(1, -1)
    place = jnp.zeros((HEAD_W, N_HEADS, HEAD_W), F32)
    r = jnp.arange(MLA_ROPE_DIM)
    place = place.at[r, :, MLA_NOPE_DIM + r].set(1.0).reshape(HEAD_W, N_HEADS * HEAD_W)
    lw = {
        "norm_mix_g": row(P["norm_mix_g"][l]),
        "norm_ffn_g": row(P["norm_ffn_g"][l]),
        "w_in": jnp.pad(P["w_in"][l], ((0, 0), (0, W_IN_PAD - O_END))).astype(BF16),
        "q_norm_g": row(P["mla_q_norm_g"][l]),
        "kv_norm_g": row(P["mla_kv_norm_g"][l]),
        "w_uq": _head_slots(P["w_uq"][l], MLA_NOPE_DIM + MLA_ROPE_DIM).astype(BF16),
        "w_kc": _head_slots(P["w_uk"][l], MLA_NOPE_DIM).astype(BF16),
        "w_kr": place.astype(BF16),
        "w_kr_rows": place[:MLA_ROPE_DIM].astype(BF16),
        "w_uv": P["w_uv"][l].astype(BF16),
        "lq1": row(P["diff_lq1"][l]), "lk1": row(P["diff_lk1"][l]),
        "lq2": row(P["diff_lq2"][l]), "lk2": row(P["diff_lk2"][l]),
        "subln_g": row(P["diff_subln_g"][l]),
        "w_o": P["w_o"][l].astype(BF16),
    }
    if l % 2 == 0:
        i = l // 2
        lw.update(w_gate=P["w_gate"][i].astype(BF16), w_up=P["w_up"][i].astype(BF16),
                  w_down=P["w_down"][i].astype(BF16))
    else:
        m = l // 2
        lw.update(router=P["router"][m], moe_w_gate=P["moe_w_gate"][m],
                  moe_w_up=P["moe_w_up"][m], moe_w_down=P["moe_w_down"][m])
    return lw


def _run_group(x, mods, weights, final_g, *, batch, seq, mod_row, tables, caches, cache_out):
    assert DEPTH % 2 == 0
    own = []
    for l in range(DEPTH):
        lw = weights[l]
        if not cache_out:
            outs = _pre_mixer(x, mods[l], lw, tables, seq=seq, mod_row=mod_row)
        elif l < DEPTH - 1:
            outs = _pre_mixer(x, mods[l], lw, tables, seq=seq, mod_row=mod_row, cache_out="flat")
            own.append(outs[6:])
        else:
            outs = _pre_mixer(x, mods[l], lw, tables, seq=seq, mod_row=mod_row, cache_out="stacked",
                              prev=own)
            own = outs[6:]
        mixed = _attention(outs[:6], lw, l, batch=batch, seq=seq, cache=caches)
        if l % 2 == 0:
            x = _dense_ffn(x, mixed, mods[l], lw, mod_row=mod_row)
        else:
            assert l == DEPTH - 1
            x = _sparse_moe(x, mixed, mods[l], lw, final_g, mod_row=mod_row)
    return x, own


def kernel(x_prompt, x_sample, cache_diff_k, cache_diff_v, cache_mla_ckv, cache_mla_krope, c, c_ctx, w_ada, b_ada, norm_mix_g, norm_ffn_g, w_in, mla_q_norm_g, mla_kv_norm_g, w_uq, w_uk, w_uv, diff_lq1, diff_lk1, diff_lq2, diff_lk2, diff_subln_g, w_o, w_gate, w_up, w_down, router, moe_w_gate, moe_w_up, moe_w_down, final_norm_g):
    P = dict(norm_mix_g=norm_mix_g, norm_ffn_g=norm_ffn_g, w_in=w_in, mla_q_norm_g=mla_q_norm_g,
             mla_kv_norm_g=mla_kv_norm_g, w_uq=w_uq, w_uk=w_uk, w_uv=w_uv, diff_lq1=diff_lq1,
             diff_lk1=diff_lk1, diff_lq2=diff_lq2, diff_lk2=diff_lk2, diff_subln_g=diff_subln_g,
             w_o=w_o, w_gate=w_gate, w_up=w_up, w_down=w_down, router=router,
             moe_w_gate=moe_w_gate, moe_w_up=moe_w_up, moe_w_down=moe_w_down)
    bp, sp, d = x_prompt.shape
    bs, ss, _ = x_sample.shape
    n_past = cache_diff_k.shape[2]
    w4 = N_HEADS * HEAD_W

    cond = jnp.zeros((COND_ROWS, d), F32).at[0].set(c_ctx).at[1:1 + bs].set(c)
    mod_all = _ada_table(cond, w_ada, b_ada).reshape(DEPTH, COND_ROWS, N_MOD, d)
    mods = [mod_all[l] for l in range(DEPTH)]
    weights = [_layer_weights(l, P) for l in range(DEPTH)]
    final_g = final_norm_g.reshape(1, d)

    yp, own = _run_group(
        x_prompt.reshape(bp * sp, d), mods, weights, final_g, batch=bp, seq=sp,
        mod_row=lambda i, tm: 0, tables=None, caches=None, cache_out=True)
    y_prompt = yp.reshape(bp, sp, d)
    new_diff_k, new_diff_v, new_mla_ckv, new_mla_krope = own

    caches = (cache_diff_k.reshape(bs, DEPTH, n_past, w4), cache_diff_v.reshape(bs, DEPTH, n_past, w4),
              cache_mla_ckv, cache_mla_krope)
    ys, _ = _run_group(
        x_sample.reshape(bs * ss, d), mods, weights, final_g, batch=bs, seq=ss,
        mod_row=lambda i, tm: 1 + (i * tm) // ss,
        tables=_rope_tables(ss), caches=caches, cache_out=False)
    y_sample = ys.reshape(bs, ss, d)

    return (y_prompt, y_sample, new_diff_k, new_diff_v, new_mla_ckv, new_mla_krope)
```

```python
import functools
import math
from typing import NamedTuple

import jax
import jax.numpy as jnp
from jax import lax
from jax.experimental import pallas as pl
from jax.experimental.pallas import tpu as pltpu

F32 = jnp.float32
BF16 = jnp.bfloat16

D_MODEL = 1024
DEPTH = 2
GRID_W = 64
N_HEADS = 4
HEAD_W = 128
DIFF_HEAD_DIM = 64
MLA_NOPE_DIM = 64
MLA_ROPE_DIM = 32
Q_LORA = 256
KV_LORA = 128
D_FF = 2816
FF_CHUNK = 1408
N_EXPERTS = 8
D_FF_EXPERT = 1408
EXPERT_FF_CHUNKS = ((0, 512), (512, 512), (1024, 384))
ROPE_THETA = 10000.0
RMS_EPS = 1e-6
LOG2_E = math.log2(math.e)
N_MOD = 6
COND_ROWS = 16

TOKEN_TILE = 256
ATTN_Q_TILE = 1024
PRE_MIXER_TILE = 512
FFN_TOKEN_TILE = 512
MOE_ROW_TILE = 512
MOE_TOKEN_TILE = 512
SLAB_ROWS = 8
CHUNK_SLABS = 8
TOP_K = 2
VMEM_LIMIT_BYTES = 60 * 1024 * 1024

O_QD, O_KD, O_VD, O_CQ, O_CKV, O_KR, O_END = 0, 512, 1024, 1536, 1792, 1920, 1952
W_IN_PAD = 2048


def _params(sem):
    return pltpu.CompilerParams(dimension_semantics=sem, vmem_limit_bytes=VMEM_LIMIT_BYTES)


def _const_spec(shape):
    nd = len(shape)
    return pl.BlockSpec(shape, lambda *_: (0,) * nd)


def _rms(x, g):
    return x * lax.rsqrt(jnp.mean(x * x, axis=-1, keepdims=True) + RMS_EPS) * g


def _split_bf16(x):
    hi = x.astype(BF16)
    lo = (x - hi.astype(F32)).astype(BF16)
    return hi, lo


def _dot(a, b):
    return jnp.dot(a, b, preferred_element_type=F32)


def _dot3(a, b):
    a_hi, a_lo = _split_bf16(a)
    b_hi, b_lo = _split_bf16(b)
    return _dot(a_hi, b_hi) + _dot(a_hi, b_lo) + _dot(a_lo, b_hi)


def _ada_kernel(cond_ref, w_ref, b_ref, out_ref):
    cond = cond_ref[...]
    s = cond * jax.nn.sigmoid(cond)
    out_ref[0] = _dot3(s, w_ref[0]) + b_ref[0]


def _ada_table(cond, w_ada, b_ada):
    d = D_MODEL
    return pl.pallas_call(
        _ada_kernel,
        grid=(DEPTH, N_MOD),
        in_specs=[
            pl.BlockSpec((COND_ROWS, d), lambda l, j: (0, 0)),
            pl.BlockSpec((1, d, d), lambda l, j: (l, 0, j)),
            pl.BlockSpec((1, 1, d), lambda l, j: (l, 0, j)),
        ],
        out_specs=pl.BlockSpec((1, COND_ROWS, d), lambda l, j: (l, 0, j)),
        out_shape=jax.ShapeDtypeStruct((DEPTH, COND_ROWS, N_MOD * d), F32),
        compiler_params=_params(("arbitrary", "arbitrary")),
        name="ada_table",
    )(cond, w_ada, b_ada.reshape(DEPTH, 1, N_MOD * d))


def _axial_tables(seq, dim):
    half = dim // 4
    freqs = ROPE_THETA ** (-jnp.arange(half, dtype=F32) / half)
    pos = jnp.arange(seq, dtype=jnp.int32)
    rows = (pos // GRID_W).astype(F32)[:, None] * freqs[None, :]
    cols = (pos % GRID_W).astype(F32)[:, None] * freqs[None, :]
    cos = jnp.concatenate([jnp.cos(rows), jnp.cos(rows), jnp.cos(cols), jnp.cos(cols)], axis=-1)
    sin = jnp.concatenate([-jnp.sin(rows), jnp.sin(rows), -jnp.sin(cols), jnp.sin(cols)], axis=-1)
    return cos, sin


def _rope_tables(seq):
    cos64, sin64 = _axial_tables(seq, DIFF_HEAD_DIM)
    cos32, sin32 = _axial_tables(seq, MLA_ROPE_DIM)
    ones = lambda n: jnp.ones((seq, n), F32)
    zeros = lambda n: jnp.zeros((seq, n), F32)
    cos_d = jnp.tile(cos64, (1, 2 * N_HEADS))
    sin_d = jnp.tile(sin64, (1, 2 * N_HEADS))
    pad = HEAD_W - MLA_NOPE_DIM - MLA_ROPE_DIM
    cos_m = jnp.tile(jnp.concatenate([ones(MLA_NOPE_DIM), cos32, ones(pad)], axis=-1), (1, N_HEADS))
    sin_m = jnp.tile(jnp.concatenate([zeros(MLA_NOPE_DIM), sin32, zeros(pad)], axis=-1), (1, N_HEADS))
    cos_r = jnp.concatenate([cos32, ones(HEAD_W - MLA_ROPE_DIM)], axis=-1)
    sin_r = jnp.concatenate([sin32, zeros(HEAD_W - MLA_ROPE_DIM)], axis=-1)
    return cos_d, sin_d, cos_m, sin_m, cos_r, sin_r


def _rope(x, cos, sin, block):
    width = x.shape[-1]
    lane = lax.broadcasted_iota(jnp.int32, x.shape, 1)
    first = (lane % (2 * block)) < block
    partner = jnp.where(first, pltpu.roll(x, width - block, 1), pltpu.roll(x, block, 1))
    return x * cos + partner * sin


def _store_heads(ref, layer, x):
    for hd in range(N_HEADS):
        ref[0, layer, :, hd, :] = x[:, hd * HEAD_W:(hd + 1) * HEAD_W]


def _pre_mixer_kernel(*refs, rope, cache_out, n_prev):
    it = iter(refs)
    x_ref, mod_ref, g_ref, win_ref, qg_ref, kvg_ref, wuq_ref, wkc_ref, wkr_ref, wuv_ref = (
        next(it) for _ in range(10))
    if rope:
        cos_d, sin_d, cos_m, sin_m, cos_r, sin_r = (next(it) for _ in range(6))
    prev = [[next(it) for _ in range(4)] for _ in range(n_prev)]
    qd_ref, kd_ref, vd_ref, qm_ref, km_ref, vm_ref = (next(it) for _ in range(6))
    if cache_out:
        kd32_ref, vd32_ref, ckv32_ref, kr32_ref = (next(it) for _ in range(4))

    x = x_ref[...]
    h = _rms(x, g_ref[...]) * (1.0 + mod_ref[0, 1:2, :]) + mod_ref[0, 0:1, :]
    z = _dot(h.astype(BF16), win_ref[...])

    qd = z[:, O_QD:O_KD]
    kd = z[:, O_KD:O_VD]
    vd = z[:, O_VD:O_CQ]
    cq = z[:, O_CQ:O_CKV]
    ckv = z[:, O_CKV:O_KR]
    kr = z[:, O_KR:W_IN_PAD]

    qm = _dot(_rms(cq, qg_ref[...]).astype(BF16), wuq_ref[...])
    ckv = _rms(ckv, kvg_ref[...])
    if rope:
        qd = _rope(qd, cos_d[...], sin_d[...], DIFF_HEAD_DIM // 4)
        kd = _rope(kd, cos_d[...], sin_d[...], DIFF_HEAD_DIM // 4)
        qm = _rope(qm, cos_m[...], sin_m[...], MLA_ROPE_DIM // 4)
        kr = _rope(kr, cos_r[...], sin_r[...], MLA_ROPE_DIM // 4)

    ckv_b = ckv.astype(BF16)
    qd_ref[...] = (qd * (LOG2_E * DIFF_HEAD_DIM ** -0.5)).astype(BF16)
    kd_ref[...] = kd.astype(BF16)
    vd_ref[...] = vd.astype(BF16)
    qm_ref[...] = (qm * (LOG2_E * (MLA_NOPE_DIM + MLA_ROPE_DIM) ** -0.5)).astype(BF16)
    km_ref[...] = (_dot(ckv_b, wkc_ref[...]) + _dot(kr.astype(BF16), wkr_ref[...])).astype(BF16)
    vm_ref[...] = _dot(ckv_b, wuv_ref[...]).astype(BF16)
    if cache_out == "flat":
        kd32_ref[...] = kd
        vd32_ref[...] = vd
        ckv32_ref[...] = ckv
        kr32_ref[...] = kr[:, :MLA_ROPE_DIM]
    elif cache_out == "stacked":
        for l, (pk, pv, pc, pr) in enumerate(prev):
            _store_heads(kd32_ref, l, pk[...])
            _store_heads(vd32_ref, l, pv[...])
            ckv32_ref[0, l] = pc[...]
            kr32_ref[0, l] = pr[...]
        _store_heads(kd32_ref, n_prev, kd)
        _store_heads(vd32_ref, n_prev, vd)
        ckv32_ref[0, n_prev] = ckv
        kr32_ref[0, n_prev] = kr[:, :MLA_ROPE_DIM]


def _pre_mixer(x, mod, lw, tables, *, seq, mod_row, cache_out=None, prev=()):
    n = x.shape[0]
    tm = min(PRE_MIXER_TILE, seq)
    tiles_per_seq = seq // tm
    rope = tables is not None
    row = lambda i: (i, 0)
    args = [x, mod, lw["norm_mix_g"], lw["w_in"], lw["q_norm_g"], lw["kv_norm_g"],
            lw["w_uq"], lw["w_kc"], lw["w_kr"], lw["w_uv"]]
    specs = [pl.BlockSpec((tm, D_MODEL), row),
             pl.BlockSpec((1, N_MOD, D_MODEL), lambda i: (mod_row(i, tm), 0, 0))]
    specs += [_const_spec(a.shape) for a in args[2:]]
    if rope:
        for t in tables:
            args.append(t)
            specs.append(pl.BlockSpec((tm, t.shape[1]), lambda i: (i % tiles_per_seq, 0)))
    for layer_rows in prev:
        for a in layer_rows:
            args.append(a)
            specs.append(pl.BlockSpec((tm, a.shape[1]), row))
    w4 = N_HEADS * HEAD_W
    out_shape = [jax.ShapeDtypeStruct((n, w4), BF16)] * 6
    out_specs = [pl.BlockSpec((tm, w4), row)] * 6
    if cache_out == "flat":
        for w in (w4, w4, KV_LORA, MLA_ROPE_DIM):
            out_shape.append(jax.ShapeDtypeStruct((n, w), F32))
            out_specs.append(pl.BlockSpec((tm, w), row))
    elif cache_out == "stacked":
        n_layers = len(prev) + 1
        for tail in ((N_HEADS, HEAD_W), (N_HEADS, HEAD_W), (KV_LORA,), (MLA_ROPE_DIM,)):
            zeros = (0,) * len(tail)
            out_shape.append(jax.ShapeDtypeStruct((n // seq, n_layers, seq) + tail, F32))
            out_specs.append(pl.BlockSpec(
                (1, n_layers, tm) + tail,
                lambda i, zeros=zeros: (i // tiles_per_seq, 0, i % tiles_per_seq) + zeros))
    return pl.pallas_call(
        functools.partial(_pre_mixer_kernel, rope=rope, cache_out=cache_out, n_prev=len(prev)),
        grid=(n // tm,),
        in_specs=specs,
        out_specs=out_specs,
        out_shape=out_shape,
        compiler_params=_params(("parallel",)),
        name="pre_mixer",
    )(*args)


def _softmax_terms(s):
    m = jnp.max(s, axis=-1, keepdims=True)
    e = jnp.exp2(s - m)
    return e, 1.0 / jnp.sum(e, axis=-1, keepdims=True)


def _nt_dot(a, b):
    return lax.dot_general(a, b, (((1,), (1,)), ((), ())), preferred_element_type=F32)


def _attention_kernel(*refs, with_cache, lam_init, n_seqs):
    it = iter(refs)
    qd_ref, qm_ref, kd_ref, vd_ref, km_ref, vm_ref = (next(it) for _ in range(6))
    lq1, lk1, lq2, lk2, subg_ref = (next(it) for _ in range(5))
    if with_cache:
        ckd_ref, cvd_ref, cckv_ref, ckr_ref, wkc_ref, wkr_ref, wuv_ref = (next(it) for _ in range(7))
    out_ref = next(it)
    if with_cache:
        kd_all, vd_all, km_all, vm_all = (next(it) for _ in range(4))
        n_cache = cckv_ref.shape[2]

        @pl.when(pl.program_id(1) == 0)
        def _fill():
            kd_all[:n_cache, :] = ckd_ref[0, 0].astype(BF16)
            vd_all[:n_cache, :] = cvd_ref[0, 0].astype(BF16)
            cckv = cckv_ref[0, 0].astype(BF16)
            km_all[:n_cache, :] = (_dot(cckv, wkc_ref[...])
                                   + _dot(ckr_ref[0, 0].astype(BF16), wkr_ref[...])).astype(BF16)
            vm_all[:n_cache, :] = _dot(cckv, wuv_ref[...]).astype(BF16)
            kd_all[n_cache:, :] = kd_ref[...]
            vd_all[n_cache:, :] = vd_ref[...]
            km_all[n_cache:, :] = km_ref[...]
            vm_all[n_cache:, :] = vm_ref[...]
    else:
        kd_all, vd_all, km_all, vm_all = kd_ref, vd_ref, km_ref, vm_ref

    lam = (jnp.exp(jnp.sum(lq1[...] * lk1[...], axis=-1, keepdims=True))
           - jnp.exp(jnp.sum(lq2[...] * lk2[...], axis=-1, keepdims=True)) + lam_init)
    subg = subg_ref[...]
    tq = qd_ref.shape[0] // n_seqs
    tk = kd_all.shape[0] // n_seqs
    lane = lax.broadcasted_iota(jnp.int32, (tq, HEAD_W), 1)
    first = lane < DIFF_HEAD_DIM

    for s in range(n_seqs):
        qrows = slice(s * tq, (s + 1) * tq)
        krows = slice(s * tk, (s + 1) * tk)
        for hd in range(N_HEADS):
            sl = slice(hd * HEAD_W, (hd + 1) * HEAD_W)
            q = qd_ref[qrows, sl]
            k = kd_all[krows, sl]
            zero = jnp.zeros_like(q)
            e1, r1 = _softmax_terms(_nt_dot(jnp.where(first, q, zero), k))
            e2, r2 = _softmax_terms(_nt_dot(jnp.where(first, zero, q), k))
            p = (e1 * r1 - e2 * (lam * r2)).astype(BF16)
            o = _dot(p, vd_all[krows, sl])
            out_ref[qrows, sl] = (_rms(o, subg) * (1.0 - lam_init)).astype(BF16)

        for hd in range(N_HEADS):
            sl = slice(hd * HEAD_W, (hd + 1) * HEAD_W)
            osl = slice(N_HEADS * HEAD_W + hd * HEAD_W, N_HEADS * HEAD_W + (hd + 1) * HEAD_W)
            e, r = _softmax_terms(_nt_dot(qm_ref[qrows, sl], km_all[krows, sl]))
            o = _dot(e.astype(BF16), vm_all[krows, sl]) * r
            out_ref[qrows, osl] = o.astype(BF16)


def _attention(qkv, lw, layer, *, batch, seq, cache):
    qd, kd, vd, qm, km, vm = qkv
    w4 = N_HEADS * HEAD_W
    tq = min(ATTN_Q_TILE, seq)
    nq = seq // tq
    n_seqs = max(1, ATTN_Q_TILE // seq) if cache is None else 1
    assert batch % n_seqs == 0
    lam_init = 0.8 - 0.6 * math.exp(-0.3 * layer)
    q_spec = pl.BlockSpec((n_seqs * tq, w4), lambda b, j: (b * nq + j, 0))
    kv_spec = pl.BlockSpec((n_seqs * seq, w4), lambda b, j: (b, 0))
    args = [qd, qm, kd, vd, km, vm, lw["lq1"], lw["lk1"], lw["lq2"], lw["lk2"], lw["subln_g"]]
    specs = [q_spec, q_spec, kv_spec, kv_spec, kv_spec, kv_spec] + [_const_spec(a.shape) for a in args[6:]]
    scratch = []
    if cache is not None:
        ckd, cvd, cckv, ckr = cache
        n_cache = ckd.shape[2]
        for a in (ckd, cvd, cckv, ckr):
            args.append(a)
            specs.append(pl.BlockSpec((1, 1) + a.shape[2:], lambda b, j: (b, layer, 0, 0)))
        for name in ("w_kc", "w_kr_rows", "w_uv"):
            args.append(lw[name])
            specs.append(_const_spec(lw[name].shape))
        scratch = [pltpu.VMEM((n_cache + seq, w4), BF16)] * 4
    return pl.pallas_call(
        functools.partial(_attention_kernel, with_cache=cache is not None, lam_init=lam_init, n_seqs=n_seqs),
        grid=(batch // n_seqs, nq),
        in_specs=specs,
        out_specs=pl.BlockSpec((n_seqs * tq, 2 * w4), lambda b, j: (b * nq + j, 0)),
        out_shape=jax.ShapeDtypeStruct((batch * seq, 2 * w4), BF16),
        scratch_shapes=scratch,
        compiler_params=_params(("parallel", "arbitrary")),
        name="attention",
    )(*args)


def _post_mixer(x_ref, mixed_ref, mod_ref, wo_ref, g_ref):
    x1 = x_ref[...] + mod_ref[0, 2:3, :] * _dot(mixed_ref[...], wo_ref[...])
    h = _rms(x1, g_ref[...]) * (1.0 + mod_ref[0, 4:5, :]) + mod_ref[0, 3:4, :]
    return x1, h


def _silu(g):
    return g * jax.nn.sigmoid(g)


def _dense_ffn_kernel(x_ref, mixed_ref, mod_ref, wo_ref, g_ref, wg_ref, wu_ref, wd_ref, out_ref):
    x1, h = _post_mixer(x_ref, mixed_ref, mod_ref, wo_ref, g_ref)
    hb = h.astype(BF16)
    acc = jnp.zeros_like(x1)
    for c in range(D_FF // FF_CHUNK):
        sl = slice(c * FF_CHUNK, (c + 1) * FF_CHUNK)
        act = _silu(_dot(hb, wg_ref[:, sl])) * _dot(hb, wu_ref[:, sl])
        acc = acc + _dot(act.astype(BF16), wd_ref[sl, :])
    out_ref[...] = x1 + mod_ref[0, 5:6, :] * acc


def _dense_ffn(x, mixed, mod, lw, *, mod_row):
    n = x.shape[0]
    tm = FFN_TOKEN_TILE
    row = lambda i: (i, 0)
    weights = [lw["w_o"], lw["norm_ffn_g"], lw["w_gate"], lw["w_up"], lw["w_down"]]
    wspecs = [pl.BlockSpec(w.shape, lambda i: (0, 0), pipeline_mode=pl.Buffered(1)) for w in weights]
    return pl.pallas_call(
        _dense_ffn_kernel,
        grid=(n // tm,),
        in_specs=[pl.BlockSpec((tm, D_MODEL), row), pl.BlockSpec((tm, D_MODEL), row),
                  pl.BlockSpec((1, N_MOD, D_MODEL), lambda i: (mod_row(i, tm), 0, 0))] + wspecs,
        out_specs=pl.BlockSpec((tm, D_MODEL), row),
        out_shape=jax.ShapeDtypeStruct((n, D_MODEL), F32),
        compiler_params=_params(("parallel",)),
        name="dense_ffn",
    )(x, mixed, mod, *weights)


def _router_kernel(x_ref, mixed_ref, mod_ref, wo_ref, g_ref, router_ref,
                   x1_ref, h_ref, topi_ref, topg_ref, pos_ref, slabs_ref):
    x1, h = _post_mixer(x_ref, mixed_ref, mod_ref, wo_ref, g_ref)
    x1_ref[...] = x1
    h_ref[...] = h
    logits = _dot3(h, router_ref[...])
    ex = jnp.exp(logits - jnp.max(logits, axis=-1, keepdims=True))
    probs = ex / jnp.sum(ex, axis=-1, keepdims=True)
    idx = lax.broadcasted_iota(jnp.int32, probs.shape, 1)
    p1 = jnp.max(probs, axis=-1, keepdims=True)
    i1 = jnp.min(jnp.where(probs == p1, idx, N_EXPERTS), axis=-1, keepdims=True)
    rest = jnp.where(idx == i1, -1.0, probs)
    p2 = jnp.max(rest, axis=-1, keepdims=True)
    i2 = jnp.min(jnp.where(rest == p2, idx, N_EXPERTS), axis=-1, keepdims=True)
    den = p1 + p2
    first = lax.broadcasted_iota(jnp.int32, topi_ref.shape, 1) == 0
    topi_ref[...] = jnp.where(first, i1, i2)
    topg_ref[...] = jnp.where(first, p1 / den, p2 / den)

    tt = idx.shape[0]
    hot1 = jnp.where(idx == i1, 1.0, 0.0)
    hot2 = jnp.where(idx == i2, 1.0, 0.0)
    tri = jnp.where(lax.broadcasted_iota(jnp.int32, (tt, tt), 0) >= lax.broadcasted_iota(jnp.int32, (tt, tt), 1),
                    1.0, 0.0).astype(BF16)
    before1 = _dot(tri, hot1.astype(BF16)) - hot1
    before2 = _dot(tri, hot2.astype(BF16)) - hot2
    count1 = jnp.sum(hot1, axis=0, keepdims=True)
    count2 = jnp.sum(hot2, axis=0, keepdims=True)
    slabs = jnp.floor((count1 + count2 + (SLAB_ROWS - 1)) * (1.0 / SLAB_ROWS))
    e_row = lax.broadcasted_iota(jnp.int32, (N_EXPERTS, N_EXPERTS), 0)
    e_col = lax.broadcasted_iota(jnp.int32, (N_EXPERTS, N_EXPERTS), 1)
    earlier = jnp.where(e_row < e_col, 1.0, 0.0).astype(BF16)
    seg_start = _dot(slabs.astype(BF16), earlier) * SLAB_ROWS
    pos1 = jnp.sum(hot1 * (seg_start + before1), axis=-1, keepdims=True)
    pos2 = jnp.sum(hot2 * (seg_start + count1 + before2), axis=-1, keepdims=True)
    pos_ref[...] = jnp.where(first, pos1, pos2).astype(jnp.int32)
    slabs_ref[0] = slabs.astype(jnp.int32)


def _router(x, mixed, mod, lw, *, mod_row):
    n = x.shape[0]
    tm = MOE_TOKEN_TILE
    row = lambda i: (i, 0)
    weights = [lw["w_o"], lw["norm_ffn_g"], lw["router"]]
    i32 = jnp.int32
    return pl.pallas_call(
        _router_kernel,
        grid=(n // tm,),
        in_specs=[pl.BlockSpec((tm, D_MODEL), row), pl.BlockSpec((tm, D_MODEL), row),
                  pl.BlockSpec((1, N_MOD, D_MODEL), lambda i: (mod_row(i, tm), 0, 0))]
                 + [_const_spec(w.shape) for w in weights],
        out_specs=[pl.BlockSpec((tm, D_MODEL), row), pl.BlockSpec((tm, D_MODEL), row),
                   pl.BlockSpec((tm, TOP_K), row), pl.BlockSpec((tm, TOP_K), row),
                   pl.BlockSpec((tm, TOP_K), row), pl.BlockSpec((1, 1, N_EXPERTS), lambda i: (i, 0, 0))],
        out_shape=[jax.ShapeDtypeStruct((n, D_MODEL), F32), jax.ShapeDtypeStruct((n, D_MODEL), F32),
                   jax.ShapeDtypeStruct((n, TOP_K), i32), jax.ShapeDtypeStruct((n, TOP_K), F32),
                   jax.ShapeDtypeStruct((n, TOP_K), i32), jax.ShapeDtypeStruct((n // tm, 1, N_EXPERTS), i32)],
        compiler_params=_params(("parallel",)),
        name="router",
    )(x, mixed, mod, *weights)


class _RoutePlan(NamedTuple):
    pos: jax.Array
    dst: jax.Array
    seg_slabs: jax.Array
    seg_local: jax.Array
    seg_global: jax.Array
    tile_expert: jax.Array
    n_used: jax.Array
    gsize: jax.Array
    gend: jax.Array


def _max_row_tiles(n_pairs, tt, tm):
    n_segments = n_pairs // (tt * TOP_K) * N_EXPERTS
    return (n_pairs + n_segments * (SLAB_ROWS - 1) + N_EXPERTS * (tm - 1)) // tm


def _route_plan(top_i, pos, slabs, tm):
    i32 = jnp.int32
    n = top_i.shape[0]
    n_tt = slabs.shape[0]
    tt = n // n_tt
    seg = slabs.reshape(n_tt, N_EXPERTS) * SLAB_ROWS
    seg_local = jnp.cumsum(seg, axis=1) - seg
    gsize = (jnp.sum(seg, axis=0) + tm - 1) // tm * tm
    gend = jnp.cumsum(gsize)
    seg_global = (gend - gsize)[None, :] + jnp.cumsum(seg, axis=0) - seg
    chosen = top_i.reshape(n_tt, tt, TOP_K)[..., None] == jnp.arange(N_EXPERTS, dtype=i32)
    shift = jnp.sum(jnp.where(chosen, (seg_global - seg_local)[:, None, None, :], 0), axis=-1)
    dst = (pos.reshape(n_tt, tt, TOP_K) + shift).reshape(-1)
    pos = pos.T
    n_tiles = _max_row_tiles(n * TOP_K, tt, tm)
    n_used = gend[-1] // tm
    tile_start = jnp.minimum(jnp.arange(n_tiles, dtype=i32), n_used - 1) * tm
    tile_expert = jnp.sum((gend[None, :] <= tile_start[:, None]).astype(i32), axis=1)
    flat = lambda a: a.reshape(-1).astype(i32)
    return _RoutePlan(pos.astype(i32), dst.astype(i32), flat(seg // SLAB_ROWS), flat(seg_local),
                      flat(seg_global), tile_expert.astype(i32), n_used.reshape(1).astype(i32),
                      gsize.astype(i32), gend.astype(i32))


def _for_segment_copies(n_slabs, fn):
    chunk_rows = CHUNK_SLABS * SLAB_ROWS
    n_chunks = n_slabs // CHUNK_SLABS

    def chunk(j, carry):
        fn(j * chunk_rows, chunk_rows)
        return carry

    def single(j, carry):
        fn(n_chunks * chunk_rows + j * SLAB_ROWS, SLAB_ROWS)
        return carry

    lax.fori_loop(0, n_chunks, chunk, 0)
    lax.fori_loop(0, n_slabs - n_chunks * CHUNK_SLABS, single, 0)


def _dispatch_kernel(slabs_ref, local_ref, global_ref, gsize_ref, gend_ref, nu_ref, pos_ref, *rest, group_tiles):
    h_refs = rest[:len(group_tiles)]
    xs_ref, sorted_ref, zero_ref, sem = rest[len(group_tiles):]
    i = pl.program_id(0)
    tm = zero_ref.shape[0]
    n_sorted, tt = sorted_ref.shape[1], h_refs[0].shape[0]

    @pl.when(i == 0)
    def _zero_unused():
        zero_ref[...] = jnp.zeros_like(zero_ref)

        def zero_tile(start):
            cp = pltpu.make_async_copy(zero_ref, xs_ref.at[pl.ds(pl.multiple_of(start, tm), tm)], sem.at[0])
            cp.start()
            cp.wait()

        for e in range(N_EXPERTS):
            @pl.when(gsize_ref[e] > 0)
            def _():
                zero_tile(gend_ref[e] - tm)

        def unused(t, carry):
            zero_tile(t * tm)
            return carry

        lax.fori_loop(nu_ref[0], xs_ref.shape[0] // tm, unused, 0)

    slot = i % 2

    def seg_copy(buf, src_row, dst_row, rows):
        return pltpu.make_async_copy(
            sorted_ref.at[buf, pl.ds(pl.multiple_of(src_row, SLAB_ROWS), rows)],
            xs_ref.at[pl.ds(pl.multiple_of(dst_row, SLAB_ROWS), rows)], sem.at[buf])

    def drain(tile, buf):
        for e in range(N_EXPERTS):
            _for_segment_copies(slabs_ref[tile * N_EXPERTS + e],
                                lambda off, rows: seg_copy(buf, 0, 0, rows).wait())

    @pl.when(i >= 2)
    def _():
        drain(i - 2, slot)

    rows = lax.broadcasted_iota(jnp.int32, (n_sorted, tt), 0)
    hit = (rows == pos_ref[0:1, :]) | (rows == pos_ref[1:2, :])
    onehot = jnp.where(hit, 1.0, 0.0).astype(BF16)
    tile0 = 0
    for h_ref, n_tiles in zip(h_refs, group_tiles):
        @pl.when((i >= tile0) & (i < tile0 + n_tiles))
        def _(h_ref=h_ref):
            sorted_ref[slot] = _dot(onehot, h_ref[...].astype(BF16))

        tile0 += n_tiles

    for e in range(N_EXPERTS):
        s = i * N_EXPERTS + e
        _for_segment_copies(
            slabs_ref[s],
            lambda off, rows, s=s: seg_copy(slot, local_ref[s] + off, global_ref[s] + off, rows).start())

    @pl.when(i == pl.num_programs(0) - 1)
    def _():
        @pl.when(i >= 1)
        def _():
            drain(i - 1, 1 - slot)

        drain(i, slot)


def _dispatch(hs, plan, tt, tm):
    group_tiles = tuple(h.shape[0] // tt for h in hs)
    n_tiles = sum(group_tiles)
    n_rows = _max_row_tiles(n_tiles * tt * TOP_K, tt, tm) * tm
    n_sorted = tt * TOP_K + N_EXPERTS * SLAB_ROWS
    in_specs = [pl.BlockSpec((TOP_K, tt), lambda i, *_: (0, i))]
    tile0 = 0
    for n_g in group_tiles:
        in_specs.append(pl.BlockSpec(
            (tt, D_MODEL), lambda i, *_, tile0=tile0, n_g=n_g: (jnp.clip(i - tile0, 0, n_g - 1), 0)))
        tile0 += n_g
    return pl.pallas_call(
        functools.partial(_dispatch_kernel, group_tiles=group_tiles),
        grid_spec=pltpu.PrefetchScalarGridSpec(
            num_scalar_prefetch=6,
            grid=(n_tiles,),
            in_specs=in_specs,
            out_specs=pl.BlockSpec(memory_space=pl.ANY),
            scratch_shapes=[pltpu.VMEM((2, n_sorted, D_MODEL), F32), pltpu.VMEM((tm, D_MODEL), F32),
                            pltpu.SemaphoreType.DMA((2,))],
        ),
        out_shape=jax.ShapeDtypeStruct((n_rows, D_MODEL), F32),
        compiler_params=_params(("arbitrary",)),
        name="moe_dispatch",
    )(plan.seg_slabs, plan.seg_local, plan.seg_global, plan.gsize, plan.gend, plan.n_used, plan.pos, *hs)


def _expert_ffn_kernel(te_ref, nu_ref, xs_ref, wg_ref, wu_ref, wd_ref, ys_ref, wg_b, wu_b, wd_b):
    i = pl.program_id(0)

    @pl.when((i == 0) | (te_ref[i] != te_ref[jnp.maximum(i - 1, 0)]))
    def _new_expert():
        wg_b[...] = wg_ref[0].astype(BF16)
        wu_b[...] = wu_ref[0].astype(BF16)
        wd_b[...] = wd_ref[0].astype(BF16)

    @pl.when(i < nu_ref[0])
    def _compute():
        xb = xs_ref[...].astype(BF16)
        y = jnp.zeros(ys_ref.shape, F32)
        for start, size in EXPERT_FF_CHUNKS:
            sl = slice(start, start + size)
            act = _silu(_dot(xb, wg_b[:, sl])) * _dot(xb, wu_b[:, sl])
            y = y + _dot(act.astype(BF16), wd_b[sl, :])
        ys_ref[...] = y

    @pl.when(i >= nu_ref[0])
    def _unused():
        ys_ref[...] = jnp.zeros_like(ys_ref)


def _expert_ffn(xs, plan, lw, tm):
    tile_expert, n_used = plan.tile_expert, plan.n_used
    n_tiles = xs.shape[0] // tm
    wspec = lambda w: pl.BlockSpec((1,) + w.shape[1:], lambda i, te, nu: (te[i], 0, 0))
    return pl.pallas_call(
        _expert_ffn_kernel,
        grid_spec=pltpu.PrefetchScalarGridSpec(
            num_scalar_prefetch=2,
            grid=(n_tiles,),
            in_specs=[pl.BlockSpec((tm, D_MODEL), lambda i, te, nu: (jnp.minimum(i, nu[0] - 1), 0)),
                      wspec(lw["moe_w_gate"]), wspec(lw["moe_w_up"]), wspec(lw["moe_w_down"])],
            out_specs=pl.BlockSpec((tm, D_MODEL), lambda i, te, nu: (i, 0)),
            scratch_shapes=[pltpu.VMEM((D_MODEL, D_FF_EXPERT), BF16), pltpu.VMEM((D_MODEL, D_FF_EXPERT), BF16),
                            pltpu.VMEM((D_FF_EXPERT, D_MODEL), BF16)],
        ),
        out_shape=jax.ShapeDtypeStruct(xs.shape, F32),
        compiler_params=_params(("arbitrary",)),
        name="expert_ffn",
    )(tile_expert, n_used, xs, lw["moe_w_gate"], lw["moe_w_up"], lw["moe_w_down"])


def _combine_kernel(dst_ref, x1_ref, topg_ref, mod_ref, fg_ref, ys_ref, out_ref, buf, sem):
    i = pl.program_id(0)
    n = pl.num_programs(0)
    tm = x1_ref.shape[0]

    def issue(tile, slot):
        base = tile * (tm * TOP_K)

        def body(r, carry):
            for k in range(TOP_K):
                d = dst_ref[base + TOP_K * r + k]
                pltpu.make_async_copy(ys_ref.at[pl.ds(d, 1)], buf.at[slot, k, pl.ds(r, 1)],
                                      sem.at[slot]).start()
            return carry

        lax.fori_loop(0, tm, body, 0, unroll=8)

    @pl.when(i == 0)
    def _first():
        issue(0, 0)

    @pl.when(i + 1 < n)
    def _ahead():
        issue(i + 1, (i + 1) % 2)

    slot = i % 2
    for k in range(TOP_K):
        pltpu.make_async_copy(ys_ref.at[pl.ds(0, tm)], buf.at[slot, k], sem.at[slot]).wait()
    g = topg_ref[...]
    f = g[:, 0:1] * buf[slot, 0] + g[:, 1:2] * buf[slot, 1]
    out_ref[...] = _rms(x1_ref[...] + mod_ref[0, 5:6, :] * f, fg_ref[...])


def _combine(x1, top_g, mod, final_g, ys, dst, *, mod_row):
    n = x1.shape[0]
    tm = TOKEN_TILE
    row = lambda i, *_: (i, 0)
    return pl.pallas_call(
        _combine_kernel,
        grid_spec=pltpu.PrefetchScalarGridSpec(
            num_scalar_prefetch=1,
            grid=(n // tm,),
            in_specs=[pl.BlockSpec((tm, D_MODEL), row), pl.BlockSpec((tm, TOP_K), row),
                      pl.BlockSpec((1, N_MOD, D_MODEL), lambda i, *_: (mod_row(i, tm), 0, 0)),
                      pl.BlockSpec(final_g.shape, lambda i, *_: (0, 0)),
                      pl.BlockSpec(memory_space=pl.ANY)],
            out_specs=pl.BlockSpec((tm, D_MODEL), row),
            scratch_shapes=[pltpu.VMEM((2, TOP_K, tm, D_MODEL), F32), pltpu.SemaphoreType.DMA((2,))],
        ),
        out_shape=jax.ShapeDtypeStruct((n, D_MODEL), F32),
        compiler_params=_params(("arbitrary",)),
        name="moe_combine",
    )(dst, x1, top_g, mod, final_g, ys)


def _sparse_moe(groups, lw, final_g):
    routed = [_router(x, mixed, mod, lw, mod_row=mod_row) for x, mixed, mod, mod_row in groups]
    plan = _route_plan(*(jnp.concatenate([r[k] for r in routed], axis=0) for k in (2, 4, 5)), MOE_ROW_TILE)
    xs = _dispatch([r[1] for r in routed], plan, MOE_TOKEN_TILE, MOE_ROW_TILE)
    ys = _expert_ffn(xs, plan, lw, MOE_ROW_TILE)
    outs, pair0 = [], 0
    for (_, _, mod, mod_row), (x1, _, _, top_g, _, _) in zip(groups, routed):
        n_pairs = x1.shape[0] * TOP_K
        outs.append(_combine(x1, top_g, mod, final_g, ys, plan.dst[pair0:pair0 + n_pairs], mod_row=mod_row))
        pair0 += n_pairs
    return outs


def _head_slots(w, used):
    k = w.shape[0]
    w = w.reshape(k, N_HEADS, used)
    return jnp.pad(w, ((0, 0), (0, 0), (0, HEAD_W - used))).reshape(k, N_HEADS * HEAD_W)


def _layer_weights(l, P):
    row = lambda v: v.reshape(1, -1)
    place = jnp.zeros((HEAD_W, N_HEADS, HEAD_W), F32)
    r = jnp.arange(MLA_ROPE_DIM)
    place = place.at[r, :, MLA_NOPE_DIM + r].set(1.0).reshape(HEAD_W, N_HEADS * HEAD_W)
    lw = {
        "norm_mix_g": row(P["norm_mix_g"][l]),
        "norm_ffn_g": row(P["norm_ffn_g"][l]),
        "w_in": jnp.pad(P["w_in"][l], ((0, 0), (0, W_IN_PAD - O_END))).astype(BF16),
        "q_norm_g": row(P["mla_q_norm_g"][l]),
        "kv_norm_g": row(P["mla_kv_norm_g"][l]),
        "w_uq": _head_slots(P["w_uq"][l], MLA_NOPE_DIM + MLA_ROPE_DIM).astype(BF16),
        "w_kc": _head_slots(P["w_uk"][l], MLA_NOPE_DIM).astype(BF16),
        "w_kr": place.astype(BF16),
        "w_kr_rows": place[:MLA_ROPE_DIM].astype(BF16),
        "w_uv": P["w_uv"][l].astype(BF16),
        "lq1": row(P["diff_lq1"][l]), "lk1": row(P["diff_lk1"][l]),
        "lq2": row(P["diff_lq2"][l]), "lk2": row(P["diff_lk2"][l]),
        "subln_g": row(P["diff_subln_g"][l]),
        "w_o": P["w_o"][l].astype(BF16),
    }
    if l % 2 == 0:
        i = l // 2
        lw.update(w_gate=P["w_gate"][i].astype(BF16), w_up=P["w_up"][i].astype(BF16),
                  w_down=P["w_down"][i].astype(BF16))
    else:
        m = l // 2
        lw.update(router=P["router"][m], moe_w_gate=P["moe_w_gate"][m],
                  moe_w_up=P["moe_w_up"][m], moe_w_down=P["moe_w_down"][m])
    return lw


def _run_group(x, mods, weights, *, batch, seq, mod_row, tables, caches, cache_out):
    assert DEPTH % 2 == 0
    own = []
    for l in range(DEPTH):
        lw = weights[l]
        if not cache_out:
            outs = _pre_mixer(x, mods[l], lw, tables, seq=seq, mod_row=mod_row)
        elif l < DEPTH - 1:
            outs = _pre_mixer(x, mods[l], lw, tables, seq=seq, mod_row=mod_row, cache_out="flat")
            own.append(outs[6:])
        else:
            outs = _pre_mixer(x, mods[l], lw, tables, seq=seq, mod_row=mod_row, cache_out="stacked",
                              prev=own)
            own = outs[6:]
        mixed = _attention(outs[:6], lw, l, batch=batch, seq=seq, cache=caches)
        if l % 2 == 0:
            x = _dense_ffn(x, mixed, mods[l], lw, mod_row=mod_row)
        else:
            assert l == DEPTH - 1
    return (x, mixed, mods[DEPTH - 1], mod_row), own


def kernel(x_prompt, x_sample, cache_diff_k, cache_diff_v, cache_mla_ckv, cache_mla_krope, c, c_ctx, w_ada, b_ada, norm_mix_g, norm_ffn_g, w_in, mla_q_norm_g, mla_kv_norm_g, w_uq, w_uk, w_uv, diff_lq1, diff_lk1, diff_lq2, diff_lk2, diff_subln_g, w_o, w_gate, w_up, w_down, router, moe_w_gate, moe_w_up, moe_w_down, final_norm_g):
    P = dict(norm_mix_g=norm_mix_g, norm_ffn_g=norm_ffn_g, w_in=w_in, mla_q_norm_g=mla_q_norm_g,
             mla_kv_norm_g=mla_kv_norm_g, w_uq=w_uq, w_uk=w_uk, w_uv=w_uv, diff_lq1=diff_lq1,
             diff_lk1=diff_lk1, diff_lq2=diff_lq2, diff_lk2=diff_lk2, diff_subln_g=diff_subln_g,
             w_o=w_o, w_gate=w_gate, w_up=w_up, w_down=w_down, router=router,
             moe_w_gate=moe_w_gate, moe_w_up=moe_w_up, moe_w_down=moe_w_down)
    bp, sp, d = x_prompt.shape
    bs, ss, _ = x_sample.shape
    n_past = cache_diff_k.shape[2]
    w4 = N_HEADS * HEAD_W

    cond = jnp.zeros((COND_ROWS, d), F32).at[0].set(c_ctx).at[1:1 + bs].set(c)
    mod_all = _ada_table(cond, w_ada, b_ada).reshape(DEPTH, COND_ROWS, N_MOD, d)
    mods = [mod_all[l] for l in range(DEPTH)]
    weights = [_layer_weights(l, P) for l in range(DEPTH)]
    final_g = final_norm_g.reshape(1, d)

    moe_in_p, own = _run_group(
        x_prompt.reshape(bp * sp, d), mods, weights, batch=bp, seq=sp,
        mod_row=lambda i, tm: 0, tables=None, caches=None, cache_out=True)
    new_diff_k, new_diff_v, new_mla_ckv, new_mla_krope = own

    caches = (cache_diff_k.reshape(bs, DEPTH, n_past, w4), cache_diff_v.reshape(bs, DEPTH, n_past, w4),
              cache_mla_ckv, cache_mla_krope)
    moe_in_s, _ = _run_group(
        x_sample.reshape(bs * ss, d), mods, weights, batch=bs, seq=ss,
        mod_row=lambda i, tm: 1 + (i * tm) // ss,
        tables=_rope_tables(ss), caches=caches, cache_out=False)

    yp, ys = _sparse_moe([moe_in_p, moe_in_s], weights[DEPTH - 1], final_g)
    y_prompt = yp.reshape(bp, sp, d)
    y_sample = ys.reshape(bs, ss, d)

    return (y_prompt, y_sample, new_diff_k, new_diff_v, new_mla_ckv, new_mla_krope)
```

```python
import functools
import math
from typing import NamedTuple

import jax
import jax.numpy as jnp
from jax import lax
from jax.experimental import pallas as pl
from jax.experimental.pallas import tpu as pltpu

F32 = jnp.float32
BF16 = jnp.bfloat16

D_MODEL = 1024
DEPTH = 2
GRID_W = 64
N_HEADS = 4
HEAD_W = 128
DIFF_HEAD_DIM = 64
MLA_NOPE_DIM = 64
MLA_ROPE_DIM = 32
Q_LORA = 256
KV_LORA = 128
D_FF = 2816
FF_CHUNK = 1408
N_EXPERTS = 8
D_FF_EXPERT = 1408
EXPERT_FF_CHUNKS = ((0, 512), (512, 512), (1024, 384))
ROPE_THETA = 10000.0
RMS_EPS = 1e-6
LOG2_E = math.log2(math.e)
N_MOD = 6
COND_ROWS = 16

TOKEN_TILE = 256
ATTN_Q_TILE = 1024
PRE_MIXER_TILE = 512
FFN_TOKEN_TILE = 1024
COMBINE_TOKEN_TILE = 512
MOE_ROW_TILE = 512
MOE_TOKEN_TILE = 512
SLAB_ROWS = 8
CHUNK_SLABS = 8
TOP_K = 2
VMEM_LIMIT_BYTES = 60 * 1024 * 1024

O_QD, O_KD, O_VD, O_CQ, O_CKV, O_KR, O_END = 0, 512, 1024, 1536, 1792, 1920, 1952
W_IN_PAD = 2048


def _params(sem):
    return pltpu.CompilerParams(dimension_semantics=sem, vmem_limit_bytes=VMEM_LIMIT_BYTES)


def _const_spec(shape):
    nd = len(shape)
    return pl.BlockSpec(shape, lambda *_: (0,) * nd)


def _rms(x, g):
    return x * lax.rsqrt(jnp.mean(x * x, axis=-1, keepdims=True) + RMS_EPS) * g


def _split_bf16(x):
    hi = x.astype(BF16)
    lo = (x - hi.astype(F32)).astype(BF16)
    return hi, lo


def _dot(a, b):
    return jnp.dot(a, b, preferred_element_type=F32)


def _dot3(a, b):
    a_hi, a_lo = _split_bf16(a)
    b_hi, b_lo = _split_bf16(b)
    return _dot(a_hi, b_hi) + _dot(a_hi, b_lo) + _dot(a_lo, b_hi)


def _ada_kernel(cond_ref, w_ref, b_ref, out_ref):
    cond = cond_ref[...]
    s = cond * jax.nn.sigmoid(cond)
    out_ref[0] = _dot3(s, w_ref[0]) + b_ref[0]


def _ada_table(cond, w_ada, b_ada):
    d = D_MODEL
    return pl.pallas_call(
        _ada_kernel,
        grid=(DEPTH, N_MOD),
        in_specs=[
            pl.BlockSpec((COND_ROWS, d), lambda l, j: (0, 0)),
            pl.BlockSpec((1, d, d), lambda l, j: (l, 0, j)),
            pl.BlockSpec((1, 1, d), lambda l, j: (l, 0, j)),
        ],
        out_specs=pl.BlockSpec((1, COND_ROWS, d), lambda l, j: (l, 0, j)),
        out_shape=jax.ShapeDtypeStruct((DEPTH, COND_ROWS, N_MOD * d), F32),
        compiler_params=_params(("arbitrary", "arbitrary")),
        name="ada_table",
    )(cond, w_ada, b_ada.reshape(DEPTH, 1, N_MOD * d))


def _axial_tables(seq, dim):
    half = dim // 4
    freqs = ROPE_THETA ** (-jnp.arange(half, dtype=F32) / half)
    pos = jnp.arange(seq, dtype=jnp.int32)
    rows = (pos // GRID_W).astype(F32)[:, None] * freqs[None, :]
    cols = (pos % GRID_W).astype(F32)[:, None] * freqs[None, :]
    cos = jnp.concatenate([jnp.cos(rows), jnp.cos(rows), jnp.cos(cols), jnp.cos(cols)], axis=-1)
    sin = jnp.concatenate([-jnp.sin(rows), jnp.sin(rows), -jnp.sin(cols), jnp.sin(cols)], axis=-1)
    return cos, sin


def _rope_tables(seq):
    cos64, sin64 = _axial_tables(seq, DIFF_HEAD_DIM)
    cos32, sin32 = _axial_tables(seq, MLA_ROPE_DIM)
    ones = lambda n: jnp.ones((seq, n), F32)
    zeros = lambda n: jnp.zeros((seq, n), F32)
    cos_d = jnp.tile(cos64, (1, 2 * N_HEADS))
    sin_d = jnp.tile(sin64, (1, 2 * N_HEADS))
    pad = HEAD_W - MLA_NOPE_DIM - MLA_ROPE_DIM
    cos_m = jnp.tile(jnp.concatenate([ones(MLA_NOPE_DIM), cos32, ones(pad)], axis=-1), (1, N_HEADS))
    sin_m = jnp.tile(jnp.concatenate([zeros(MLA_NOPE_DIM), sin32, zeros(pad)], axis=-1), (1, N_HEADS))
    cos_r = jnp.concatenate([cos32, ones(HEAD_W - MLA_ROPE_DIM)], axis=-1)
    sin_r = jnp.concatenate([sin32, zeros(HEAD_W - MLA_ROPE_DIM)], axis=-1)
    return cos_d, sin_d, cos_m, sin_m, cos_r, sin_r


def _rope(x, cos, sin, block):
    width = x.shape[-1]
    lane = lax.broadcasted_iota(jnp.int32, x.shape, 1)
    first = (lane % (2 * block)) < block
    partner = jnp.where(first, pltpu.roll(x, width - block, 1), pltpu.roll(x, block, 1))
    return x * cos + partner * sin


def _store_heads(ref, layer, x):
    for hd in range(N_HEADS):
        ref[0, layer, :, hd, :] = x[:, hd * HEAD_W:(hd + 1) * HEAD_W]


def _pre_mixer_kernel(*refs, rope, cache_out, n_prev):
    it = iter(refs)
    x_ref, mod_ref, g_ref, win_ref, qg_ref, kvg_ref, wuq_ref, wkc_ref, wkr_ref, wuv_ref = (
        next(it) for _ in range(10))
    if rope:
        cos_d, sin_d, cos_m, sin_m, cos_r, sin_r = (next(it) for _ in range(6))
    prev = [[next(it) for _ in range(4)] for _ in range(n_prev)]
    qd_ref, kd_ref, vd_ref, qm_ref, km_ref, vm_ref = (next(it) for _ in range(6))
    if cache_out:
        kd32_ref, vd32_ref, ckv32_ref, kr32_ref = (next(it) for _ in range(4))

    x = x_ref[...]
    h = _rms(x, g_ref[...]) * (1.0 + mod_ref[0, 1:2, :]) + mod_ref[0, 0:1, :]
    z = _dot(h.astype(BF16), win_ref[...])

    qd = z[:, O_QD:O_KD]
    kd = z[:, O_KD:O_VD]
    vd = z[:, O_VD:O_CQ]
    cq = z[:, O_CQ:O_CKV]
    ckv = z[:, O_CKV:O_KR]
    kr = z[:, O_KR:W_IN_PAD]

    qm = _dot(_rms(cq, qg_ref[...]).astype(BF16), wuq_ref[...])
    ckv = _rms(ckv, kvg_ref[...])
    if rope:
        qd = _rope(qd, cos_d[...], sin_d[...], DIFF_HEAD_DIM // 4)
        kd = _rope(kd, cos_d[...], sin_d[...], DIFF_HEAD_DIM // 4)
        qm = _rope(qm, cos_m[...], sin_m[...], MLA_ROPE_DIM // 4)
        kr = _rope(kr, cos_r[...], sin_r[...], MLA_ROPE_DIM // 4)

    ckv_b = ckv.astype(BF16)
    qd_ref[...] = (qd * (LOG2_E * DIFF_HEAD_DIM ** -0.5)).astype(BF16)
    kd_ref[...] = kd.astype(BF16)
    vd_ref[...] = vd.astype(BF16)
    qm_ref[...] = (qm * (LOG2_E * (MLA_NOPE_DIM + MLA_ROPE_DIM) ** -0.5)).astype(BF16)
    km_ref[...] = (_dot(ckv_b, wkc_ref[...]) + _dot(kr.astype(BF16), wkr_ref[...])).astype(BF16)
    vm_ref[...] = _dot(ckv_b, wuv_ref[...]).astype(BF16)
    if cache_out == "flat":
        kd32_ref[...] = kd
        vd32_ref[...] = vd
        ckv32_ref[...] = ckv
        kr32_ref[...] = kr[:, :MLA_ROPE_DIM]
    elif cache_out == "stacked":
        for l, (pk, pv, pc, pr) in enumerate(prev):
            _store_heads(kd32_ref, l, pk[...])
            _store_heads(vd32_ref, l, pv[...])
            ckv32_ref[0, l] = pc[...]
            kr32_ref[0, l] = pr[...]
        _store_heads(kd32_ref, n_prev, kd)
        _store_heads(vd32_ref, n_prev, vd)
        ckv32_ref[0, n_prev] = ckv
        kr32_ref[0, n_prev] = kr[:, :MLA_ROPE_DIM]


def _pre_mixer(x, mod, lw, tables, *, seq, mod_row, cache_out=None, prev=()):
    n = x.shape[0]
    tm = min(PRE_MIXER_TILE, seq)
    tiles_per_seq = seq // tm
    rope = tables is not None
    row = lambda i: (i, 0)
    args = [x, mod, lw["norm_mix_g"], lw["w_in"], lw["q_norm_g"], lw["kv_norm_g"],
            lw["w_uq"], lw["w_kc"], lw["w_kr"], lw["w_uv"]]
    specs = [pl.BlockSpec((tm, D_MODEL), row),
             pl.BlockSpec((1, N_MOD, D_MODEL), lambda i: (mod_row(i, tm), 0, 0))]
    specs += [_const_spec(a.shape) for a in args[2:]]
    if rope:
        for t in tables:
            args.append(t)
            specs.append(pl.BlockSpec((tm, t.shape[1]), lambda i: (i % tiles_per_seq, 0)))
    for layer_rows in prev:
        for a in layer_rows:
            args.append(a)
            specs.append(pl.BlockSpec((tm, a.shape[1]), row))
    w4 = N_HEADS * HEAD_W
    out_shape = [jax.ShapeDtypeStruct((n, w4), BF16)] * 6
    out_specs = [pl.BlockSpec((tm, w4), row)] * 6
    if cache_out == "flat":
        for w in (w4, w4, KV_LORA, MLA_ROPE_DIM):
            out_shape.append(jax.ShapeDtypeStruct((n, w), F32))
            out_specs.append(pl.BlockSpec((tm, w), row))
    elif cache_out == "stacked":
        n_layers = len(prev) + 1
        for tail in ((N_HEADS, HEAD_W), (N_HEADS, HEAD_W), (KV_LORA,), (MLA_ROPE_DIM,)):
            zeros = (0,) * len(tail)
            out_shape.append(jax.ShapeDtypeStruct((n // seq, n_layers, seq) + tail, F32))
            out_specs.append(pl.BlockSpec(
                (1, n_layers, tm) + tail,
                lambda i, zeros=zeros: (i // tiles_per_seq, 0, i % tiles_per_seq) + zeros))
    return pl.pallas_call(
        functools.partial(_pre_mixer_kernel, rope=rope, cache_out=cache_out, n_prev=len(prev)),
        grid=(n // tm,),
        in_specs=specs,
        out_specs=out_specs,
        out_shape=out_shape,
        compiler_params=_params(("parallel",)),
        name="pre_mixer",
    )(*args)


def _softmax_terms(s):
    m = jnp.max(s, axis=-1, keepdims=True)
    e = jnp.exp2(s - m)
    return e, 1.0 / jnp.sum(e, axis=-1, keepdims=True)


def _nt_dot(a, b):
    return lax.dot_general(a, b, (((1,), (1,)), ((), ())), preferred_element_type=F32)


def _attention_kernel(*refs, with_cache, lam_init, n_seqs):
    it = iter(refs)
    qd_ref, qm_ref, kd_ref, vd_ref, km_ref, vm_ref = (next(it) for _ in range(6))
    lq1, lk1, lq2, lk2, subg_ref = (next(it) for _ in range(5))
    if with_cache:
        ckd_ref, cvd_ref, cckv_ref, ckr_ref, wkc_ref, wkr_ref, wuv_ref = (next(it) for _ in range(7))
    out_ref = next(it)
    if with_cache:
        kd_all, vd_all, km_all, vm_all = (next(it) for _ in range(4))
        n_cache = cckv_ref.shape[2]

        @pl.when(pl.program_id(1) == 0)
        def _fill():
            kd_all[:n_cache, :] = ckd_ref[0, 0].astype(BF16)
            vd_all[:n_cache, :] = cvd_ref[0, 0].astype(BF16)
            cckv = cckv_ref[0, 0].astype(BF16)
            km_all[:n_cache, :] = (_dot(cckv, wkc_ref[...])
                                   + _dot(ckr_ref[0, 0].astype(BF16), wkr_ref[...])).astype(BF16)
            vm_all[:n_cache, :] = _dot(cckv, wuv_ref[...]).astype(BF16)
            kd_all[n_cache:, :] = kd_ref[...]
            vd_all[n_cache:, :] = vd_ref[...]
            km_all[n_cache:, :] = km_ref[...]
            vm_all[n_cache:, :] = vm_ref[...]
    else:
        kd_all, vd_all, km_all, vm_all = kd_ref, vd_ref, km_ref, vm_ref

    lam = (jnp.exp(jnp.sum(lq1[...] * lk1[...], axis=-1, keepdims=True))
           - jnp.exp(jnp.sum(lq2[...] * lk2[...], axis=-1, keepdims=True)) + lam_init)
    subg = subg_ref[...]
    tq = qd_ref.shape[0] // n_seqs
    tk = kd_all.shape[0] // n_seqs
    lane = lax.broadcasted_iota(jnp.int32, (tq, HEAD_W), 1)
    first = lane < DIFF_HEAD_DIM

    for s in range(n_seqs):
        qrows = slice(s * tq, (s + 1) * tq)
        krows = slice(s * tk, (s + 1) * tk)
        for hd in range(N_HEADS):
            sl = slice(hd * HEAD_W, (hd + 1) * HEAD_W)
            q = qd_ref[qrows, sl]
            k = kd_all[krows, sl]
            zero = jnp.zeros_like(q)
            e1, r1 = _softmax_terms(_nt_dot(jnp.where(first, q, zero), k))
            e2, r2 = _softmax_terms(_nt_dot(jnp.where(first, zero, q), k))
            p = (e1 * r1 - e2 * (lam * r2)).astype(BF16)
            o = _dot(p, vd_all[krows, sl])
            out_ref[qrows, sl] = (_rms(o, subg) * (1.0 - lam_init)).astype(BF16)

        for hd in range(N_HEADS):
            sl = slice(hd * HEAD_W, (hd + 1) * HEAD_W)
            osl = slice(N_HEADS * HEAD_W + hd * HEAD_W, N_HEADS * HEAD_W + (hd + 1) * HEAD_W)
            e, r = _softmax_terms(_nt_dot(qm_ref[qrows, sl], km_all[krows, sl]))
            o = _dot(e.astype(BF16), vm_all[krows, sl]) * r
            out_ref[qrows, osl] = o.astype(BF16)


def _attention(qkv, lw, layer, *, batch, seq, cache):
    qd, kd, vd, qm, km, vm = qkv
    w4 = N_HEADS * HEAD_W
    tq = min(ATTN_Q_TILE, seq)
    nq = seq // tq
    n_seqs = max(1, ATTN_Q_TILE // seq) if cache is None else 1
    assert batch % n_seqs == 0
    lam_init = 0.8 - 0.6 * math.exp(-0.3 * layer)
    q_spec = pl.BlockSpec((n_seqs * tq, w4), lambda b, j: (b * nq + j, 0))
    kv_spec = pl.BlockSpec((n_seqs * seq, w4), lambda b, j: (b, 0))
    args = [qd, qm, kd, vd, km, vm, lw["lq1"], lw["lk1"], lw["lq2"], lw["lk2"], lw["subln_g"]]
    specs = [q_spec, q_spec, kv_spec, kv_spec, kv_spec, kv_spec] + [_const_spec(a.shape) for a in args[6:]]
    scratch = []
    if cache is not None:
        ckd, cvd, cckv, ckr = cache
        n_cache = ckd.shape[2]
        for a in (ckd, cvd, cckv, ckr):
            args.append(a)
            specs.append(pl.BlockSpec((1, 1) + a.shape[2:], lambda b, j: (b, layer, 0, 0)))
        for name in ("w_kc", "w_kr_rows", "w_uv"):
            args.append(lw[name])
            specs.append(_const_spec(lw[name].shape))
        scratch = [pltpu.VMEM((n_cache + seq, w4), BF16)] * 4
    return pl.pallas_call(
        functools.partial(_attention_kernel, with_cache=cache is not None, lam_init=lam_init, n_seqs=n_seqs),
        grid=(batch // n_seqs, nq),
        in_specs=specs,
        out_specs=pl.BlockSpec((n_seqs * tq, 2 * w4), lambda b, j: (b * nq + j, 0)),
        out_shape=jax.ShapeDtypeStruct((batch * seq, 2 * w4), BF16),
        scratch_shapes=scratch,
        compiler_params=_params(("parallel", "arbitrary")),
        name="attention",
    )(*args)


def _post_mixer(x_ref, mixed_ref, mod_ref, wo_ref, g_ref):
    x1 = x_ref[...] + mod_ref[0, 2:3, :] * _dot(mixed_ref[...], wo_ref[...])
    h = _rms(x1, g_ref[...]) * (1.0 + mod_ref[0, 4:5, :]) + mod_ref[0, 3:4, :]
    return x1, h


def _silu(g):
    return g * jax.nn.sigmoid(g)


def _dense_ffn_kernel(x_ref, mixed_ref, mod_ref, wo_ref, g_ref, wg_ref, wu_ref, wd_ref, out_ref):
    x1, h = _post_mixer(x_ref, mixed_ref, mod_ref, wo_ref, g_ref)
    hb = h.astype(BF16)
    acc = jnp.zeros_like(x1)
    for c in range(D_FF // FF_CHUNK):
        sl = slice(c * FF_CHUNK, (c + 1) * FF_CHUNK)
        act = _silu(_dot(hb, wg_ref[:, sl])) * _dot(hb, wu_ref[:, sl])
        acc = acc + _dot(act.astype(BF16), wd_ref[sl, :])
    out_ref[...] = x1 + mod_ref[0, 5:6, :] * acc


def _dense_ffn(x, mixed, mod, lw, *, mod_row):
    n = x.shape[0]
    tm = FFN_TOKEN_TILE
    row = lambda i: (i, 0)
    weights = [lw["w_o"], lw["norm_ffn_g"], lw["w_gate"], lw["w_up"], lw["w_down"]]
    wspecs = [pl.BlockSpec(w.shape, lambda i: (0, 0), pipeline_mode=pl.Buffered(1)) for w in weights]
    return pl.pallas_call(
        _dense_ffn_kernel,
        grid=(n // tm,),
        in_specs=[pl.BlockSpec((tm, D_MODEL), row), pl.BlockSpec((tm, D_MODEL), row),
                  pl.BlockSpec((1, N_MOD, D_MODEL), lambda i: (mod_row(i, tm), 0, 0))] + wspecs,
        out_specs=pl.BlockSpec((tm, D_MODEL), row),
        out_shape=jax.ShapeDtypeStruct((n, D_MODEL), F32),
        compiler_params=_params(("parallel",)),
        name="dense_ffn",
    )(x, mixed, mod, *weights)


def _router_kernel(x_ref, mixed_ref, mod_ref, wo_ref, g_ref, router_ref,
                   x1_ref, h_ref, topi_ref, topg_ref, pos_ref, slabs_ref):
    x1, h = _post_mixer(x_ref, mixed_ref, mod_ref, wo_ref, g_ref)
    x1_ref[...] = x1
    h_ref[...] = h
    h_hi, h_lo = _split_bf16(h)
    r_hi, r_lo = _split_bf16(router_ref[...])
    hi_part = _nt_dot(jnp.concatenate([r_hi, r_lo], axis=0), h_hi)
    logits = hi_part[:N_EXPERTS] + hi_part[N_EXPERTS:] + _nt_dot(r_hi, h_lo)
    ex = jnp.exp(logits - jnp.max(logits, axis=0, keepdims=True))
    probs = ex / jnp.sum(ex, axis=0, keepdims=True)
    idx = lax.broadcasted_iota(jnp.int32, probs.shape, 0)
    p1 = jnp.max(probs, axis=0, keepdims=True)
    i1 = jnp.min(jnp.where(probs == p1, idx, N_EXPERTS), axis=0, keepdims=True)
    rest = jnp.where(idx == i1, -1.0, probs)
    p2 = jnp.max(rest, axis=0, keepdims=True)
    i2 = jnp.min(jnp.where(rest == p2, idx, N_EXPERTS), axis=0, keepdims=True)
    den = p1 + p2
    first = lax.broadcasted_iota(jnp.int32, topi_ref.shape, 0) == 0
    topi_ref[...] = jnp.where(first, i1, i2)
    topg_ref[...] = jnp.where(first, p1 / den, p2 / den)

    tt = idx.shape[1]
    hot1 = jnp.where(idx == i1, 1.0, 0.0)
    hot2 = jnp.where(idx == i2, 1.0, 0.0)
    tri = jnp.where(lax.broadcasted_iota(jnp.int32, (tt, tt), 0) <= lax.broadcasted_iota(jnp.int32, (tt, tt), 1),
                    1.0, 0.0).astype(BF16)
    before1 = _dot(hot1.astype(BF16), tri) - hot1
    before2 = _dot(hot2.astype(BF16), tri) - hot2
    count1 = jnp.sum(hot1, axis=1, keepdims=True)
    count2 = jnp.sum(hot2, axis=1, keepdims=True)
    slabs = jnp.floor((count1 + count2 + (SLAB_ROWS - 1)) * (1.0 / SLAB_ROWS))
    e_col = lax.broadcasted_iota(jnp.int32, (N_EXPERTS, 1), 0)
    seg_start = jnp.zeros_like(slabs)
    for e in range(N_EXPERTS - 1):
        seg_start = seg_start + jnp.where(e_col > e, slabs[e:e + 1, :] * SLAB_ROWS, 0.0)
    pos1 = jnp.sum(hot1 * (seg_start + before1), axis=0, keepdims=True)
    pos2 = jnp.sum(hot2 * (seg_start + count1 + before2), axis=0, keepdims=True)
    pos_ref[...] = jnp.where(first, pos1, pos2).astype(jnp.int32)
    slabs_ref[0] = slabs.astype(jnp.int32)


def _router(x, mixed, mod, lw, *, mod_row):
    n = x.shape[0]
    tm = MOE_TOKEN_TILE
    row = lambda i: (i, 0)
    weights = [lw["w_o"], lw["norm_ffn_g"], lw["router_t"]]
    i32 = jnp.int32
    col = lambda i: (0, i)
    x1, h, top_i, top_g, pos, slabs = pl.pallas_call(
        _router_kernel,
        grid=(n // tm,),
        in_specs=[pl.BlockSpec((tm, D_MODEL), row), pl.BlockSpec((tm, D_MODEL), row),
                  pl.BlockSpec((1, N_MOD, D_MODEL), lambda i: (mod_row(i, tm), 0, 0))]
                 + [_const_spec(w.shape) for w in weights],
        out_specs=[pl.BlockSpec((tm, D_MODEL), row), pl.BlockSpec((tm, D_MODEL), row),
                   pl.BlockSpec((TOP_K, tm), col), pl.BlockSpec((TOP_K, tm), col),
                   pl.BlockSpec((TOP_K, tm), col), pl.BlockSpec((1, N_EXPERTS, 1), lambda i: (i, 0, 0))],
        out_shape=[jax.ShapeDtypeStruct((n, D_MODEL), F32), jax.ShapeDtypeStruct((n, D_MODEL), F32),
                   jax.ShapeDtypeStruct((TOP_K, n), i32), jax.ShapeDtypeStruct((TOP_K, n), F32),
                   jax.ShapeDtypeStruct((TOP_K, n), i32), jax.ShapeDtypeStruct((n // tm, N_EXPERTS, 1), i32)],
        compiler_params=_params(("parallel",)),
        name="router",
    )(x, mixed, mod, *weights)
    return x1, h, top_i.T, top_g.T, pos.T, slabs.reshape(n // tm, 1, N_EXPERTS)


class _RoutePlan(NamedTuple):
    pos: jax.Array
    dst: jax.Array
    seg_slabs: jax.Array
    seg_local: jax.Array
    seg_global: jax.Array
    tile_expert: jax.Array
    n_used: jax.Array
    gsize: jax.Array
    gend: jax.Array


def _max_row_tiles(n_pairs, tt, tm):
    n_segments = n_pairs // (tt * TOP_K) * N_EXPERTS
    return (n_pairs + n_segments * (SLAB_ROWS - 1) + N_EXPERTS * (tm - 1)) // tm


def _route_plan(top_i, pos, slabs, tm):
    i32 = jnp.int32
    n = top_i.shape[0]
    n_tt = slabs.shape[0]
    tt = n // n_tt
    seg = slabs.reshape(n_tt, N_EXPERTS) * SLAB_ROWS
    seg_local = jnp.cumsum(seg, axis=1) - seg
    gsize = (jnp.sum(seg, axis=0) + tm - 1) // tm * tm
    gend = jnp.cumsum(gsize)
    seg_global = (gend - gsize)[None, :] + jnp.cumsum(seg, axis=0) - seg
    chosen = top_i.reshape(n_tt, tt, TOP_K)[..., None] == jnp.arange(N_EXPERTS, dtype=i32)
    shift = jnp.sum(jnp.where(chosen, (seg_global - seg_local)[:, None, None, :], 0), axis=-1)
    dst = (pos.reshape(n_tt, tt, TOP_K) + shift).reshape(-1)
    pos = pos.T
    n_tiles = _max_row_tiles(n * TOP_K, tt, tm)
    n_used = gend[-1] // tm
    tile_start = jnp.minimum(jnp.arange(n_tiles, dtype=i32), n_used - 1) * tm
    tile_expert = jnp.sum((gend[None, :] <= tile_start[:, None]).astype(i32), axis=1)
    flat = lambda a: a.reshape(-1).astype(i32)
    return _RoutePlan(pos.astype(i32), dst.astype(i32), flat(seg // SLAB_ROWS), flat(seg_local),
                      flat(seg_global), tile_expert.astype(i32), n_used.reshape(1).astype(i32),
                      gsize.astype(i32), gend.astype(i32))


def _for_segment_copies(n_slabs, fn):
    chunk_rows = CHUNK_SLABS * SLAB_ROWS
    n_chunks = n_slabs // CHUNK_SLABS

    def chunk(j, carry):
        fn(j * chunk_rows, chunk_rows)
        return carry

    def single(j, carry):
        fn(n_chunks * chunk_rows + j * SLAB_ROWS, SLAB_ROWS)
        return carry

    lax.fori_loop(0, n_chunks, chunk, 0)
    lax.fori_loop(0, n_slabs - n_chunks * CHUNK_SLABS, single, 0)


def _dispatch_kernel(slabs_ref, local_ref, global_ref, gsize_ref, gend_ref, nu_ref, pos_ref, *rest, group_tiles):
    h_refs = rest[:len(group_tiles)]
    xs_ref, sorted_ref, zero_ref, sem = rest[len(group_tiles):]
    i = pl.program_id(0)
    tm = zero_ref.shape[0]
    n_sorted, tt = sorted_ref.shape[1], h_refs[0].shape[0]

    @pl.when(i == 0)
    def _zero_unused():
        zero_ref[...] = jnp.zeros_like(zero_ref)

        def zero_tile(start):
            cp = pltpu.make_async_copy(zero_ref, xs_ref.at[pl.ds(pl.multiple_of(start, tm), tm)], sem.at[0])
            cp.start()
            cp.wait()

        for e in range(N_EXPERTS):
            @pl.when(gsize_ref[e] > 0)
            def _():
                zero_tile(gend_ref[e] - tm)

        def unused(t, carry):
            zero_tile(t * tm)
            return carry

        lax.fori_loop(nu_ref[0], xs_ref.shape[0] // tm, unused, 0)

    slot = i % 2

    def seg_copy(buf, src_row, dst_row, rows):
        return pltpu.make_async_copy(
            sorted_ref.at[buf, pl.ds(pl.multiple_of(src_row, SLAB_ROWS), rows)],
            xs_ref.at[pl.ds(pl.multiple_of(dst_row, SLAB_ROWS), rows)], sem.at[buf])

    def drain(tile, buf):
        for e in range(N_EXPERTS):
            _for_segment_copies(slabs_ref[tile * N_EXPERTS + e],
                                lambda off, rows: seg_copy(buf, 0, 0, rows).wait())

    @pl.when(i >= 2)
    def _():
        drain(i - 2, slot)

    rows = lax.broadcasted_iota(jnp.int32, (n_sorted, tt), 0)
    hit = (rows == pos_ref[0:1, :]) | (rows == pos_ref[1:2, :])
    onehot = jnp.where(hit, 1.0, 0.0).astype(BF16)
    tile0 = 0
    for h_ref, n_tiles in zip(h_refs, group_tiles):
        @pl.when((i >= tile0) & (i < tile0 + n_tiles))
        def _(h_ref=h_ref):
            sorted_ref[slot] = _dot(onehot, h_ref[...].astype(BF16))

        tile0 += n_tiles

    for e in range(N_EXPERTS):
        s = i * N_EXPERTS + e
        _for_segment_copies(
            slabs_ref[s],
            lambda off, rows, s=s: seg_copy(slot, local_ref[s] + off, global_ref[s] + off, rows).start())

    @pl.when(i == pl.num_programs(0) - 1)
    def _():
        @pl.when(i >= 1)
        def _():
            drain(i - 1, 1 - slot)

        drain(i, slot)


def _dispatch(hs, plan, tt, tm):
    group_tiles = tuple(h.shape[0] // tt for h in hs)
    n_tiles = sum(group_tiles)
    n_rows = _max_row_tiles(n_tiles * tt * TOP_K, tt, tm) * tm
    n_sorted = tt * TOP_K + N_EXPERTS * SLAB_ROWS
    in_specs = [pl.BlockSpec((TOP_K, tt), lambda i, *_: (0, i))]
    tile0 = 0
    for n_g in group_tiles:
        in_specs.append(pl.BlockSpec(
            (tt, D_MODEL), lambda i, *_, tile0=tile0, n_g=n_g: (jnp.clip(i - tile0, 0, n_g - 1), 0)))
        tile0 += n_g
    return pl.pallas_call(
        functools.partial(_dispatch_kernel, group_tiles=group_tiles),
        grid_spec=pltpu.PrefetchScalarGridSpec(
            num_scalar_prefetch=6,
            grid=(n_tiles,),
            in_specs=in_specs,
            out_specs=pl.BlockSpec(memory_space=pl.ANY),
            scratch_shapes=[pltpu.VMEM((2, n_sorted, D_MODEL), F32), pltpu.VMEM((tm, D_MODEL), F32),
                            pltpu.SemaphoreType.DMA((2,))],
        ),
        out_shape=jax.ShapeDtypeStruct((n_rows, D_MODEL), F32),
        compiler_params=_params(("arbitrary",)),
        name="moe_dispatch",
    )(plan.seg_slabs, plan.seg_local, plan.seg_global, plan.gsize, plan.gend, plan.n_used, plan.pos, *hs)


def _expert_ffn_kernel(te_ref, nu_ref, xs_ref, wg_ref, wu_ref, wd_ref, ys_ref, wg_b, wu_b, wd_b):
    i = pl.program_id(0)

    @pl.when((i == 0) | (te_ref[i] != te_ref[jnp.maximum(i - 1, 0)]))
    def _new_expert():
        wg_b[...] = wg_ref[0].astype(BF16)
        wu_b[...] = wu_ref[0].astype(BF16)
        wd_b[...] = wd_ref[0].astype(BF16)

    @pl.when(i < nu_ref[0])
    def _compute():
        xb = xs_ref[...].astype(BF16)
        y = jnp.zeros(ys_ref.shape, F32)
        for start, size in EXPERT_FF_CHUNKS:
            sl = slice(start, start + size)
            act = _silu(_dot(xb, wg_b[:, sl])) * _dot(xb, wu_b[:, sl])
            y = y + _dot(act.astype(BF16), wd_b[sl, :])
        ys_ref[...] = y

    @pl.when(i >= nu_ref[0])
    def _unused():
        ys_ref[...] = jnp.zeros_like(ys_ref)


def _expert_ffn(xs, plan, lw, tm):
    tile_expert, n_used = plan.tile_expert, plan.n_used
    n_tiles = xs.shape[0] // tm
    wspec = lambda w: pl.BlockSpec((1,) + w.shape[1:], lambda i, te, nu: (te[i], 0, 0))
    return pl.pallas_call(
        _expert_ffn_kernel,
        grid_spec=pltpu.PrefetchScalarGridSpec(
            num_scalar_prefetch=2,
            grid=(n_tiles,),
            in_specs=[pl.BlockSpec((tm, D_MODEL), lambda i, te, nu: (jnp.minimum(i, nu[0] - 1), 0)),
                      wspec(lw["moe_w_gate"]), wspec(lw["moe_w_up"]), wspec(lw["moe_w_down"])],
            out_specs=pl.BlockSpec((tm, D_MODEL), lambda i, te, nu: (i, 0)),
            scratch_shapes=[pltpu.VMEM((D_MODEL, D_FF_EXPERT), BF16), pltpu.VMEM((D_MODEL, D_FF_EXPERT), BF16),
                            pltpu.VMEM((D_FF_EXPERT, D_MODEL), BF16)],
        ),
        out_shape=jax.ShapeDtypeStruct(xs.shape, F32),
        compiler_params=_params(("arbitrary",)),
        name="expert_ffn",
    )(tile_expert, n_used, xs, lw["moe_w_gate"], lw["moe_w_up"], lw["moe_w_down"])


def _combine_kernel(dst_ref, x1_ref, topg_ref, mod_ref, fg_ref, ys_ref, out_ref, buf, sem):
    i = pl.program_id(0)
    n = pl.num_programs(0)
    tm = x1_ref.shape[0]

    def issue(tile, slot):
        base = tile * (tm * TOP_K)

        def body(r, carry):
            for k in range(TOP_K):
                d = dst_ref[base + TOP_K * r + k]
                pltpu.make_async_copy(ys_ref.at[pl.ds(d, 1)], buf.at[slot, k, pl.ds(r, 1)],
                                      sem.at[slot]).start()
            return carry

        lax.fori_loop(0, tm, body, 0, unroll=8)

    @pl.when(i == 0)
    def _first():
        issue(0, 0)

    @pl.when(i + 1 < n)
    def _ahead():
        issue(i + 1, (i + 1) % 2)

    slot = i % 2
    for k in range(TOP_K):
        pltpu.make_async_copy(ys_ref.at[pl.ds(0, tm)], buf.at[slot, k], sem.at[slot]).wait()
    g = topg_ref[...]
    f = g[:, 0:1] * buf[slot, 0] + g[:, 1:2] * buf[slot, 1]
    out_ref[...] = _rms(x1_ref[...] + mod_ref[0, 5:6, :] * f, fg_ref[...])


def _combine(x1, top_g, mod, final_g, ys, dst, *, mod_row):
    n = x1.shape[0]
    tm = COMBINE_TOKEN_TILE
    row = lambda i, *_: (i, 0)
    return pl.pallas_call(
        _combine_kernel,
        grid_spec=pltpu.PrefetchScalarGridSpec(
            num_scalar_prefetch=1,
            grid=(n // tm,),
            in_specs=[pl.BlockSpec((tm, D_MODEL), row), pl.BlockSpec((tm, TOP_K), row),
                      pl.BlockSpec((1, N_MOD, D_MODEL), lambda i, *_: (mod_row(i, tm), 0, 0)),
                      pl.BlockSpec(final_g.shape, lambda i, *_: (0, 0)),
                      pl.BlockSpec(memory_space=pl.ANY)],
            out_specs=pl.BlockSpec((tm, D_MODEL), row),
            scratch_shapes=[pltpu.VMEM((2, TOP_K, tm, D_MODEL), F32), pltpu.SemaphoreType.DMA((2,))],
        ),
        out_shape=jax.ShapeDtypeStruct((n, D_MODEL), F32),
        compiler_params=_params(("arbitrary",)),
        name="moe_combine",
    )(dst, x1, top_g, mod, final_g, ys)


def _sparse_moe(groups, lw, final_g):
    routed = [_router(x, mixed, mod, lw, mod_row=mod_row) for x, mixed, mod, mod_row in groups]
    plan = _route_plan(*(jnp.concatenate([r[k] for r in routed], axis=0) for k in (2, 4, 5)), MOE_ROW_TILE)
    xs = _dispatch([r[1] for r in routed], plan, MOE_TOKEN_TILE, MOE_ROW_TILE)
    ys = _expert_ffn(xs, plan, lw, MOE_ROW_TILE)
    outs, pair0 = [], 0
    for (_, _, mod, mod_row), (x1, _, _, top_g, _, _) in zip(groups, routed):
        n_pairs = x1.shape[0] * TOP_K
        outs.append(_combine(x1, top_g, mod, final_g, ys, plan.dst[pair0:pair0 + n_pairs], mod_row=mod_row))
        pair0 += n_pairs
    return outs


def _head_slots(w, used):
    k = w.shape[0]
    w = w.reshape(k, N_HEADS, used)
    return jnp.pad(w, ((0, 0), (0, 0), (0, HEAD_W - used))).reshape(k, N_HEADS * HEAD_W)


def _layer_weights(l, P):
    row = lambda v: v.reshape(1, -1)
    place = jnp.zeros((HEAD_W, N_HEADS, HEAD_W), F32)
    r = jnp.arange(MLA_ROPE_DIM)
    place = place.at[r, :, MLA_NOPE_DIM + r].set(1.0).reshape(HEAD_W, N_HEADS * HEAD_W)
    lw = {
        "norm_mix_g": row(P["norm_mix_g"][l]),
        "norm_ffn_g": row(P["norm_ffn_g"][l]),
        "w_in": jnp.pad(P["w_in"][l], ((0, 0), (0, W_IN_PAD - O_END))).astype(BF16),
        "q_norm_g": row(P["mla_q_norm_g"][l]),
        "kv_norm_g": row(P["mla_kv_norm_g"][l]),
        "w_uq": _head_slots(P["w_uq"][l], MLA_NOPE_DIM + MLA_ROPE_DIM).astype(BF16),
        "w_kc": _head_slots(P["w_uk"][l], MLA_NOPE_DIM).astype(BF16),
        "w_kr": place.astype(BF16),
        "w_kr_rows": place[:MLA_ROPE_DIM].astype(BF16),
        "w_uv": P["w_uv"][l].astype(BF16),
        "lq1": row(P["diff_lq1"][l]), "lk1": row(P["diff_lk1"][l]),
        "lq2": row(P["diff_lq2"][l]), "lk2": row(P["diff_lk2"][l]),
        "subln_g": row(P["diff_subln_g"][l]),
        "w_o": P["w_o"][l].astype(BF16),
    }
    if l % 2 == 0:
        i = l // 2
        lw.update(w_gate=P["w_gate"][i].astype(BF16), w_up=P["w_up"][i].astype(BF16),
                  w_down=P["w_down"][i].astype(BF16))
    else:
        m = l // 2
        lw.update(router_t=P["router"][m].T, moe_w_gate=P["moe_w_gate"][m],
                  moe_w_up=P["moe_w_up"][m], moe_w_down=P["moe_w_down"][m])
    return lw


def _run_group(x, mods, weights, *, batch, seq, mod_row, tables, caches, cache_out):
    assert DEPTH % 2 == 0
    own = []
    for l in range(DEPTH):
        lw = weights[l]
        if not cache_out:
            outs = _pre_mixer(x, mods[l], lw, tables, seq=seq, mod_row=mod_row)
        elif l < DEPTH - 1:
            outs = _pre_mixer(x, mods[l], lw, tables, seq=seq, mod_row=mod_row, cache_out="flat")
            own.append(outs[6:])
        else:
            outs = _pre_mixer(x, mods[l], lw, tables, seq=seq, mod_row=mod_row, cache_out="stacked",
                              prev=own)
            own = outs[6:]
        mixed = _attention(outs[:6], lw, l, batch=batch, seq=seq, cache=caches)
        if l % 2 == 0:
            x = _dense_ffn(x, mixed, mods[l], lw, mod_row=mod_row)
        else:
            assert l == DEPTH - 1
    return (x, mixed, mods[DEPTH - 1], mod_row), own


def kernel(x_prompt, x_sample, cache_diff_k, cache_diff_v, cache_mla_ckv, cache_mla_krope, c, c_ctx, w_ada, b_ada, norm_mix_g, norm_ffn_g, w_in, mla_q_norm_g, mla_kv_norm_g, w_uq, w_uk, w_uv, diff_lq1, diff_lk1, diff_lq2, diff_lk2, diff_subln_g, w_o, w_gate, w_up, w_down, router, moe_w_gate, moe_w_up, moe_w_down, final_norm_g):
    P = dict(norm_mix_g=norm_mix_g, norm_ffn_g=norm_ffn_g, w_in=w_in, mla_q_norm_g=mla_q_norm_g,
             mla_kv_norm_g=mla_kv_norm_g, w_uq=w_uq, w_uk=w_uk, w_uv=w_uv, diff_lq1=diff_lq1,
             diff_lk1=diff_lk1, diff_lq2=diff_lq2, diff_lk2=diff_lk2, diff_subln_g=diff_subln_g,
             w_o=w_o, w_gate=w_gate, w_up=w_up, w_down=w_down, router=router,
             moe_w_gate=moe_w_gate, moe_w_up=moe_w_up, moe_w_down=moe_w_down)
    bp, sp, d = x_prompt.shape
    bs, ss, _ = x_sample.shape
    n_past = cache_diff_k.shape[2]
    w4 = N_HEADS * HEAD_W

    cond = jnp.zeros((COND_ROWS, d), F32).at[0].set(c_ctx).at[1:1 + bs].set(c)
    mod_all = _ada_table(cond, w_ada, b_ada).reshape(DEPTH, COND_ROWS, N_MOD, d)
    mods = [mod_all[l] for l in range(DEPTH)]
    weights = [_layer_weights(l, P) for l in range(DEPTH)]
    final_g = final_norm_g.reshape(1, d)

    moe_in_p, own = _run_group(
        x_prompt.reshape(bp * sp, d), mods, weights, batch=bp, seq=sp,
        mod_row=lambda i, tm: 0, tables=None, caches=None, cache_out=True)
    new_diff_k, new_diff_v, new_mla_ckv, new_mla_krope = own

    caches = (cache_diff_k.reshape(bs, DEPTH, n_past, w4), cache_diff_v.reshape(bs, DEPTH, n_past, w4),
              cache_mla_ckv, cache_mla_krope)
    moe_in_s, _ = _run_group(
        x_sample.reshape(bs * ss, d), mods, weights, batch=bs, seq=ss,
        mod_row=lambda i, tm: 1 + (i * tm) // ss,
        tables=_rope_tables(ss), caches=caches, cache_out=False)

    yp, ys = _sparse_moe([moe_in_p, moe_in_s], weights[DEPTH - 1], final_g)
    y_prompt = yp.reshape(bp, sp, d)
    y_sample = ys.reshape(bs, ss, d)

    return (y_prompt, y_sample, new_diff_k, new_diff_v, new_mla_ckv, new_mla_krope)
```

```python
import functools
import math
from typing import NamedTuple

import jax
import jax.numpy as jnp
from jax import lax
from jax.experimental import pallas as pl
from jax.experimental.pallas import tpu as pltpu

F32 = jnp.float32
BF16 = jnp.bfloat16

D_MODEL = 1024
DEPTH = 2
GRID_W = 64
N_HEADS = 4
HEAD_W = 128
DIFF_HEAD_DIM = 64
MLA_NOPE_DIM = 64
MLA_ROPE_DIM = 32
Q_LORA = 256
KV_LORA = 128
D_FF = 2816
FF_CHUNK = 1408
N_EXPERTS = 8
D_FF_EXPERT = 1408
EXPERT_FF_CHUNKS = ((0, 512), (512, 512), (1024, 384))
ROPE_THETA = 10000.0
RMS_EPS = 1e-6
LOG2_E = math.log2(math.e)
N_MOD = 6
COND_ROWS = 16

TOKEN_TILE = 256
ATTN_Q_TILE = 1024
PRE_MIXER_TILE = 512
FFN_TOKEN_TILE = 1024
COMBINE_TOKEN_TILE = 512
MOE_ROW_TILE = 512
MOE_TOKEN_TILE = 512
SLAB_ROWS = 16
CHUNK_SLABS = 8
TOP_K = 2
VMEM_LIMIT_BYTES = 60 * 1024 * 1024

O_QD, O_KD, O_VD, O_CQ, O_CKV, O_KR, O_END = 0, 512, 1024, 1536, 1792, 1920, 1952
W_IN_PAD = 2048


def _params(sem):
    return pltpu.CompilerParams(dimension_semantics=sem, vmem_limit_bytes=VMEM_LIMIT_BYTES)


def _const_spec(shape):
    nd = len(shape)
    return pl.BlockSpec(shape, lambda *_: (0,) * nd)


def _rms(x, g):
    return x * lax.rsqrt(jnp.mean(x * x, axis=-1, keepdims=True) + RMS_EPS) * g


def _split_bf16(x):
    hi = x.astype(BF16)
    lo = (x - hi.astype(F32)).astype(BF16)
    return hi, lo


def _dot(a, b):
    return jnp.dot(a, b, preferred_element_type=F32)


def _dot3(a, b):
    a_hi, a_lo = _split_bf16(a)
    b_hi, b_lo = _split_bf16(b)
    return _dot(a_hi, b_hi) + _dot(a_hi, b_lo) + _dot(a_lo, b_hi)


def _ada_kernel(cond_ref, w_ref, b_ref, out_ref):
    cond = cond_ref[...]
    s = cond * jax.nn.sigmoid(cond)
    out_ref[0] = _dot3(s, w_ref[0]) + b_ref[0]


def _ada_table(cond, w_ada, b_ada):
    d = D_MODEL
    return pl.pallas_call(
        _ada_kernel,
        grid=(DEPTH, N_MOD),
        in_specs=[
            pl.BlockSpec((COND_ROWS, d), lambda l, j: (0, 0)),
            pl.BlockSpec((1, d, d), lambda l, j: (l, 0, j)),
            pl.BlockSpec((1, 1, d), lambda l, j: (l, 0, j)),
        ],
        out_specs=pl.BlockSpec((1, COND_ROWS, d), lambda l, j: (l, 0, j)),
        out_shape=jax.ShapeDtypeStruct((DEPTH, COND_ROWS, N_MOD * d), F32),
        compiler_params=_params(("arbitrary", "arbitrary")),
        name="ada_table",
    )(cond, w_ada, b_ada.reshape(DEPTH, 1, N_MOD * d))


def _axial_tables(seq, dim):
    half = dim // 4
    freqs = ROPE_THETA ** (-jnp.arange(half, dtype=F32) / half)
    pos = jnp.arange(seq, dtype=jnp.int32)
    rows = (pos // GRID_W).astype(F32)[:, None] * freqs[None, :]
    cols = (pos % GRID_W).astype(F32)[:, None] * freqs[None, :]
    cos = jnp.concatenate([jnp.cos(rows), jnp.cos(rows), jnp.cos(cols), jnp.cos(cols)], axis=-1)
    sin = jnp.concatenate([-jnp.sin(rows), jnp.sin(rows), -jnp.sin(cols), jnp.sin(cols)], axis=-1)
    return cos, sin


def _rope_tables(seq):
    cos64, sin64 = _axial_tables(seq, DIFF_HEAD_DIM)
    cos32, sin32 = _axial_tables(seq, MLA_ROPE_DIM)
    ones = lambda n: jnp.ones((seq, n), F32)
    zeros = lambda n: jnp.zeros((seq, n), F32)
    cos_d = jnp.tile(cos64, (1, 2 * N_HEADS))
    sin_d = jnp.tile(sin64, (1, 2 * N_HEADS))
    pad = HEAD_W - MLA_NOPE_DIM - MLA_ROPE_DIM
    cos_m = jnp.tile(jnp.concatenate([ones(MLA_NOPE_DIM), cos32, ones(pad)], axis=-1), (1, N_HEADS))
    sin_m = jnp.tile(jnp.concatenate([zeros(MLA_NOPE_DIM), sin32, zeros(pad)], axis=-1), (1, N_HEADS))
    cos_r = jnp.concatenate([cos32, ones(HEAD_W - MLA_ROPE_DIM)], axis=-1)
    sin_r = jnp.concatenate([sin32, zeros(HEAD_W - MLA_ROPE_DIM)], axis=-1)
    return cos_d, sin_d, cos_m, sin_m, cos_r, sin_r


def _rope(x, cos, sin, block):
    width = x.shape[-1]
    lane = lax.broadcasted_iota(jnp.int32, x.shape, 1)
    first = (lane % (2 * block)) < block
    partner = jnp.where(first, pltpu.roll(x, width - block, 1), pltpu.roll(x, block, 1))
    return x * cos + partner * sin


def _store_heads(ref, layer, x):
    for hd in range(N_HEADS):
        ref[0, layer, :, hd, :] = x[:, hd * HEAD_W:(hd + 1) * HEAD_W]


def _pre_mixer_kernel(*refs, rope, cache_out, n_prev):
    it = iter(refs)
    x_ref, mod_ref, g_ref, win_ref, qg_ref, kvg_ref, wuq_ref, wkc_ref, wkr_ref, wuv_ref = (
        next(it) for _ in range(10))
    if rope:
        cos_d, sin_d, cos_m, sin_m, cos_r, sin_r = (next(it) for _ in range(6))
    prev = [[next(it) for _ in range(4)] for _ in range(n_prev)]
    qd_ref, kd_ref, vd_ref, qm_ref, km_ref, vm_ref = (next(it) for _ in range(6))
    if cache_out:
        kd32_ref, vd32_ref, ckv32_ref, kr32_ref = (next(it) for _ in range(4))

    x = x_ref[...]
    h = _rms(x, g_ref[...]) * (1.0 + mod_ref[0, 1:2, :]) + mod_ref[0, 0:1, :]
    z = _dot(h.astype(BF16), win_ref[...])

    qd = z[:, O_QD:O_KD]
    kd = z[:, O_KD:O_VD]
    vd = z[:, O_VD:O_CQ]
    cq = z[:, O_CQ:O_CKV]
    ckv = z[:, O_CKV:O_KR]
    kr = z[:, O_KR:W_IN_PAD]

    qm = _dot(_rms(cq, qg_ref[...]).astype(BF16), wuq_ref[...])
    ckv = _rms(ckv, kvg_ref[...])
    if rope:
        qd = _rope(qd, cos_d[...], sin_d[...], DIFF_HEAD_DIM // 4)
        kd = _rope(kd, cos_d[...], sin_d[...], DIFF_HEAD_DIM // 4)
        qm = _rope(qm, cos_m[...], sin_m[...], MLA_ROPE_DIM // 4)
        kr = _rope(kr, cos_r[...], sin_r[...], MLA_ROPE_DIM // 4)

    ckv_b = ckv.astype(BF16)
    qd_ref[...] = (qd * (LOG2_E * DIFF_HEAD_DIM ** -0.5)).astype(BF16)
    kd_ref[...] = kd.astype(BF16)
    vd_ref[...] = vd.astype(BF16)
    qm_ref[...] = (qm * (LOG2_E * (MLA_NOPE_DIM + MLA_ROPE_DIM) ** -0.5)).astype(BF16)
    km_ref[...] = (_dot(ckv_b, wkc_ref[...]) + _dot(kr.astype(BF16), wkr_ref[...])).astype(BF16)
    vm_ref[...] = _dot(ckv_b, wuv_ref[...]).astype(BF16)
    if cache_out == "flat":
        kd32_ref[...] = kd
        vd32_ref[...] = vd
        ckv32_ref[...] = ckv
        kr32_ref[...] = kr[:, :MLA_ROPE_DIM]
    elif cache_out == "stacked":
        for l, (pk, pv, pc, pr) in enumerate(prev):
            _store_heads(kd32_ref, l, pk[...])
            _store_heads(vd32_ref, l, pv[...])
            ckv32_ref[0, l] = pc[...]
            kr32_ref[0, l] = pr[...]
        _store_heads(kd32_ref, n_prev, kd)
        _store_heads(vd32_ref, n_prev, vd)
        ckv32_ref[0, n_prev] = ckv
        kr32_ref[0, n_prev] = kr[:, :MLA_ROPE_DIM]


def _pre_mixer(x, mod, lw, tables, *, seq, mod_row, cache_out=None, prev=()):
    n = x.shape[0]
    tm = min(PRE_MIXER_TILE, seq)
    tiles_per_seq = seq // tm
    rope = tables is not None
    row = lambda i: (i, 0)
    args = [x, mod, lw["norm_mix_g"], lw["w_in"], lw["q_norm_g"], lw["kv_norm_g"],
            lw["w_uq"], lw["w_kc"], lw["w_kr"], lw["w_uv"]]
    specs = [pl.BlockSpec((tm, D_MODEL), row),
             pl.BlockSpec((1, N_MOD, D_MODEL), lambda i: (mod_row(i, tm), 0, 0))]
    specs += [_const_spec(a.shape) for a in args[2:]]
    if rope:
        for t in tables:
            args.append(t)
            specs.append(pl.BlockSpec((tm, t.shape[1]), lambda i: (i % tiles_per_seq, 0)))
    for layer_rows in prev:
        for a in layer_rows:
            args.append(a)
            specs.append(pl.BlockSpec((tm, a.shape[1]), row))
    w4 = N_HEADS * HEAD_W
    out_shape = [jax.ShapeDtypeStruct((n, w4), BF16)] * 6
    out_specs = [pl.BlockSpec((tm, w4), row)] * 6
    if cache_out == "flat":
        for w in (w4, w4, KV_LORA, MLA_ROPE_DIM):
            out_shape.append(jax.ShapeDtypeStruct((n, w), F32))
            out_specs.append(pl.BlockSpec((tm, w), row))
    elif cache_out == "stacked":
        n_layers = len(prev) + 1
        for tail in ((N_HEADS, HEAD_W), (N_HEADS, HEAD_W), (KV_LORA,), (MLA_ROPE_DIM,)):
            zeros = (0,) * len(tail)
            out_shape.append(jax.ShapeDtypeStruct((n // seq, n_layers, seq) + tail, F32))
            out_specs.append(pl.BlockSpec(
                (1, n_layers, tm) + tail,
                lambda i, zeros=zeros: (i // tiles_per_seq, 0, i % tiles_per_seq) + zeros))
    return pl.pallas_call(
        functools.partial(_pre_mixer_kernel, rope=rope, cache_out=cache_out, n_prev=len(prev)),
        grid=(n // tm,),
        in_specs=specs,
        out_specs=out_specs,
        out_shape=out_shape,
        compiler_params=_params(("parallel",)),
        name="pre_mixer",
    )(*args)


def _softmax_terms(s):
    m = jnp.max(s, axis=-1, keepdims=True)
    e = jnp.exp2(s - m)
    return e, 1.0 / jnp.sum(e, axis=-1, keepdims=True)


def _nt_dot(a, b):
    return lax.dot_general(a, b, (((1,), (1,)), ((), ())), preferred_element_type=F32)


def _attention_kernel(*refs, with_cache, lam_init, n_seqs):
    it = iter(refs)
    qd_ref, qm_ref, kd_ref, vd_ref, km_ref, vm_ref = (next(it) for _ in range(6))
    lq1, lk1, lq2, lk2, subg_ref = (next(it) for _ in range(5))
    if with_cache:
        ckd_ref, cvd_ref, cckv_ref, ckr_ref, wkc_ref, wkr_ref, wuv_ref = (next(it) for _ in range(7))
    out_ref = next(it)
    if with_cache:
        kd_all, vd_all, km_all, vm_all = (next(it) for _ in range(4))
        n_cache = cckv_ref.shape[2]

        @pl.when(pl.program_id(1) == 0)
        def _fill():
            kd_all[:n_cache, :] = ckd_ref[0, 0].astype(BF16)
            vd_all[:n_cache, :] = cvd_ref[0, 0].astype(BF16)
            cckv = cckv_ref[0, 0].astype(BF16)
            km_all[:n_cache, :] = (_dot(cckv, wkc_ref[...])
                                   + _dot(ckr_ref[0, 0].astype(BF16), wkr_ref[...])).astype(BF16)
            vm_all[:n_cache, :] = _dot(cckv, wuv_ref[...]).astype(BF16)
            kd_all[n_cache:, :] = kd_ref[...]
            vd_all[n_cache:, :] = vd_ref[...]
            km_all[n_cache:, :] = km_ref[...]
            vm_all[n_cache:, :] = vm_ref[...]
    else:
        kd_all, vd_all, km_all, vm_all = kd_ref, vd_ref, km_ref, vm_ref

    lam = (jnp.exp(jnp.sum(lq1[...] * lk1[...], axis=-1, keepdims=True))
           - jnp.exp(jnp.sum(lq2[...] * lk2[...], axis=-1, keepdims=True)) + lam_init)
    subg = subg_ref[...]
    tq = qd_ref.shape[0] // n_seqs
    tk = kd_all.shape[0] // n_seqs
    lane = lax.broadcasted_iota(jnp.int32, (tq, HEAD_W), 1)
    first = lane < DIFF_HEAD_DIM

    for s in range(n_seqs):
        qrows = slice(s * tq, (s + 1) * tq)
        krows = slice(s * tk, (s + 1) * tk)
        for hd in range(N_HEADS):
            sl = slice(hd * HEAD_W, (hd + 1) * HEAD_W)
            q = qd_ref[qrows, sl]
            k = kd_all[krows, sl]
            zero = jnp.zeros_like(q)
            e1, r1 = _softmax_terms(_nt_dot(jnp.where(first, q, zero), k))
            e2, r2 = _softmax_terms(_nt_dot(jnp.where(first, zero, q), k))
            p = (e1 * r1 - e2 * (lam * r2)).astype(BF16)
            o = _dot(p, vd_all[krows, sl])
            out_ref[qrows, sl] = (_rms(o, subg) * (1.0 - lam_init)).astype(BF16)

        for hd in range(N_HEADS):
            sl = slice(hd * HEAD_W, (hd + 1) * HEAD_W)
            osl = slice(N_HEADS * HEAD_W + hd * HEAD_W, N_HEADS * HEAD_W + (hd + 1) * HEAD_W)
            e, r = _softmax_terms(_nt_dot(qm_ref[qrows, sl], km_all[krows, sl]))
            o = _dot(e.astype(BF16), vm_all[krows, sl]) * r
            out_ref[qrows, osl] = o.astype(BF16)


def _attention(qkv, lw, layer, *, batch, seq, cache):
    qd, kd, vd, qm, km, vm = qkv
    w4 = N_HEADS * HEAD_W
    tq = min(ATTN_Q_TILE, seq)
    nq = seq // tq
    n_seqs = max(1, ATTN_Q_TILE // seq) if cache is None else 1
    assert batch % n_seqs == 0
    lam_init = 0.8 - 0.6 * math.exp(-0.3 * layer)
    q_spec = pl.BlockSpec((n_seqs * tq, w4), lambda b, j: (b * nq + j, 0))
    kv_spec = pl.BlockSpec((n_seqs * seq, w4), lambda b, j: (b, 0))
    args = [qd, qm, kd, vd, km, vm, lw["lq1"], lw["lk1"], lw["lq2"], lw["lk2"], lw["subln_g"]]
    specs = [q_spec, q_spec, kv_spec, kv_spec, kv_spec, kv_spec] + [_const_spec(a.shape) for a in args[6:]]
    scratch = []
    if cache is not None:
        ckd, cvd, cckv, ckr = cache
        n_cache = ckd.shape[2]
        for a in (ckd, cvd, cckv, ckr):
            args.append(a)
            specs.append(pl.BlockSpec((1, 1) + a.shape[2:], lambda b, j: (b, layer, 0, 0)))
        for name in ("w_kc", "w_kr_rows", "w_uv"):
            args.append(lw[name])
            specs.append(_const_spec(lw[name].shape))
        scratch = [pltpu.VMEM((n_cache + seq, w4), BF16)] * 4
    return pl.pallas_call(
        functools.partial(_attention_kernel, with_cache=cache is not None, lam_init=lam_init, n_seqs=n_seqs),
        grid=(batch // n_seqs, nq),
        in_specs=specs,
        out_specs=pl.BlockSpec((n_seqs * tq, 2 * w4), lambda b, j: (b * nq + j, 0)),
        out_shape=jax.ShapeDtypeStruct((batch * seq, 2 * w4), BF16),
        scratch_shapes=scratch,
        compiler_params=_params(("parallel", "arbitrary")),
        name="attention",
    )(*args)


def _post_mixer(x_ref, mixed_ref, mod_ref, wo_ref, g_ref):
    x1 = x_ref[...] + mod_ref[0, 2:3, :] * _dot(mixed_ref[...], wo_ref[...])
    h = _rms(x1, g_ref[...]) * (1.0 + mod_ref[0, 4:5, :]) + mod_ref[0, 3:4, :]
    return x1, h


def _silu(g):
    return g * jax.nn.sigmoid(g)


def _dense_ffn_kernel(x_ref, mixed_ref, mod_ref, wo_ref, g_ref, wg_ref, wu_ref, wd_ref, out_ref):
    x1, h = _post_mixer(x_ref, mixed_ref, mod_ref, wo_ref, g_ref)
    hb = h.astype(BF16)
    acc = jnp.zeros_like(x1)
    for c in range(D_FF // FF_CHUNK):
        sl = slice(c * FF_CHUNK, (c + 1) * FF_CHUNK)
        act = _silu(_dot(hb, wg_ref[:, sl])) * _dot(hb, wu_ref[:, sl])
        acc = acc + _dot(act.astype(BF16), wd_ref[sl, :])
    out_ref[...] = x1 + mod_ref[0, 5:6, :] * acc


def _dense_ffn(x, mixed, mod, lw, *, mod_row):
    n = x.shape[0]
    tm = FFN_TOKEN_TILE
    row = lambda i: (i, 0)
    weights = [lw["w_o"], lw["norm_ffn_g"], lw["w_gate"], lw["w_up"], lw["w_down"]]
    wspecs = [pl.BlockSpec(w.shape, lambda i: (0, 0), pipeline_mode=pl.Buffered(1)) for w in weights]
    return pl.pallas_call(
        _dense_ffn_kernel,
        grid=(n // tm,),
        in_specs=[pl.BlockSpec((tm, D_MODEL), row), pl.BlockSpec((tm, D_MODEL), row),
                  pl.BlockSpec((1, N_MOD, D_MODEL), lambda i: (mod_row(i, tm), 0, 0))] + wspecs,
        out_specs=pl.BlockSpec((tm, D_MODEL), row),
        out_shape=jax.ShapeDtypeStruct((n, D_MODEL), F32),
        compiler_params=_params(("parallel",)),
        name="dense_ffn",
    )(x, mixed, mod, *weights)


def _router_kernel(x_ref, mixed_ref, mod_ref, wo_ref, g_ref, router_ref,
                   x1_ref, h_ref, topi_ref, topg_ref, pos_ref, slabs_ref):
    x1, h = _post_mixer(x_ref, mixed_ref, mod_ref, wo_ref, g_ref)
    x1_ref[...] = x1
    h_ref[...] = h
    h_hi, h_lo = _split_bf16(h)
    r_hi, r_lo = _split_bf16(router_ref[...])
    hi_part = _nt_dot(jnp.concatenate([r_hi, r_lo], axis=0), h_hi)
    logits = hi_part[:N_EXPERTS] + hi_part[N_EXPERTS:] + _nt_dot(r_hi, h_lo)
    ex = jnp.exp(logits - jnp.max(logits, axis=0, keepdims=True))
    probs = ex / jnp.sum(ex, axis=0, keepdims=True)
    idx = lax.broadcasted_iota(jnp.int32, probs.shape, 0)
    p1 = jnp.max(probs, axis=0, keepdims=True)
    i1 = jnp.min(jnp.where(probs == p1, idx, N_EXPERTS), axis=0, keepdims=True)
    rest = jnp.where(idx == i1, -1.0, probs)
    p2 = jnp.max(rest, axis=0, keepdims=True)
    i2 = jnp.min(jnp.where(rest == p2, idx, N_EXPERTS), axis=0, keepdims=True)
    den = p1 + p2
    first = lax.broadcasted_iota(jnp.int32, topi_ref.shape, 0) == 0
    topi_ref[...] = jnp.where(first, i1, i2)
    topg_ref[...] = jnp.where(first, p1 / den, p2 / den)

    tt = idx.shape[1]
    hot1 = jnp.where(idx == i1, 1.0, 0.0)
    hot2 = jnp.where(idx == i2, 1.0, 0.0)
    tri = jnp.where(lax.broadcasted_iota(jnp.int32, (tt, tt), 0) <= lax.broadcasted_iota(jnp.int32, (tt, tt), 1),
                    1.0, 0.0).astype(BF16)
    before1 = _dot(hot1.astype(BF16), tri) - hot1
    before2 = _dot(hot2.astype(BF16), tri) - hot2
    count1 = jnp.sum(hot1, axis=1, keepdims=True)
    count2 = jnp.sum(hot2, axis=1, keepdims=True)
    slabs = jnp.floor((count1 + count2 + (SLAB_ROWS - 1)) * (1.0 / SLAB_ROWS))
    e_col = lax.broadcasted_iota(jnp.int32, (N_EXPERTS, 1), 0)
    seg_start = jnp.zeros_like(slabs)
    for e in range(N_EXPERTS - 1):
        seg_start = seg_start + jnp.where(e_col > e, slabs[e:e + 1, :] * SLAB_ROWS, 0.0)
    pos1 = jnp.sum(hot1 * (seg_start + before1), axis=0, keepdims=True)
    pos2 = jnp.sum(hot2 * (seg_start + count1 + before2), axis=0, keepdims=True)
    pos_ref[...] = jnp.where(first, pos1, pos2).astype(jnp.int32)
    slabs_ref[0] = slabs.astype(jnp.int32)


def _router(x, mixed, mod, lw, *, mod_row):
    n = x.shape[0]
    tm = MOE_TOKEN_TILE
    row = lambda i: (i, 0)
    weights = [lw["w_o"], lw["norm_ffn_g"], lw["router_t"]]
    i32 = jnp.int32
    col = lambda i: (0, i)
    x1, h, top_i, top_g, pos, slabs = pl.pallas_call(
        _router_kernel,
        grid=(n // tm,),
        in_specs=[pl.BlockSpec((tm, D_MODEL), row), pl.BlockSpec((tm, D_MODEL), row),
                  pl.BlockSpec((1, N_MOD, D_MODEL), lambda i: (mod_row(i, tm), 0, 0))]
                 + [_const_spec(w.shape) for w in weights],
        out_specs=[pl.BlockSpec((tm, D_MODEL), row), pl.BlockSpec((tm, D_MODEL), row),
                   pl.BlockSpec((TOP_K, tm), col), pl.BlockSpec((TOP_K, tm), col),
                   pl.BlockSpec((TOP_K, tm), col), pl.BlockSpec((1, N_EXPERTS, 1), lambda i: (i, 0, 0))],
        out_shape=[jax.ShapeDtypeStruct((n, D_MODEL), F32), jax.ShapeDtypeStruct((n, D_MODEL), F32),
                   jax.ShapeDtypeStruct((TOP_K, n), i32), jax.ShapeDtypeStruct((TOP_K, n), F32),
                   jax.ShapeDtypeStruct((TOP_K, n), i32), jax.ShapeDtypeStruct((n // tm, N_EXPERTS, 1), i32)],
        compiler_params=_params(("parallel",)),
        name="router",
    )(x, mixed, mod, *weights)
    return x1, h, top_i.T, top_g.T, pos.T, slabs.reshape(n // tm, 1, N_EXPERTS)


class _RoutePlan(NamedTuple):
    pos: jax.Array
    dst: jax.Array
    seg_slabs: jax.Array
    seg_local: jax.Array
    seg_global: jax.Array
    tile_expert: jax.Array
    n_used: jax.Array
    gsize: jax.Array
    gend: jax.Array


def _max_row_tiles(n_pairs, tt, tm):
    n_segments = n_pairs // (tt * TOP_K) * N_EXPERTS
    return (n_pairs + n_segments * (SLAB_ROWS - 1) + N_EXPERTS * (tm - 1)) // tm


def _route_plan(top_i, pos, slabs, tm):
    i32 = jnp.int32
    n = top_i.shape[0]
    n_tt = slabs.shape[0]
    tt = n // n_tt
    seg = slabs.reshape(n_tt, N_EXPERTS) * SLAB_ROWS
    seg_local = jnp.cumsum(seg, axis=1) - seg
    gsize = (jnp.sum(seg, axis=0) + tm - 1) // tm * tm
    gend = jnp.cumsum(gsize)
    seg_global = (gend - gsize)[None, :] + jnp.cumsum(seg, axis=0) - seg
    chosen = top_i.reshape(n_tt, tt, TOP_K)[..., None] == jnp.arange(N_EXPERTS, dtype=i32)
    shift = jnp.sum(jnp.where(chosen, (seg_global - seg_local)[:, None, None, :], 0), axis=-1)
    dst = (pos.reshape(n_tt, tt, TOP_K) + shift).reshape(-1)
    pos = pos.T
    n_tiles = _max_row_tiles(n * TOP_K, tt, tm)
    n_used = gend[-1] // tm
    tile_start = jnp.minimum(jnp.arange(n_tiles, dtype=i32), n_used - 1) * tm
    tile_expert = jnp.sum((gend[None, :] <= tile_start[:, None]).astype(i32), axis=1)
    flat = lambda a: a.reshape(-1).astype(i32)
    return _RoutePlan(pos.astype(i32), dst.astype(i32), flat(seg // SLAB_ROWS), flat(seg_local),
                      flat(seg_global), tile_expert.astype(i32), n_used.reshape(1).astype(i32),
                      gsize.astype(i32), gend.astype(i32))


def _for_segment_copies(n_slabs, fn):
    chunk_rows = CHUNK_SLABS * SLAB_ROWS
    n_chunks = n_slabs // CHUNK_SLABS

    def chunk(j, carry):
        fn(j * chunk_rows, chunk_rows)
        return carry

    def single(j, carry):
        fn(n_chunks * chunk_rows + j * SLAB_ROWS, SLAB_ROWS)
        return carry

    lax.fori_loop(0, n_chunks, chunk, 0)
    lax.fori_loop(0, n_slabs - n_chunks * CHUNK_SLABS, single, 0)


def _dispatch_kernel(slabs_ref, local_ref, global_ref, gsize_ref, gend_ref, nu_ref, pos_ref, *rest, group_tiles):
    h_refs = rest[:len(group_tiles)]
    xs_ref, sorted_ref, zero_ref, sem = rest[len(group_tiles):]
    i = pl.program_id(0)
    tm = zero_ref.shape[0]
    n_sorted, tt = sorted_ref.shape[1], h_refs[0].shape[0]

    @pl.when(i == 0)
    def _zero_unused():
        zero_ref[...] = jnp.zeros_like(zero_ref)

        def zero_tile(start):
            cp = pltpu.make_async_copy(zero_ref, xs_ref.at[pl.ds(pl.multiple_of(start, tm), tm)], sem.at[0])
            cp.start()
            cp.wait()

        for e in range(N_EXPERTS):
            @pl.when(gsize_ref[e] > 0)
            def _():
                zero_tile(gend_ref[e] - tm)

        def unused(t, carry):
            zero_tile(t * tm)
            return carry

        lax.fori_loop(nu_ref[0], xs_ref.shape[0] // tm, unused, 0)

    slot = i % 2

    def seg_copy(buf, src_row, dst_row, rows):
        return pltpu.make_async_copy(
            sorted_ref.at[buf, pl.ds(pl.multiple_of(src_row, SLAB_ROWS), rows)],
            xs_ref.at[pl.ds(pl.multiple_of(dst_row, SLAB_ROWS), rows)], sem.at[buf])

    def drain(tile, buf):
        for e in range(N_EXPERTS):
            _for_segment_copies(slabs_ref[tile * N_EXPERTS + e],
                                lambda off, rows: seg_copy(buf, 0, 0, rows).wait())

    @pl.when(i >= 2)
    def _():
        drain(i - 2, slot)

    rows = lax.broadcasted_iota(jnp.int32, (n_sorted, tt), 0)
    hit = (rows == pos_ref[0:1, :]) | (rows == pos_ref[1:2, :])
    onehot = jnp.where(hit, 1.0, 0.0).astype(BF16)
    tile0 = 0
    for h_ref, n_tiles in zip(h_refs, group_tiles):
        @pl.when((i >= tile0) & (i < tile0 + n_tiles))
        def _(h_ref=h_ref):
            sorted_ref[slot] = _dot(onehot, h_ref[...].astype(BF16)).astype(BF16)

        tile0 += n_tiles

    for e in range(N_EXPERTS):
        s = i * N_EXPERTS + e
        _for_segment_copies(
            slabs_ref[s],
            lambda off, rows, s=s: seg_copy(slot, local_ref[s] + off, global_ref[s] + off, rows).start())

    @pl.when(i == pl.num_programs(0) - 1)
    def _():
        @pl.when(i >= 1)
        def _():
            drain(i - 1, 1 - slot)

        drain(i, slot)


def _dispatch(hs, plan, tt, tm):
    group_tiles = tuple(h.shape[0] // tt for h in hs)
    n_tiles = sum(group_tiles)
    n_rows = _max_row_tiles(n_tiles * tt * TOP_K, tt, tm) * tm
    n_sorted = tt * TOP_K + N_EXPERTS * SLAB_ROWS
    in_specs = [pl.BlockSpec((TOP_K, tt), lambda i, *_: (0, i))]
    tile0 = 0
    for n_g in group_tiles:
        in_specs.append(pl.BlockSpec(
            (tt, D_MODEL), lambda i, *_, tile0=tile0, n_g=n_g: (jnp.clip(i - tile0, 0, n_g - 1), 0)))
        tile0 += n_g
    return pl.pallas_call(
        functools.partial(_dispatch_kernel, group_tiles=group_tiles),
        grid_spec=pltpu.PrefetchScalarGridSpec(
            num_scalar_prefetch=6,
            grid=(n_tiles,),
            in_specs=in_specs,
            out_specs=pl.BlockSpec(memory_space=pl.ANY),
            scratch_shapes=[pltpu.VMEM((2, n_sorted, D_MODEL), BF16), pltpu.VMEM((tm, D_MODEL), BF16),
                            pltpu.SemaphoreType.DMA((2,))],
        ),
        out_shape=jax.ShapeDtypeStruct((n_rows, D_MODEL), BF16),
        compiler_params=_params(("arbitrary",)),
        name="moe_dispatch",
    )(plan.seg_slabs, plan.seg_local, plan.seg_global, plan.gsize, plan.gend, plan.n_used, plan.pos, *hs)


def _expert_ffn_kernel(te_ref, nu_ref, xs_ref, wg_ref, wu_ref, wd_ref, ys_ref, wg_b, wu_b, wd_b):
    i = pl.program_id(0)

    @pl.when((i == 0) | (te_ref[i] != te_ref[jnp.maximum(i - 1, 0)]))
    def _new_expert():
        wg_b[...] = wg_ref[0].astype(BF16)
        wu_b[...] = wu_ref[0].astype(BF16)
        wd_b[...] = wd_ref[0].astype(BF16)

    @pl.when(i < nu_ref[0])
    def _compute():
        xb = xs_ref[...]
        y = jnp.zeros(ys_ref.shape, F32)
        for start, size in EXPERT_FF_CHUNKS:
            sl = slice(start, start + size)
            act = _silu(_dot(xb, wg_b[:, sl])) * _dot(xb, wu_b[:, sl])
            y = y + _dot(act.astype(BF16), wd_b[sl, :])
        ys_ref[...] = y

    @pl.when(i >= nu_ref[0])
    def _unused():
        ys_ref[...] = jnp.zeros_like(ys_ref)


def _expert_ffn(xs, plan, lw, tm):
    tile_expert, n_used = plan.tile_expert, plan.n_used
    n_tiles = xs.shape[0] // tm
    wspec = lambda w: pl.BlockSpec((1,) + w.shape[1:], lambda i, te, nu: (te[i], 0, 0))
    return pl.pallas_call(
        _expert_ffn_kernel,
        grid_spec=pltpu.PrefetchScalarGridSpec(
            num_scalar_prefetch=2,
            grid=(n_tiles,),
            in_specs=[pl.BlockSpec((tm, D_MODEL), lambda i, te, nu: (jnp.minimum(i, nu[0] - 1), 0)),
                      wspec(lw["moe_w_gate"]), wspec(lw["moe_w_up"]), wspec(lw["moe_w_down"])],
            out_specs=pl.BlockSpec((tm, D_MODEL), lambda i, te, nu: (i, 0)),
            scratch_shapes=[pltpu.VMEM((D_MODEL, D_FF_EXPERT), BF16), pltpu.VMEM((D_MODEL, D_FF_EXPERT), BF16),
                            pltpu.VMEM((D_FF_EXPERT, D_MODEL), BF16)],
        ),
        out_shape=jax.ShapeDtypeStruct(xs.shape, F32),
        compiler_params=_params(("arbitrary",)),
        name="expert_ffn",
    )(tile_expert, n_used, xs, lw["moe_w_gate"], lw["moe_w_up"], lw["moe_w_down"])


def _combine_kernel(dst_ref, x1_ref, topg_ref, mod_ref, fg_ref, ys_ref, out_ref, buf, sem):
    i = pl.program_id(0)
    n = pl.num_programs(0)
    tm = x1_ref.shape[0]

    def issue(tile, slot):
        base = tile * (tm * TOP_K)

        def body(r, carry):
            for k in range(TOP_K):
                d = dst_ref[base + TOP_K * r + k]
                pltpu.make_async_copy(ys_ref.at[pl.ds(d, 1)], buf.at[slot, k, pl.ds(r, 1)],
                                      sem.at[slot]).start()
            return carry

        lax.fori_loop(0, tm, body, 0, unroll=8)

    @pl.when(i == 0)
    def _first():
        issue(0, 0)

    @pl.when(i + 1 < n)
    def _ahead():
        issue(i + 1, (i + 1) % 2)

    slot = i % 2
    for k in range(TOP_K):
        pltpu.make_async_copy(ys_ref.at[pl.ds(0, tm)], buf.at[slot, k], sem.at[slot]).wait()
    g = topg_ref[...]
    f = g[:, 0:1] * buf[slot, 0] + g[:, 1:2] * buf[slot, 1]
    out_ref[...] = _rms(x1_ref[...] + mod_ref[0, 5:6, :] * f, fg_ref[...])


def _combine(x1, top_g, mod, final_g, ys, dst, *, mod_row):
    n = x1.shape[0]
    tm = COMBINE_TOKEN_TILE
    row = lambda i, *_: (i, 0)
    return pl.pallas_call(
        _combine_kernel,
        grid_spec=pltpu.PrefetchScalarGridSpec(
            num_scalar_prefetch=1,
            grid=(n // tm,),
            in_specs=[pl.BlockSpec((tm, D_MODEL), row), pl.BlockSpec((tm, TOP_K), row),
                      pl.BlockSpec((1, N_MOD, D_MODEL), lambda i, *_: (mod_row(i, tm), 0, 0)),
                      pl.BlockSpec(final_g.shape, lambda i, *_: (0, 0)),
                      pl.BlockSpec(memory_space=pl.ANY)],
            out_specs=pl.BlockSpec((tm, D_MODEL), row),
            scratch_shapes=[pltpu.VMEM((2, TOP_K, tm, D_MODEL), F32), pltpu.SemaphoreType.DMA((2,))],
        ),
        out_shape=jax.ShapeDtypeStruct((n, D_MODEL), F32),
        compiler_params=_params(("arbitrary",)),
        name="moe_combine",
    )(dst, x1, top_g, mod, final_g, ys)


def _sparse_moe(groups, lw, final_g):
    routed = [_router(x, mixed, mod, lw, mod_row=mod_row) for x, mixed, mod, mod_row in groups]
    plan = _route_plan(*(jnp.concatenate([r[k] for r in routed], axis=0) for k in (2, 4, 5)), MOE_ROW_TILE)
    xs = _dispatch([r[1] for r in routed], plan, MOE_TOKEN_TILE, MOE_ROW_TILE)
    ys = _expert_ffn(xs, plan, lw, MOE_ROW_TILE)
    outs, pair0 = [], 0
    for (_, _, mod, mod_row), (x1, _, _, top_g, _, _) in zip(groups, routed):
        n_pairs = x1.shape[0] * TOP_K
        outs.append(_combine(x1, top_g, mod, final_g, ys, plan.dst[pair0:pair0 + n_pairs], mod_row=mod_row))
        pair0 += n_pairs
    return outs


def _head_slots(w, used):
    k = w.shape[0]
    w = w.reshape(k, N_HEADS, used)
    return jnp.pad(w, ((0, 0), (0, 0), (0, HEAD_W - used))).reshape(k, N_HEADS * HEAD_W)


def _layer_weights(l, P):
    row = lambda v: v.reshape(1, -1)
    place = jnp.zeros((HEAD_W, N_HEADS, HEAD_W), F32)
    r = jnp.arange(MLA_ROPE_DIM)
    place = place.at[r, :, MLA_NOPE_DIM + r].set(1.0).reshape(HEAD_W, N_HEADS * HEAD_W)
    lw = {
        "norm_mix_g": row(P["norm_mix_g"][l]),
        "norm_ffn_g": row(P["norm_ffn_g"][l]),
        "w_in": jnp.pad(P["w_in"][l], ((0, 0), (0, W_IN_PAD - O_END))).astype(BF16),
        "q_norm_g": row(P["mla_q_norm_g"][l]),
        "kv_norm_g": row(P["mla_kv_norm_g"][l]),
        "w_uq": _head_slots(P["w_uq"][l], MLA_NOPE_DIM + MLA_ROPE_DIM).astype(BF16),
        "w_kc": _head_slots(P["w_uk"][l], MLA_NOPE_DIM).astype(BF16),
        "w_kr": place.astype(BF16),
        "w_kr_rows": place[:MLA_ROPE_DIM].astype(BF16),
        "w_uv": P["w_uv"][l].astype(BF16),
        "lq1": row(P["diff_lq1"][l]), "lk1": row(P["diff_lk1"][l]),
        "lq2": row(P["diff_lq2"][l]), "lk2": row(P["diff_lk2"][l]),
        "subln_g": row(P["diff_subln_g"][l]),
        "w_o": P["w_o"][l].astype(BF16),
    }
    if l % 2 == 0:
        i = l // 2
        lw.update(w_gate=P["w_gate"][i].astype(BF16), w_up=P["w_up"][i].astype(BF16),
                  w_down=P["w_down"][i].astype(BF16))
    else:
        m = l // 2
        lw.update(router_t=P["router"][m].T, moe_w_gate=P["moe_w_gate"][m],
                  moe_w_up=P["moe_w_up"][m], moe_w_down=P["moe_w_down"][m])
    return lw


def _run_group(x, mods, weights, *, batch, seq, mod_row, tables, caches, cache_out):
    assert DEPTH % 2 == 0
    own = []
    for l in range(DEPTH):
        lw = weights[l]
        if not cache_out:
            outs = _pre_mixer(x, mods[l], lw, tables, seq=seq, mod_row=mod_row)
        elif l < DEPTH - 1:
            outs = _pre_mixer(x, mods[l], lw, tables, seq=seq, mod_row=mod_row, cache_out="flat")
            own.append(outs[6:])
        else:
            outs = _pre_mixer(x, mods[l], lw, tables, seq=seq, mod_row=mod_row, cache_out="stacked",
                              prev=own)
            own = outs[6:]
        mixed = _attention(outs[:6], lw, l, batch=batch, seq=seq, cache=caches)
        if l % 2 == 0:
            x = _dense_ffn(x, mixed, mods[l], lw, mod_row=mod_row)
        else:
            assert l == DEPTH - 1
    return (x, mixed, mods[DEPTH - 1], mod_row), own


def kernel(x_prompt, x_sample, cache_diff_k, cache_diff_v, cache_mla_ckv, cache_mla_krope, c, c_ctx, w_ada, b_ada, norm_mix_g, norm_ffn_g, w_in, mla_q_norm_g, mla_kv_norm_g, w_uq, w_uk, w_uv, diff_lq1, diff_lk1, diff_lq2, diff_lk2, diff_subln_g, w_o, w_gate, w_up, w_down, router, moe_w_gate, moe_w_up, moe_w_down, final_norm_g):
    P = dict(norm_mix_g=norm_mix_g, norm_ffn_g=norm_ffn_g, w_in=w_in, mla_q_norm_g=mla_q_norm_g,
             mla_kv_norm_g=mla_kv_norm_g, w_uq=w_uq, w_uk=w_uk, w_uv=w_uv, diff_lq1=diff_lq1,
             diff_lk1=diff_lk1, diff_lq2=diff_lq2, diff_lk2=diff_lk2, diff_subln_g=diff_subln_g,
             w_o=w_o, w_gate=w_gate, w_up=w_up, w_down=w_down, router=router,
             moe_w_gate=moe_w_gate, moe_w_up=moe_w_up, moe_w_down=moe_w_down)
    bp, sp, d = x_prompt.shape
    bs, ss, _ = x_sample.shape
    n_past = cache_diff_k.shape[2]
    w4 = N_HEADS * HEAD_W

    cond = jnp.zeros((COND_ROWS, d), F32).at[0].set(c_ctx).at[1:1 + bs].set(c)
    mod_all = _ada_table(cond, w_ada, b_ada).reshape(DEPTH, COND_ROWS, N_MOD, d)
    mods = [mod_all[l] for l in range(DEPTH)]
    weights = [_layer_weights(l, P) for l in range(DEPTH)]
    final_g = final_norm_g.reshape(1, d)

    moe_in_p, own = _run_group(
        x_prompt.reshape(bp * sp, d), mods, weights, batch=bp, seq=sp,
        mod_row=lambda i, tm: 0, tables=None, caches=None, cache_out=True)
    new_diff_k, new_diff_v, new_mla_ckv, new_mla_krope = own

    caches = (cache_diff_k.reshape(bs, DEPTH, n_past, w4), cache_diff_v.reshape(bs, DEPTH, n_past, w4),
              cache_mla_ckv, cache_mla_krope)
    moe_in_s, _ = _run_group(
        x_sample.reshape(bs * ss, d), mods, weights, batch=bs, seq=ss,
        mod_row=lambda i, tm: 1 + (i * tm) // ss,
        tables=_rope_tables(ss), caches=caches, cache_out=False)

    yp, ys = _sparse_moe([moe_in_p, moe_in_s], weights[DEPTH - 1], final_g)
    y_prompt = yp.reshape(bp, sp, d)
    y_sample = ys.reshape(bs, ss, d)

    return (y_prompt, y_sample, new_diff_k, new_diff_v, new_mla_ckv, new_mla_krope)
```
